```python
import jax
import jax.numpy as jnp
from jax import lax
import numpy as np

D_MODEL = 4096
BATCH = 32
SEQ = 256
DEPTH = 4
DEC_BATCH = 2
DEC_SEQ = 4096
PAST_LEN = 256

GRID_W = 64
EPS = 1e-6
L2_EPS = 1e-6
GDN_W = D_MODEL // 4
GDN_HEAD = 128
GDN_HEADS = GDN_W // GDN_HEAD
GDN_CHUNK = 64
SHORT_CONV = 3
RWKV_W = D_MODEL // 4
RWKV_HEAD = 64
RWKV_HEADS = RWKV_W // RWKV_HEAD
RWKV_DECAY_LORA = 64
RWKV_A_LORA = 64
RWKV_GATE_LORA = 128
RWKV_GN_EPS = RWKV_HEAD * 1e-5
MLA_V = 128
MLA_HEADS = (D_MODEL // 2) // MLA_V
MLA_Q_RANK = 1536
MLA_KV_RANK = 512
MLA_NOPE = 128
MLA_ROPE = 64
MLA_QK = MLA_NOPE + MLA_ROPE
ROPE_THETA = 10000.0
Q_BLOCK = 128
MIX_W = GDN_W + RWKV_W + MLA_HEADS * MLA_V
GDN_COLS = 4 * GDN_W + 4 * GDN_HEADS
RWKV_SHIFT_COLS = 3 * RWKV_W + RWKV_DECAY_LORA + RWKV_A_LORA
RWKV_COLS = RWKV_SHIFT_COLS + RWKV_GATE_LORA
MLA_COLS = MLA_Q_RANK + MLA_KV_RANK + MLA_ROPE
IN_COLS = GDN_COLS + RWKV_COLS + MLA_COLS
N_GROUPS = 8
EXPERTS_PER_GROUP = 8
N_EXPERTS = N_GROUPS * EXPERTS_PER_GROUP
TOP_K = 2
D_EXPERT = 512
MOE_BLOCK = 128

kernel_name = 'hybrid_dit_gdn_rwkv7_mla_hmoe'
F32 = jnp.float32


def rms_norm(x, w):
    xf = x.astype(F32)
    y = xf * lax.rsqrt(jnp.mean(xf * xf, axis=-1, keepdims=True) + EPS)
    return (y * w.astype(F32)).astype(x.dtype)


def l2_normalize(x):
    xf = x.astype(F32)
    return xf * lax.rsqrt(jnp.sum(xf * xf, axis=-1, keepdims=True) + L2_EPS)


def axial_rope(x, row, col):
    half = x.shape[-1] // 2
    nfreq = half // 2
    inv = ROPE_THETA ** (-jnp.arange(nfreq, dtype=F32) / nfreq)
    bshape = (row.shape[0],) + (1,) * (x.ndim - 3) + (nfreq,)

    def rot(xa, pos):
        ang = (pos.astype(F32)[:, None] * inv).reshape(bshape)
        cos, sin = jnp.cos(ang), jnp.sin(ang)
        x1, x2 = xa[..., :nfreq], xa[..., nfreq:]
        return jnp.concatenate([x1 * cos - x2 * sin, x2 * cos + x1 * sin], axis=-1)

    xf = x.astype(F32)
    return jnp.concatenate([rot(xf[..., :half], row), rot(xf[..., half:], col)], axis=-1).astype(x.dtype)


def attend(q, k, v):
    B, Tq, H, dqk = q.shape
    nb = Tq // Q_BLOCK
    qb = jnp.moveaxis(q.reshape(B, nb, Q_BLOCK, H, dqk), 1, 0)
    scale = dqk ** -0.5

    def one(qblk):
        s = jnp.einsum('bqhd,bkhd->bhqk', qblk, k, preferred_element_type=F32) * scale
        p = jax.nn.softmax(s, axis=-1).astype(v.dtype)
        return jnp.einsum('bhqk,bkhd->bqhd', p, v)

    o = lax.map(one, qb)
    return jnp.moveaxis(o, 0, 1).reshape(B, Tq, H, v.shape[-1])


def short_conv(x, w):
    C = x.shape[-1]
    rhs = w.T[:, None, :].astype(x.dtype)
    return lax.conv_general_dilated(x, rhs, window_strides=(1,), padding=[(SHORT_CONV // 2, SHORT_CONV // 2)],
                                    dimension_numbers=('NWC', 'WIO', 'NWC'), feature_group_count=C)


def gdn_chunked(q, k, v, g, beta, s0):
    B, T, H, _ = q.shape
    Vd = v.shape[-1]
    N = T // GDN_CHUNK

    def chunks(t):
        return jnp.swapaxes(t.astype(F32).reshape((B, N, GDN_CHUNK, H) + t.shape[3:]), 2, 3)

    q, k, v, g, beta = chunks(q), chunks(k), chunks(v), chunks(g), chunks(beta)
    gc = jnp.cumsum(g, axis=-1)
    lower = jnp.tril(jnp.ones((GDN_CHUNK, GDN_CHUNK), bool))
    decay = jnp.exp(jnp.where(lower, gc[..., :, None] - gc[..., None, :], -jnp.inf))
    kb = k * beta[..., None]
    a_mat = jnp.einsum('bnhcd,bnhsd->bnhcs', kb, k) * decay
    rhs = jnp.concatenate([v * beta[..., None], kb * jnp.exp(gc)[..., None]], axis=-1)
    sol = lax.linalg.triangular_solve(a_mat, rhs, left_side=True, lower=True, unit_diagonal=True)
    u, w = sol[..., :Vd], sol[..., Vd:]
    attn = jnp.einsum('bnhcd,bnhsd->bnhcs', q, k) * decay
    q_dec = q * jnp.exp(gc)[..., None]
    k_dec = k * jnp.exp(gc[..., -1:] - gc)[..., None]
    c_dec = jnp.exp(gc[..., -1])

    def step(s, xs):
        u_n, w_n, attn_n, qd_n, kd_n, cd_n = xs
        v_new = u_n - jnp.einsum('bhck,bhkv->bhcv', w_n, s)
        o_n = jnp.einsum('bhck,bhkv->bhcv', qd_n, s) + jnp.einsum('bhcs,bhsv->bhcv', attn_n, v_new)
        s = s * cd_n[..., None, None] + jnp.einsum('bhck,bhcv->bhkv', kd_n, v_new)
        return s, o_n

    xs = tuple(jnp.moveaxis(t, 1, 0) for t in (u, w, attn, q_dec, k_dec, c_dec))
    s, o = lax.scan(step, s0.astype(F32), xs)
    o = jnp.swapaxes(jnp.moveaxis(o, 0, 1), 2, 3).reshape(B, T, H, Vd)
    return o, s


def gdn_mixer(p, s0, lp):
    B, T, _ = p.shape
    qkv, z, a, b = jnp.split(p, [3 * GDN_W, 4 * GDN_W, 4 * GDN_W + 2 * GDN_HEADS], axis=-1)
    qkv = jax.nn.silu(short_conv(qkv, lp['gdn_conv_w']))
    q, k, v = [t.reshape(B, T, GDN_HEADS, GDN_HEAD) for t in jnp.split(qkv, 3, axis=-1)]
    q = l2_normalize(q) * (GDN_HEAD ** -0.5)
    k = l2_normalize(k)
    a = a.astype(F32).reshape(B, T, 2, GDN_HEADS)
    beta = jax.nn.sigmoid(b.astype(F32).reshape(B, T, 2, GDN_HEADS))
    g = -jnp.exp(lp['gdn_A_log'].astype(F32)) * jax.nn.softplus(a + lp['gdn_dt_bias'].astype(F32))
    o_f, s_f = gdn_chunked(q, k, v, g[:, :, 0], beta[:, :, 0], s0[:, 0])
    rev = lambda t: jnp.flip(t, axis=1)
    o_b, s_b = gdn_chunked(rev(q), rev(k), rev(v), rev(g[:, :, 1]), rev(beta[:, :, 1]), s0[:, 1])
    o = o_f + rev(o_b)
    o = rms_norm(o, lp['gdn_norm_w']) * jax.nn.silu(z.astype(F32).reshape(B, T, GDN_HEADS, GDN_HEAD))
    return o.reshape(B, T, GDN_W).astype(p.dtype), jnp.stack([s_f, s_b], axis=1)


def token_shift(f, direction):
    if direction == 0:
        return jnp.pad(f, ((0, 0), (1, 0), (0, 0)))[:, :-1]
    return jnp.pad(f, ((0, 0), (0, 1), (0, 0)))[:, 1:]


def rwkv_step(s, inp):
    r, w, k, v, kk, a = inp
    sa = jnp.einsum('bhij,bhj->bhi', s, -kk)
    s = s * w[:, :, None, :] + sa[..., None] * (kk * a)[:, :, None, :] + v[..., None] * k[:, :, None, :]
    return s, jnp.einsum('bhij,bhj->bhi', s, r)


def rwkv_direction(feat, d, s0, lp):
    B, T, _ = feat.shape
    f = (feat + (token_shift(feat, d) - feat) * lp['rwkv_mu'][d]).astype(F32)
    r, k, v, fw, fa = jnp.split(f, [RWKV_W, 2 * RWKV_W, 3 * RWKV_W, 3 * RWKV_W + RWKV_DECAY_LORA], axis=-1)
    w_log = -jax.nn.softplus(-(lp['rwkv_w0'][d] + jnp.tanh(fw) @ lp['rwkv_w_up'][d])) - 0.5
    a = jax.nn.sigmoid(lp['rwkv_a0'][d] + fa @ lp['rwkv_a_up'][d])
    heads = lambda t: t.astype(F32).reshape(B, T, RWKV_HEADS, RWKV_HEAD)
    per_head = lambda t: t.astype(F32).reshape(RWKV_HEADS, RWKV_HEAD)
    r, k, v, a = heads(r), heads(k), heads(v), heads(a)
    decay = jnp.exp(-jnp.exp(heads(w_log)))
    kk = l2_normalize(k * per_head(lp['rwkv_k_k']))
    k = k * (1.0 + (a - 1.0) * per_head(lp['rwkv_k_a']))
    xs = tuple(jnp.swapaxes(t, 0, 1) for t in (r, decay, k, v, kk, a))
    s, y = lax.scan(rwkv_step, s0.astype(F32), xs, reverse=(d == 1))
    y = jnp.swapaxes(y, 0, 1)
    mu = jnp.mean(y, axis=-1, keepdims=True)
    var = jnp.mean(jnp.square(y - mu), axis=-1, keepdims=True)
    y = (y - mu) * lax.rsqrt(var + RWKV_GN_EPS) * per_head(lp['rwkv_ln_w']) + per_head(lp['rwkv_ln_b'])
    y = y + jnp.sum(r * k * lp['rwkv_r_k'].astype(F32), axis=-1, keepdims=True) * v
    return y.reshape(B, T, RWKV_W), s


def rwkv_mixer(p, s0, lp):
    feat, fg = p[..., :RWKV_SHIFT_COLS], p[..., RWKV_SHIFT_COLS:]
    y_f, s_f = rwkv_direction(feat, 0, s0[:, 0], lp)
    y_b, s_b = rwkv_direction(feat, 1, s0[:, 1], lp)
    g = jax.nn.sigmoid(fg.astype(F32)) @ lp['rwkv_g_up'].astype(F32)
    return ((y_f + y_b) * g).astype(p.dtype), jnp.stack([s_f, s_b], axis=1)


def mla_keys_values(ckv, kr, kv_up):
    B, T, _ = ckv.shape
    kv = (ckv @ kv_up).reshape(B, T, MLA_HEADS, MLA_NOPE + MLA_V)
    k = jnp.concatenate([kv[..., :MLA_NOPE], jnp.broadcast_to(kr[:, :, None, :], (B, T, MLA_HEADS, MLA_ROPE)).astype(kv.dtype)], axis=-1)
    return k, kv[..., MLA_NOPE:]


def mla_mixer(p, ckv_ctx, kr_ctx, pos, lp):
    B, T, _ = p.shape
    cq, ckv, kr = jnp.split(p, [MLA_Q_RANK, MLA_Q_RANK + MLA_KV_RANK], axis=-1)
    q = (rms_norm(cq, lp['mla_q_norm_w']) @ lp['mla_q_up']).reshape(B, T, MLA_HEADS, MLA_QK)
    ckv = rms_norm(ckv, lp['mla_kv_norm_w'])
    q_nope, q_rope = q[..., :MLA_NOPE], q[..., MLA_NOPE:]
    if pos is None:
        kr_pos = kr
    else:
        q_rope = axial_rope(q_rope, pos[0], pos[1])
        kr_pos = axial_rope(kr, pos[0], pos[1])
    q = jnp.concatenate([q_nope, q_rope], axis=-1)
    k, v = mla_keys_values(ckv, kr_pos, lp['mla_kv_up'])
    if ckv_ctx is not None:
        k_c, v_c = mla_keys_values(ckv_ctx, kr_ctx, lp['mla_kv_up'])
        k = jnp.concatenate([k_c, k], axis=1)
        v = jnp.concatenate([v_c, v], axis=1)
    o = attend(q, k, v)
    return o.reshape(B, T, MLA_HEADS * MLA_V), ckv, kr


def mixer(h, lp, ctx, pos):
    B = h.shape[0]
    proj = h @ lp['w_in']
    p_gdn, p_rwkv, p_mla = jnp.split(proj, [GDN_COLS, GDN_COLS + RWKV_COLS], axis=-1)
    if ctx is None:
        s_gdn0 = jnp.zeros((B, 2, GDN_HEADS, GDN_HEAD, GDN_HEAD), F32)
        s_rwkv0 = jnp.zeros((B, 2, RWKV_HEADS, RWKV_HEAD, RWKV_HEAD), F32)
        ckv_ctx, kr_ctx = None, None
    else:
        s_gdn0, s_rwkv0, ckv_ctx, kr_ctx = ctx
    o_gdn, s_gdn = gdn_mixer(p_gdn, s_gdn0, lp)
    o_rwkv, s_rwkv = rwkv_mixer(p_rwkv, s_rwkv0, lp)
    o_mla, ckv, kr = mla_mixer(p_mla, ckv_ctx, kr_ctx, pos, lp)
    out = jnp.concatenate([o_gdn, o_rwkv, o_mla], axis=-1) @ lp['w_out']
    return out, (s_gdn, s_rwkv, ckv, kr)


def moe_ffn(h, lp):
    B, T, D = h.shape
    n_tok = B * T
    x = h.reshape(n_tok, D)
    gl = jnp.einsum('td,dg->tg', x, lp['router_group_w'], preferred_element_type=F32) + lp['router_group_b'].astype(F32)
    pg = jax.nn.softmax(gl, axis=-1)
    grp = jnp.argmax(gl, axis=-1)
    pg_sel = jnp.max(pg, axis=-1, keepdims=True)
    el = jnp.einsum('td,de->te', x, lp['router_expert_w'], preferred_element_type=F32) + lp['router_expert_b'].astype(F32)
    el = el.reshape(n_tok, N_GROUPS, EXPERTS_PER_GROUP)[jnp.arange(n_tok), grp]
    top_p, top_i = lax.top_k(jax.nn.softmax(el, axis=-1), TOP_K)
    gate = pg_sel * top_p / jnp.sum(top_p, axis=-1, keepdims=True)
    expert = grp[:, None].astype(jnp.int32) * EXPERTS_PER_GROUP + top_i.astype(jnp.int32)
    n_assign = n_tok * TOP_K
    flat_e = expert.reshape(-1)
    flat_t = jnp.repeat(jnp.arange(n_tok, dtype=jnp.int32), TOP_K)
    flat_g = gate.reshape(-1)
    order = jnp.argsort(flat_e)
    e_sorted = flat_e[order]
    counts = jnp.bincount(flat_e, length=N_EXPERTS)
    padded = (counts + MOE_BLOCK - 1) // MOE_BLOCK * MOE_BLOCK
    padded_end = jnp.cumsum(padded)
    padded_start = padded_end - padded
    start = jnp.cumsum(counts) - counts
    dest = padded_start[e_sorted] + jnp.arange(n_assign, dtype=jnp.int32) - start[e_sorted]
    n_slots = -(-n_assign // MOE_BLOCK) * MOE_BLOCK + N_EXPERTS * MOE_BLOCK
    n_blocks = n_slots // MOE_BLOCK
    slot_tok = jnp.full((n_slots,), n_tok, jnp.int32).at[dest].set(flat_t[order])
    slot_gate = jnp.zeros((n_slots,), F32).at[dest].set(flat_g[order])
    block_e = jnp.minimum(jnp.searchsorted(padded_end, jnp.arange(n_blocks, dtype=jnp.int32) * MOE_BLOCK, side='right'), N_EXPERTS - 1)
    x_pad = jnp.concatenate([x, jnp.zeros((1, D), x.dtype)], axis=0)

    def expert_block(args):
        tok, gt, e = args
        xb = x_pad[tok]
        hid = jax.nn.silu(xb @ lp['moe_w_gate'][e]) * (xb @ lp['moe_w_up'][e])
        return (hid @ lp['moe_w_down'][e]) * gt[:, None].astype(x.dtype)

    yb = lax.map(expert_block, (slot_tok.reshape(n_blocks, MOE_BLOCK), slot_gate.reshape(n_blocks, MOE_BLOCK), block_e))
    y = jnp.zeros((n_tok + 1, D), x.dtype).at[slot_tok].add(yb.reshape(n_slots, D))
    return y[:n_tok].reshape(B, T, D)


def block(x, mod, lp, ctx, pos):
    sh1, sc1, g1, sh2, sc2, g2 = jnp.split(mod, 6, axis=-1)
    h = rms_norm(x, lp['norm_mix_w']) * (1.0 + sc1) + sh1
    mix, ctx_tensors = mixer(h, lp, ctx, pos)
    x = x + g1 * mix
    h = rms_norm(x, lp['norm_ffn_w']) * (1.0 + sc2) + sh2
    x = x + g2 * moe_ffn(h, lp)
    return x, ctx_tensors


def setup_inputs(seed: int = 0) -> dict:
    key = jax.random.key(seed)
    ks = iter(jax.random.split(key, 48))
    L, D = DEPTH, D_MODEL

    def nrm(shape, scale):
        return jax.random.normal(next(ks), shape, F32) * scale

    def unif(shape, lo, hi):
        return jax.random.uniform(next(ks), shape, F32, lo, hi)

    inp = {}
    inp['x_prompt'] = nrm((BATCH, SEQ, D), 1.0)
    inp['x_sample'] = nrm((DEC_BATCH, DEC_SEQ, D), 1.0)
    inp['state_gdn'] = nrm((DEC_BATCH, L, 2, GDN_HEADS, GDN_HEAD, GDN_HEAD), 0.3)
    inp['state_rwkv'] = nrm((DEC_BATCH, L, 2, RWKV_HEADS, RWKV_HEAD, RWKV_HEAD), 0.3)
    inp['cache_mla_ckv'] = nrm((DEC_BATCH, L, PAST_LEN, MLA_KV_RANK), 1.0)
    inp['cache_mla_krope'] = nrm((DEC_BATCH, L, PAST_LEN, MLA_ROPE), 1.0)
    inp['c'] = nrm((DEC_BATCH, D), 1.0)
    inp['c_ctx'] = nrm((D,), 1.0)
    inp['ada_w'] = nrm((L, D, 6 * D), 0.5 * D ** -0.5)
    inp['ada_b'] = nrm((L, 6 * D), 0.02)
    inp['norm_mix_w'] = 1.0 + nrm((L, D), 0.05)
    inp['norm_ffn_w'] = 1.0 + nrm((L, D), 0.05)
    inp['w_in'] = nrm((L, D, IN_COLS), D ** -0.5)
    inp['gdn_conv_w'] = nrm((L, 3 * GDN_W, SHORT_CONV), SHORT_CONV ** -0.5)
    inp['gdn_A_log'] = jnp.log(unif((L, 2, GDN_HEADS), 1.0, 16.0))
    dt = jnp.exp(unif((L, 2, GDN_HEADS), float(np.log(1e-3)), float(np.log(1e-1))))
    inp['gdn_dt_bias'] = dt + jnp.log(-jnp.expm1(-dt))
    inp['gdn_norm_w'] = 1.0 + nrm((L, GDN_HEAD), 0.05)
    inp['rwkv_mu'] = unif((L, 2, RWKV_SHIFT_COLS), 0.0, 1.0)
    inp['rwkv_w0'] = unif((L, 2, RWKV_W), -5.0, 0.0)
    inp['rwkv_w_up'] = nrm((L, 2, RWKV_DECAY_LORA, RWKV_W), 0.1)
    inp['rwkv_a0'] = nrm((L, 2, RWKV_W), 0.1)
    inp['rwkv_a_up'] = nrm((L, 2, RWKV_A_LORA, RWKV_W), 0.1)
    inp['rwkv_g_up'] = nrm((L, RWKV_GATE_LORA, RWKV_W), RWKV_GATE_LORA ** -0.5)
    inp['rwkv_k_k'] = 0.85 + nrm((L, RWKV_W), 0.05)
    inp['rwkv_k_a'] = 1.0 + nrm((L, RWKV_W), 0.05)
    inp['rwkv_r_k'] = nrm((L, RWKV_HEADS, RWKV_HEAD), 0.1)
    inp['rwkv_ln_w'] = 1.0 + nrm((L, RWKV_W), 0.05)
    inp['rwkv_ln_b'] = nrm((L, RWKV_W), 0.02)
    inp['mla_q_norm_w'] = 1.0 + nrm((L, MLA_Q_RANK), 0.05)
    inp['mla_q_up'] = nrm((L, MLA_Q_RANK, MLA_HEADS * MLA_QK), MLA_Q_RANK ** -0.5)
    inp['mla_kv_norm_w'] = 1.0 + nrm((L, MLA_KV_RANK), 0.05)
    inp['mla_kv_up'] = nrm((L, MLA_KV_RANK, MLA_HEADS * (MLA_NOPE + MLA_V)), MLA_KV_RANK ** -0.5)
    inp['w_out'] = nrm((L, MIX_W, D), MIX_W ** -0.5)
    inp['router_group_w'] = nrm((L, D, N_GROUPS), D ** -0.5)
    inp['router_group_b'] = nrm((L, N_GROUPS), 0.01)
    inp['router_expert_w'] = nrm((L, D, N_EXPERTS), D ** -0.5)
    inp['router_expert_b'] = nrm((L, N_EXPERTS), 0.01)
    inp['moe_w_gate'] = nrm((L, N_EXPERTS, D, D_EXPERT), D ** -0.5)
    inp['moe_w_up'] = nrm((L, N_EXPERTS, D, D_EXPERT), D ** -0.5)
    inp['moe_w_down'] = nrm((L, N_EXPERTS, D_EXPERT, D), D_EXPERT ** -0.5)
    inp['final_norm_w'] = 1.0 + nrm((D,), 0.05)
    return inp


def reference(x_prompt, x_sample, state_gdn, state_rwkv, cache_mla_ckv, cache_mla_krope, c, c_ctx,
              ada_w, ada_b, norm_mix_w, norm_ffn_w, w_in, gdn_conv_w, gdn_A_log, gdn_dt_bias, gdn_norm_w,
              rwkv_mu, rwkv_w0, rwkv_w_up, rwkv_a0, rwkv_a_up, rwkv_g_up, rwkv_k_k, rwkv_k_a, rwkv_r_k,
              rwkv_ln_w, rwkv_ln_b, mla_q_norm_w, mla_q_up, mla_kv_norm_w, mla_kv_up, w_out,
              router_group_w, router_group_b, router_expert_w, router_expert_b,
              moe_w_gate, moe_w_up, moe_w_down, final_norm_w):
    t_lat = x_sample.shape[1]
    rows = t_lat // GRID_W
    row, col = jnp.meshgrid(jnp.arange(rows, dtype=jnp.int32), jnp.arange(GRID_W, dtype=jnp.int32), indexing='ij')
    pos = (row.reshape(-1), col.reshape(-1))
    xp, xs = x_prompt, x_sample
    new_gdn, new_rwkv, new_ckv, new_kr = [], [], [], []
    for l in range(DEPTH):
        lp = dict(norm_mix_w=norm_mix_w[l], norm_ffn_w=norm_ffn_w[l], w_in=w_in[l],
                  gdn_conv_w=gdn_conv_w[l], gdn_A_log=gdn_A_log[l], gdn_dt_bias=gdn_dt_bias[l], gdn_norm_w=gdn_norm_w[l],
                  rwkv_mu=rwkv_mu[l], rwkv_w0=rwkv_w0[l], rwkv_w_up=rwkv_w_up[l], rwkv_a0=rwkv_a0[l],
                  rwkv_a_up=rwkv_a_up[l], rwkv_g_up=rwkv_g_up[l], rwkv_k_k=rwkv_k_k[l], rwkv_k_a=rwkv_k_a[l],
                  rwkv_r_k=rwkv_r_k[l], rwkv_ln_w=rwkv_ln_w[l], rwkv_ln_b=rwkv_ln_b[l],
                  mla_q_norm_w=mla_q_norm_w[l], mla_q_up=mla_q_up[l], mla_kv_norm_w=mla_kv_norm_w[l],
                  mla_kv_up=mla_kv_up[l], w_out=w_out[l],
                  router_group_w=router_group_w[l], router_group_b=router_group_b[l],
                  router_expert_w=router_expert_w[l], router_expert_b=router_expert_b[l],
                  moe_w_gate=moe_w_gate[l], moe_w_up=moe_w_up[l], moe_w_down=moe_w_down[l])
        aw, ab = ada_w[l], ada_b[l]
        mod_ctx = (jax.nn.silu(c_ctx) @ aw + ab)[None, None, :]
        mod_lat = (jax.nn.silu(c) @ aw + ab)[:, None, :]
        xp, (s_gdn, s_rwkv, ckv, kr) = block(xp, mod_ctx, lp, None, None)
        new_gdn.append(s_gdn)
        new_rwkv.append(s_rwkv)
        new_ckv.append(ckv)
        new_kr.append(kr)
        ctx = (state_gdn[:, l], state_rwkv[:, l], cache_mla_ckv[:, l], cache_mla_krope[:, l])
        xs, _ = block(xs, mod_lat, lp, ctx, pos)
    y_prompt = rms_norm(xp, final_norm_w)
    y_sample = rms_norm(xs, final_norm_w)
    dt_out = x_prompt.dtype
    return (y_prompt, y_sample,
            jnp.stack(new_gdn, axis=1).astype(dt_out), jnp.stack(new_rwkv, axis=1).astype(dt_out),
            jnp.stack(new_ckv, axis=1).astype(dt_out), jnp.stack(new_kr, axis=1).astype(dt_out))
```

```python
import functools

import numpy as np
import jax
import jax.numpy as jnp
from jax import lax
from jax.experimental import pallas as pl
from jax.experimental.pallas import tpu as pltpu

F32 = jnp.float32
BF16 = jnp.bfloat16
HI = lax.Precision.HIGHEST

EPS = 1e-6
L2_EPS = 1e-6
GRID_W = 64
ROPE_THETA = 10000.0
GDN_HEAD = 128
RWKV_HEAD = 64
RWKV_GN_EPS = RWKV_HEAD * 1e-5
MLA_V = 128
MLA_NOPE = 128
MLA_ROPE = 64
MLA_QK = MLA_NOPE + MLA_ROPE
TOP_K = 2
CHUNK = 64
LANE = 128
SUB = 8
VMEM_LIMIT = 52 * 1024 * 1024


def _cp(*sem, vmem=VMEM_LIMIT):
    return pltpu.CompilerParams(dimension_semantics=sem, vmem_limit_bytes=vmem)


def _pick(prefs, *ns):
    for p in prefs:
        if all(n % p == 0 for n in ns):
            return p
    raise ValueError(f"no tile in {prefs} divides {ns}")


def _dot(a, b, prec=None):
    return lax.dot_general(a, b, (((1,), (0,)), ((), ())), precision=prec, preferred_element_type=F32)


def _dot_nt(a, b, prec=None):
    return lax.dot_general(a, b, (((1,), (1,)), ((), ())), precision=prec, preferred_element_type=F32)


def _dot_tn(a, b, prec=None):
    return lax.dot_general(a, b, (((0,), (0,)), ((), ())), precision=prec, preferred_element_type=F32)


def _b(x):
    return x.astype(BF16)


def _sigmoid(x):
    return 1.0 / (1.0 + jnp.exp(-x))


def _silu(x):
    return x * _sigmoid(x)


def _softplus(x):
    return jnp.maximum(x, 0.0) + jnp.log(1.0 + jnp.exp(-jnp.abs(x)))


def _ada_kernel(c_ref, w_ref, b_ref, o_ref):
    s = _silu(c_ref[...])
    o_ref[...] = _dot(_b(s), _b(w_ref[...])) + b_ref[...]


def _ada_mod(cvec, ada_w, ada_b):
    L, D, N = ada_w.shape
    R = cvec.shape[0]
    tn = _pick((1024, 512, 256, 128), N)
    return pl.pallas_call(
        _ada_kernel,
        grid=(L, N // tn),
        in_specs=[pl.BlockSpec((R, D), lambda l, j: (0, 0)),
                  pl.BlockSpec((None, D, tn), lambda l, j: (l, 0, j)),
                  pl.BlockSpec((None, 1, tn), lambda l, j: (l, 0, j))],
        out_specs=pl.BlockSpec((None, R, tn), lambda l, j: (l, 0, j)),
        out_shape=jax.ShapeDtypeStruct((L, R, N), F32),
        compiler_params=_cp("parallel", "parallel"),
        name="ada_mod",
    )(cvec, ada_w, ada_b.reshape(L, 1, N))


def _norm_kernel(x_ref, w_ref, o_ref):
    x = x_ref[...]
    y = x * lax.rsqrt(jnp.mean(x * x, axis=-1, keepdims=True) + EPS) * w_ref[...]
    o_ref[...] = y.astype(o_ref.dtype)


def _norm_mod_kernel(x_ref, w_ref, sc_ref, sh_ref, o_ref):
    x = x_ref[...]
    y = x * lax.rsqrt(jnp.mean(x * x, axis=-1, keepdims=True) + EPS) * w_ref[...]
    o_ref[...] = (y * (1.0 + sc_ref[...]) + sh_ref[...]).astype(o_ref.dtype)


def _norm_router_kernel(x_ref, w_ref, sc_ref, sh_ref, rw_ref, rb_ref, o_ref, lg_ref):
    x = x_ref[...]
    y = x * lax.rsqrt(jnp.mean(x * x, axis=-1, keepdims=True) + EPS) * w_ref[...]
    h = y * (1.0 + sc_ref[...]) + sh_ref[...]
    o_ref[...] = h.astype(o_ref.dtype)
    lg_ref[...] = _dot(h, rw_ref[...], HI) + rb_ref[...]


class _Tok:
    def __init__(self, n_ctx_seq, seq, n_lat_seq, dec_seq):
        self.n_ctx_seq, self.seq, self.n_lat_seq, self.dec_seq = n_ctx_seq, seq, n_lat_seq, dec_seq
        self.t_ctx = n_ctx_seq * seq
        self.t_lat = n_lat_seq * dec_seq
        self.t = self.t_ctx + self.t_lat

    def group_of_tile(self, tm):
        assert self.t_ctx % tm == 0 and self.dec_seq % tm == 0
        nct, tpl = self.t_ctx // tm, self.dec_seq // tm
        return lambda i: jnp.where(i < nct, 0, 1 + (i - nct) // tpl)

    def seq_flags(self, tm):
        assert self.seq % tm == 0 and self.dec_seq % tm == 0
        starts = np.arange(0, self.t, tm)
        first = np.where(starts < self.t_ctx, starts % self.seq == 0, (starts - self.t_ctx) % self.dec_seq == 0)
        ends = starts + tm
        last = np.where(starts < self.t_ctx, ends % self.seq == 0, (ends - self.t_ctx) % self.dec_seq == 0)
        return jnp.asarray(first, jnp.int32), jnp.asarray(last, jnp.int32)


def _norm_mod(x, w, mod3, seg, tok, tm, router=None):
    T, D = x.shape
    grp = tok.group_of_tile(tm)
    in_specs = [pl.BlockSpec((tm, D), lambda i: (i, 0)),
                pl.BlockSpec((1, D), lambda i: (0, 0)),
                pl.BlockSpec((None, 1, D), lambda i: (grp(i), 0, seg + 1)),
                pl.BlockSpec((None, 1, D), lambda i: (grp(i), 0, seg))]
    if router is None:
        return pl.pallas_call(
            _norm_mod_kernel, grid=(T // tm,), in_specs=in_specs,
            out_specs=pl.BlockSpec((tm, D), lambda i: (i, 0)),
            out_shape=jax.ShapeDtypeStruct((T, D), BF16),
            compiler_params=_cp("parallel"), name="norm_mod",
        )(x, w.reshape(1, D), mod3, mod3)
    rw, rb = router
    NR = rw.shape[1]
    return pl.pallas_call(
        _norm_router_kernel, grid=(T // tm,),
        in_specs=in_specs + [pl.BlockSpec((D, NR), lambda i: (0, 0)), pl.BlockSpec((1, NR), lambda i: (0, 0))],
        out_specs=[pl.BlockSpec((tm, D), lambda i: (i, 0)), pl.BlockSpec((tm, NR), lambda i: (i, 0))],
        out_shape=[jax.ShapeDtypeStruct((T, D), BF16), jax.ShapeDtypeStruct((T, NR), F32)],
        compiler_params=_cp("parallel"), name="norm_router",
    )(x, w.reshape(1, D), mod3, mod3, rw, rb)


def _final_norm(x, w, tm):
    T, D = x.shape
    return pl.pallas_call(
        _norm_kernel, grid=(T // tm,),
        in_specs=[pl.BlockSpec((tm, D), lambda i: (i, 0)), pl.BlockSpec((1, D), lambda i: (0, 0))],
        out_specs=pl.BlockSpec((tm, D), lambda i: (i, 0)),
        out_shape=jax.ShapeDtypeStruct((T, D), F32),
        compiler_params=_cp("parallel"), name="final_norm",
    )(x, w.reshape(1, D))


def _mm_kernel(a_ref, b_ref, o_ref):
    o_ref[...] = _dot(a_ref[...], b_ref[...]).astype(o_ref.dtype)


def _matmul(a, b, out_dtype=F32, name="matmul"):
    M, K = a.shape
    N = b.shape[1]
    tm = _pick((512, 256, 128, 64, 8), M)
    tn = _pick((1024, 512, 256, 128), N)
    return pl.pallas_call(
        _mm_kernel, grid=(N // tn, M // tm),
        in_specs=[pl.BlockSpec((tm, K), lambda j, i: (i, 0)), pl.BlockSpec((K, tn), lambda j, i: (0, j))],
        out_specs=pl.BlockSpec((tm, tn), lambda j, i: (i, j)),
        out_shape=jax.ShapeDtypeStruct((M, N), out_dtype),
        compiler_params=_cp("parallel", "parallel"), name=name,
    )(a, b)


def _wout_kernel(x_ref, g_ref, a1_ref, a2_ref, a3_ref, w1_ref, w2_ref, w3_ref, o_ref):
    mix = _dot(a1_ref[...], w1_ref[...]) + _dot(a2_ref[...], w2_ref[...]) + _dot(a3_ref[...], w3_ref[...])
    o_ref[...] = x_ref[...] + g_ref[...] * mix


def _wout_residual(x, mod3, a1, a2, a3, w, tok):
    T, D = x.shape
    k1, k2, k3 = a1.shape[1], a2.shape[1], a3.shape[1]
    assert k1 % k2 == 0 and (k1 + k2) % k3 == 0
    tm = _pick((512, 256, 128), tok.t_ctx, tok.dec_seq)
    tn = _pick((1024, 512, 256, 128), D)
    grp = tok.group_of_tile(tm)
    nj = D // tn
    return pl.pallas_call(
        _wout_kernel, grid=(nj, T // tm),
        in_specs=[pl.BlockSpec((tm, tn), lambda j, i: (i, j)),
                  pl.BlockSpec((None, 1, tn), lambda j, i: (grp(i), 0, 2 * nj + j)),
                  pl.BlockSpec((tm, k1), lambda j, i: (i, 0)),
                  pl.BlockSpec((tm, k2), lambda j, i: (i, 0)),
                  pl.BlockSpec((tm, k3), lambda j, i: (i, 0)),
                  pl.BlockSpec((k1, tn), lambda j, i: (0, j)),
                  pl.BlockSpec((k2, tn), lambda j, i: (k1 // k2, j)),
                  pl.BlockSpec((k3, tn), lambda j, i: ((k1 + k2) // k3, j))],
        out_specs=pl.BlockSpec((tm, tn), lambda j, i: (i, j)),
        out_shape=jax.ShapeDtypeStruct((T, D), F32),
        compiler_params=_cp("parallel", "parallel"), name="wout_residual",
    )(x, mod3, a1, a2, a3, w, w, w)


def _chunk_masks(C, rev):
    ti = lax.broadcasted_iota(jnp.int32, (C, C), 0)
    si = lax.broadcasted_iota(jnp.int32, (C, C), 1)
    incl = (si >= ti) if rev else (si <= ti)
    strict = (si > ti) if rev else (si < ti)
    return ti, si, incl, strict


def _tri_inv(a, ti, si):
    C = a.shape[0]
    eye = (ti == si).astype(F32)

    def same(shift):
        return jnp.right_shift(ti, shift) == jnp.right_shift(si, shift)

    ad = jnp.where(same(3), a, 0.0)
    a2 = _dot(ad, ad, HI)
    a4 = _dot(a2, a2, HI)
    t = _dot(eye - ad, eye + a2, HI)
    t = _dot(t, eye + a4, HI)
    shift = 3
    while (1 << shift) < C:
        aoff = jnp.where(same(shift + 1) & jnp.logical_not(same(shift)), a, 0.0)
        t = t - _dot(_dot(t, aoff, HI), t, HI)
        shift += 1
    return t


def _shift_rows(x, halo_prev_ref, halo_next_ref, first, last, direction):
    tm = x.shape[0]
    rows = lax.broadcasted_iota(jnp.int32, (tm, 1), 0)
    if direction == 0:
        edge = halo_prev_ref[SUB - 1:SUB, :] * (1.0 - first)
        return jnp.where(rows == 0, edge, pltpu.roll(x, 1, 0))
    edge = halo_next_ref[0:1, :] * (1.0 - last)
    return jnp.where(rows == tm - 1, edge, pltpu.roll(x, tm - 1, 0))


def _halo_specs(tm, width, col, n_rows):
    r = tm // SUB
    nb = n_rows // SUB
    prev = pl.BlockSpec((SUB, width), lambda i, f, l: (jnp.maximum(i * r - 1, 0), col))
    nxt = pl.BlockSpec((SUB, width), lambda i, f, l: (jnp.minimum((i + 1) * r, nb - 1), col))
    return prev, nxt


def _gdn_pre_kernel(n_heads, first_ref, last_ref, x_ref, xp_ref, xn_ref, cw_ref, ab_ref, gp_ref,
                    qkv_ref, gates_ref):
    i = pl.program_id(0)
    first = first_ref[i].astype(F32)
    last = last_ref[i].astype(F32)
    x = x_ref[...]
    xm = _shift_rows(x, xp_ref, xn_ref, first, last, 0)
    xq = _shift_rows(x, xp_ref, xn_ref, first, last, 1)
    y = _silu(xm * cw_ref[0:1, :] + x * cw_ref[1:2, :] + xq * cw_ref[2:3, :])
    for hd in range(3 * n_heads):
        seg = y[:, hd * GDN_HEAD:(hd + 1) * GDN_HEAD]
        if hd < 2 * n_heads:
            seg = seg * lax.rsqrt(jnp.sum(seg * seg, axis=-1, keepdims=True) + L2_EPS)
            if hd < n_heads:
                seg = seg * (GDN_HEAD ** -0.5)
        qkv_ref[:, hd * GDN_HEAD:(hd + 1) * GDN_HEAD] = seg
    ab = ab_ref[...]
    lane = lax.broadcasted_iota(jnp.int32, ab.shape, 1)
    g = gp_ref[0:1, :] * _softplus(ab + gp_ref[1:2, :])
    gates_ref[...] = jnp.where(lane < 2 * n_heads, g, _sigmoid(ab))


def _gdn_pre(p_gdn, p_small, conv_w3, gate_par, tok, n_heads, tm):
    T = p_gdn.shape[0]
    W3 = 3 * n_heads * GDN_HEAD
    first, last = tok.seq_flags(tm)
    hp, hn = _halo_specs(tm, W3, 0, T)
    grid_spec = pltpu.PrefetchScalarGridSpec(
        num_scalar_prefetch=2, grid=(T // tm,),
        in_specs=[pl.BlockSpec((tm, W3), lambda i, f, l: (i, 0)), hp, hn,
                  pl.BlockSpec((3, W3), lambda i, f, l: (0, 0)),
                  pl.BlockSpec((tm, LANE), lambda i, f, l: (i, 0)),
                  pl.BlockSpec((SUB, LANE), lambda i, f, l: (0, 0))],
        out_specs=[pl.BlockSpec((tm, W3), lambda i, f, l: (i, 0)),
                   pl.BlockSpec((tm, LANE), lambda i, f, l: (i, 0))])
    return pl.pallas_call(
        functools.partial(_gdn_pre_kernel, n_heads), grid_spec=grid_spec,
        out_shape=[jax.ShapeDtypeStruct((T, W3), F32), jax.ShapeDtypeStruct((T, LANE), F32)],
        compiler_params=_cp("parallel"), name="gdn_pre",
    )(first, last, p_gdn, p_gdn, p_gdn, conv_w3, p_small, gate_par)


def _gdn_chunk_kernel(n_heads, rev, has_s0, emit, *refs):
    refs = list(refs)
    q_ref, k_ref, v_ref, gcol_ref, grow_ref = refs[:5]
    pos = 5
    s0_ref = None
    if has_s0:
        s0_ref = refs[pos]
        pos += 1
    o_ref = refs[pos]
    pos += 1
    so_ref = None
    if emit:
        so_ref = refs[pos]
        pos += 1
    s_scr = refs[pos]

    n = pl.program_id(1)
    n_last = pl.num_programs(1) - 1
    C = q_ref.shape[0]
    H = n_heads
    d = 1 if rev else 0

    @pl.when(n == 0)
    def _():
        if has_s0:
            s_scr[...] = s0_ref[...]
        else:
            s_scr[...] = jnp.zeros(s_scr.shape, F32)

    ti, si, incl, strict = _chunk_masks(C, rev)
    lm = incl.astype(F32)
    ncol = gcol_ref.shape[1]
    er = lax.broadcasted_iota(jnp.int32, (ncol, 2 * H * LANE), 0)
    ec = lax.broadcasted_iota(jnp.int32, (ncol, 2 * H * LANE), 1) // LANE
    want = jnp.where(ec < H, d * H + ec, 2 * H + d * H + (ec - H))
    expand = (er == want).astype(F32)
    gb_all = _dot(gcol_ref[...], expand, HI)
    gc_all = _dot(lm, gb_all[:, :H * LANE], HI)
    grow = grow_ref[d * H:(d + 1) * H, :]
    gc_rows = _dot_nt(grow, lm, HI)
    last_row = 0 if rev else C - 1

    for h in range(H):
        sl = slice(h * GDN_HEAD, (h + 1) * GDN_HEAD)
        q = q_ref[:, sl]
        k = k_ref[:, sl]
        v = v_ref[:, sl]
        gcb = gc_all[:, sl]
        beta = gb_all[:, (H + h) * LANE:(H + h + 1) * LANE]
        gcr = jnp.broadcast_to(gc_rows[h:h + 1, :], (C, C))
        decay = jnp.exp(jnp.where(incl, gcb[:, :C] - gcr, -jnp.inf))
        kb = k * beta
        a_mat = jnp.where(strict, _dot_nt(_b(kb), _b(k)) * decay, 0.0)
        egc = jnp.exp(gcb)
        rhs = jnp.concatenate([v * beta, kb * egc], axis=1)
        t_inv = _tri_inv(a_mat, ti, si)
        sol = _dot(t_inv, rhs, HI)
        u = sol[:, :GDN_HEAD]
        w = sol[:, GDN_HEAD:]
        attn = _dot_nt(_b(q), _b(k)) * decay
        gl = gcb[last_row:last_row + 1, :]
        q_dec = q * egc
        k_dec = k * jnp.exp(gl - gcb)
        s = s_scr[h]
        sb = _b(s)
        v_new = u - _dot(_b(w), sb)
        o = _dot(_b(q_dec), sb) + _dot(_b(attn), _b(v_new))
        o_ref[:, sl] = o
        s_scr[h] = s * jnp.exp(gl) + _dot_tn(_b(k_dec), _b(v_new))

    if emit:
        @pl.when(n == n_last)
        def _():
            so_ref[...] = s_scr[...]


def _gdn_chunk(qkv, gcol, grow, row0, n_seq, seq_len, n_heads, rev, s0=None, s0_idx=None, emit=False):
    T = qkv.shape[0]
    C = CHUNK
    N = seq_len // C
    H = n_heads
    W = H * GDN_HEAD
    b0 = row0 // C

    def blk(s, n):
        nn = (N - 1 - n) if rev else n
        return b0 + s * N + nn

    in_specs = [pl.BlockSpec((C, W), lambda s, n: (blk(s, n), 0)),
                pl.BlockSpec((C, W), lambda s, n: (blk(s, n), 1)),
                pl.BlockSpec((C, W), lambda s, n: (blk(s, n), 2)),
                pl.BlockSpec((None, C, gcol.shape[2]), lambda s, n: (blk(s, n), 0, 0)),
                pl.BlockSpec((None, grow.shape[1], C), lambda s, n: (blk(s, n), 0, 0))]
    args = [qkv, qkv, qkv, gcol, grow]
    if s0 is not None:
        l, d = s0_idx
        in_specs.append(pl.BlockSpec((None, None, None, H, GDN_HEAD, GDN_HEAD), lambda s, n: (s, l, d, 0, 0, 0)))
        args.append(s0)
    out_specs = [pl.BlockSpec((C, W), lambda s, n: (blk(s, n) - b0, 0))]
    out_shape = [jax.ShapeDtypeStruct((n_seq * seq_len, W), F32)]
    if emit:
        out_specs.append(pl.BlockSpec((None, H, GDN_HEAD, GDN_HEAD), lambda s, n: (s, 0, 0, 0)))
        out_shape.append(jax.ShapeDtypeStruct((n_seq, H, GDN_HEAD, GDN_HEAD), F32))
    res = pl.pallas_call(
        functools.partial(_gdn_chunk_kernel, H, rev, s0 is not None, emit),
        grid=(n_seq, N), in_specs=in_specs, out_specs=out_specs, out_shape=out_shape,
        scratch_shapes=[pltpu.VMEM((H, GDN_HEAD, GDN_HEAD), F32)],
        compiler_params=_cp("parallel", "arbitrary"), name="gdn_chunk",
    )(*args)
    return res


def _gdn_out_kernel(n_heads, of_ref, ob_ref, z_ref, nw_ref, o_ref):
    for h in range(n_heads):
        sl = slice(h * GDN_HEAD, (h + 1) * GDN_HEAD)
        o = of_ref[:, sl] + ob_ref[:, sl]
        y = o * lax.rsqrt(jnp.mean(o * o, axis=-1, keepdims=True) + EPS) * nw_ref[...]
        o_ref[:, sl] = (y * _silu(z_ref[:, sl])).astype(o_ref.dtype)


def _gdn_out(o_f, o_b, p_gdn, norm_w, n_heads, tm):
    T, W = o_f.shape
    return pl.pallas_call(
        functools.partial(_gdn_out_kernel, n_heads), grid=(T // tm,),
        in_specs=[pl.BlockSpec((tm, W), lambda i: (i, 0)), pl.BlockSpec((tm, W), lambda i: (i, 0)),
                  pl.BlockSpec((tm, W), lambda i: (i, 3)), pl.BlockSpec((1, GDN_HEAD), lambda i: (0, 0))],
        out_specs=pl.BlockSpec((tm, W), lambda i: (i, 0)),
        out_shape=jax.ShapeDtypeStruct((T, W), BF16),
        compiler_params=_cp("parallel"), name="gdn_out",
    )(o_f, o_b, p_gdn, norm_w.reshape(1, GDN_HEAD))


def _pair_ones():
    r = lax.broadcasted_iota(jnp.int32, (LANE, LANE), 0) // RWKV_HEAD
    c = lax.broadcasted_iota(jnp.int32, (LANE, LANE), 1) // RWKV_HEAD
    return r == c


def _rwkv_pre_kernel(direction, first_ref, last_ref, r_ref, k_ref, v_ref, f_ref,
                     rp_ref, rn_ref, kp_ref, kn_ref, vp_ref, vn_ref, fp_ref, fn_ref,
                     mu_r_ref, mu_k_ref, mu_v_ref, mu_f_ref, w0_ref, wup_ref, a0_ref, aup_ref,
                     kkw_ref, kaw_ref,
                     ro_ref, ko_ref, vo_ref, kko_ref, bo_ref, lwo_ref):
    i = pl.program_id(0)
    first = first_ref[i].astype(F32)
    last = last_ref[i].astype(F32)

    def mix(x_ref, p_ref, n_ref, mu_ref):
        x = x_ref[...]
        return x + (_shift_rows(x, p_ref, n_ref, first, last, direction) - x) * mu_ref[...]

    r = mix(r_ref, rp_ref, rn_ref, mu_r_ref)
    k = mix(k_ref, kp_ref, kn_ref, mu_k_ref)
    v = mix(v_ref, vp_ref, vn_ref, mu_v_ref)
    f = mix(f_ref, fp_ref, fn_ref, mu_f_ref)
    w_lin = w0_ref[...] + _dot(_b(jnp.tanh(f)), _b(wup_ref[...]))
    w_log = -_softplus(-w_lin) - 0.5
    a = _sigmoid(a0_ref[...] + _dot(_b(f), _b(aup_ref[...])))
    kraw = k * kkw_ref[...]
    ones = _pair_ones().astype(F32)
    W = kraw.shape[1]
    for p in range(W // LANE):
        sl = slice(p * LANE, (p + 1) * LANE)
        kr = kraw[:, sl]
        ss = _dot(kr * kr, ones, HI)
        kk = kr * lax.rsqrt(ss + L2_EPS)
        kko_ref[:, sl] = kk
        bo_ref[:, sl] = kk * a[:, sl]
    ro_ref[...] = r
    ko_ref[...] = k * (1.0 + (a - 1.0) * kaw_ref[...])
    vo_ref[...] = v
    lwo_ref[...] = -jnp.exp(w_log)


def _rwkv_pre(p_rwkv, p_small, direction, par, tok, tm):
    T, W3 = p_rwkv.shape
    W = W3 // 3
    first, last = tok.seq_flags(tm)
    main = [pl.BlockSpec((tm, W), lambda i, f, l, c=c: (i, c)) for c in range(3)]
    main.append(pl.BlockSpec((tm, LANE), lambda i, f, l: (i, 1)))
    halos = []
    for c in range(3):
        halos.extend(_halo_specs(tm, W, c, T))
    halos.extend(_halo_specs(tm, LANE, 1, T))
    row = lambda n: pl.BlockSpec((1, n), lambda i, f, l: (0, 0))
    full = lambda a, b: pl.BlockSpec((a, b), lambda i, f, l: (0, 0))
    params = [row(W), row(W), row(W), row(LANE), row(W), full(LANE, W), row(W), full(LANE, W), row(W), row(W)]
    grid_spec = pltpu.PrefetchScalarGridSpec(
        num_scalar_prefetch=2, grid=(T // tm,),
        in_specs=main + halos + params,
        out_specs=[pl.BlockSpec((tm, W), lambda i, f, l: (i, 0)) for _ in range(6)])
    return pl.pallas_call(
        functools.partial(_rwkv_pre_kernel, direction), grid_spec=grid_spec,
        out_shape=[jax.ShapeDtypeStruct((T, W), F32) for _ in range(6)],
        compiler_params=_cp("parallel"), name="rwkv_pre",
    )(first, last, p_rwkv, p_rwkv, p_rwkv, p_small,
      p_rwkv, p_rwkv, p_rwkv, p_rwkv, p_rwkv, p_rwkv, p_small, p_small, *par)


def _rwkv_chunk_kernel(n_pairs, rev, has_s0, emit, *refs):
    refs = list(refs)
    r_ref, k_ref, v_ref, kk_ref, b_ref, lw_ref, lnw_ref, lnb_ref, rk_ref = refs[:9]
    pos = 9
    s0_ref = None
    if has_s0:
        s0_ref = refs[pos]
        pos += 1
    y_ref = refs[pos]
    pos += 1
    so_ref = None
    if emit:
        so_ref = refs[pos]
        pos += 1
    s_scr = refs[pos]

    n = pl.program_id(1)
    n_last = pl.num_programs(1) - 1
    C = r_ref.shape[0]

    @pl.when(n == 0)
    def _():
        if has_s0:
            s_scr[...] = s0_ref[...]
        else:
            s_scr[...] = jnp.zeros(s_scr.shape, F32)

    ti, si, incl, strict = _chunk_masks(C, rev)
    lm = incl.astype(F32)
    bd = _pair_ones()
    ones = bd.astype(F32)
    m0 = lax.broadcasted_iota(jnp.int32, (1, LANE), 1) < RWKV_HEAD
    last_row = 0 if rev else C - 1
    inv_n = 1.0 / RWKV_HEAD

    for p in range(n_pairs):
        sl = slice(p * LANE, (p + 1) * LANE)
        r = r_ref[:, sl]
        k = k_ref[:, sl]
        v = v_ref[:, sl]
        kk = kk_ref[:, sl]
        b = b_ref[:, sl]
        lw = lw_ref[:, sl]
        cl = _dot(lm, lw, HI)
        cll = cl[last_row:last_row + 1, :]
        e_neg = jnp.exp(-cl)
        rt = r * jnp.exp(cl)
        kt = k * e_neg
        bt = b * e_neg
        at = -kk * jnp.exp(cl - lw)
        e_end = jnp.exp(cll - cl)
        k_end = k * e_end
        b_end = b * e_end
        s = s_scr[p]
        sb = _b(s)
        xa = _dot_nt(_b(at), sb)
        xr = _dot_nt(_b(rt), sb)
        ktb, btb, vb = _b(kt), _b(bt), _b(v)
        t_inv, a_ak, r_k, r_b = [], [], [], []
        for h2 in range(2):
            mh = m0 if h2 == 0 else jnp.logical_not(m0)
            atm = _b(jnp.where(mh, at, 0.0))
            rtm = _b(jnp.where(mh, rt, 0.0))
            a_ab = jnp.where(strict, _dot_nt(atm, btb), 0.0)
            a_ak.append(jnp.where(strict, _dot_nt(atm, ktb), 0.0))
            r_k.append(jnp.where(incl, _dot_nt(rtm, ktb), 0.0))
            r_b.append(jnp.where(incl, _dot_nt(rtm, btb), 0.0))
            t_inv.append(_tri_inv(-a_ab, ti, si))
        z = jnp.where(m0, _dot(_b(a_ak[0]), vb), _dot(_b(a_ak[1]), vb))
        wm = xa + z
        u = jnp.where(m0, _dot(t_inv[0], wm, HI), _dot(t_inv[1], wm, HI))
        ub = _b(u)
        y = xr + jnp.where(m0, _dot(_b(r_k[0]), vb) + _dot(_b(r_b[0]), ub),
                           _dot(_b(r_k[1]), vb) + _dot(_b(r_b[1]), ub))
        upd = _dot_tn(vb, _b(k_end)) + _dot_tn(ub, _b(b_end))
        s_scr[p] = s * jnp.exp(cll) + jnp.where(bd, upd, 0.0)
        mu = _dot(y, ones, HI) * inv_n
        yc = y - mu
        var = _dot(yc * yc, ones, HI) * inv_n
        yn = yc * lax.rsqrt(var + RWKV_GN_EPS) * lnw_ref[:, sl] + lnb_ref[:, sl]
        bonus = _dot(r * k * rk_ref[:, sl], ones, HI)
        y_ref[:, sl] = yn + bonus * v

    if emit:
        @pl.when(n == n_last)
        def _():
            so_ref[...] = s_scr[...]


def _rwkv_chunk(pre, ln, row0, n_seq, seq_len, rev, s0=None, emit=False):
    r = pre[0]
    W = r.shape[1]
    P = W // LANE
    C = CHUNK
    N = seq_len // C
    b0 = row0 // C

    def blk(s, n):
        nn = (N - 1 - n) if rev else n
        return b0 + s * N + nn

    in_specs = [pl.BlockSpec((C, W), lambda s, n: (blk(s, n), 0)) for _ in range(6)]
    in_specs += [pl.BlockSpec((1, W), lambda s, n: (0, 0)) for _ in range(3)]
    args = list(pre) + list(ln)
    if s0 is not None:
        in_specs.append(pl.BlockSpec((None, P, LANE, LANE), lambda s, n: (s, 0, 0, 0)))
        args.append(s0)
    out_specs = [pl.BlockSpec((C, W), lambda s, n: (blk(s, n) - b0, 0))]
    out_shape = [jax.ShapeDtypeStruct((n_seq * seq_len, W), F32)]
    if emit:
        out_specs.append(pl.BlockSpec((None, P, LANE, LANE), lambda s, n: (s, 0, 0, 0)))
        out_shape.append(jax.ShapeDtypeStruct((n_seq, P, LANE, LANE), F32))
    return pl.pallas_call(
        functools.partial(_rwkv_chunk_kernel, P, rev, s0 is not None, emit),
        grid=(n_seq, N), in_specs=in_specs, out_specs=out_specs, out_shape=out_shape,
        scratch_shapes=[pltpu.VMEM((P, LANE, LANE), F32)],
        compiler_params=_cp("parallel", "arbitrary"), name="rwkv_chunk",
    )(*args)


def _rwkv_out_kernel(yf_ref, yb_ref, fg_ref, gup_ref, o_ref):
    g = _dot(_b(_sigmoid(fg_ref[...])), _b(gup_ref[...]))
    o_ref[...] = ((yf_ref[...] + yb_ref[...]) * g).astype(o_ref.dtype)


def _rwkv_out(y_f, y_b, p_small, g_up, tm):
    T, W = y_f.shape
    return pl.pallas_call(
        _rwkv_out_kernel, grid=(T // tm,),
        in_specs=[pl.BlockSpec((tm, W), lambda i: (i, 0)), pl.BlockSpec((tm, W), lambda i: (i, 0)),
                  pl.BlockSpec((tm, LANE), lambda i: (i, 2)), pl.BlockSpec((LANE, W), lambda i: (0, 0))],
        out_specs=pl.BlockSpec((tm, W), lambda i: (i, 0)),
        out_shape=jax.ShapeDtypeStruct((T, W), BF16),
        compiler_params=_cp("parallel"), name="rwkv_out",
    )(y_f, y_b, p_small, g_up)


def _rope128(x, cos, sin):
    return x * cos + pltpu.roll(x, MLA_ROPE, 1) * sin


def _mla_pre_kernel(cq_ref, ckv_ref, kr_ref, cos_ref, sin_ref, qw_ref, kvw_ref, cqn_ref, ckvn_ref, krf_ref):
    cq = cq_ref[...]
    cqn_ref[...] = (cq * lax.rsqrt(jnp.mean(cq * cq, axis=-1, keepdims=True) + EPS) * qw_ref[...]).astype(BF16)
    ckv = ckv_ref[...]
    ckvn_ref[...] = ckv * lax.rsqrt(jnp.mean(ckv * ckv, axis=-1, keepdims=True) + EPS) * kvw_ref[...]
    krf_ref[...] = _rope128(kr_ref[...], cos_ref[...], sin_ref[...]).astype(BF16)


def _mla_pre(p_cq, p_ckv, p_small, cos, sin, qw, kvw, tm):
    T, QR = p_cq.shape
    KR = p_ckv.shape[1]
    return pl.pallas_call(
        _mla_pre_kernel, grid=(T // tm,),
        in_specs=[pl.BlockSpec((tm, QR), lambda i: (i, 0)), pl.BlockSpec((tm, KR), lambda i: (i, 0)),
                  pl.BlockSpec((tm, LANE), lambda i: (i, 3)),
                  pl.BlockSpec((tm, LANE), lambda i: (i, 0)), pl.BlockSpec((tm, LANE), lambda i: (i, 0)),
                  pl.BlockSpec((1, QR), lambda i: (0, 0)), pl.BlockSpec((1, KR), lambda i: (0, 0))],
        out_specs=[pl.BlockSpec((tm, QR), lambda i: (i, 0)), pl.BlockSpec((tm, KR), lambda i: (i, 0)),
                   pl.BlockSpec((tm, LANE), lambda i: (i, 0))],
        out_shape=[jax.ShapeDtypeStruct((T, QR), BF16), jax.ShapeDtypeStruct((T, KR), F32),
                   jax.ShapeDtypeStruct((T, LANE), BF16)],
        compiler_params=_cp("parallel"), name="mla_pre",
    )(p_cq, p_ckv, p_small, cos, sin, qw.reshape(1, QR), kvw.reshape(1, KR))


def _attn_kernel(nk, tk, scale, qn_ref, qr_ref, cos_ref, sin_ref, kn_ref, kr_ref, v_ref, o_ref):
    qr = _rope128(qr_ref[...], cos_ref[...], sin_ref[...])
    q = _b(jnp.concatenate([qn_ref[...], qr], axis=1) * scale)
    tq = q.shape[0]

    def body(j, carry):
        m, l, acc = carry
        off = pl.multiple_of(j * tk, tk)
        kb = jnp.concatenate([kn_ref[pl.ds(off, tk), :], kr_ref[pl.ds(off, tk), :]], axis=1)
        s = _dot_nt(q, kb)
        m_new = jnp.maximum(m, jnp.max(s, axis=-1, keepdims=True))
        alpha = jnp.exp(m - m_new)
        p = jnp.exp(s - m_new)
        l = alpha * l + jnp.sum(p, axis=-1, keepdims=True)
        acc = alpha * acc + _dot(_b(p), v_ref[pl.ds(off, tk), :])
        return m_new, l, acc

    init = (jnp.full((tq, 1), -jnp.inf, F32), jnp.zeros((tq, 1), F32), jnp.zeros((tq, MLA_V), F32))
    m, l, acc = lax.fori_loop(0, nk, body, init)
    o_ref[...] = (acc / l).astype(o_ref.dtype)


def _attention(q, cos, sin, kv, kr, q_row0, k_row0, n_seq, tq_len, tk_len, n_heads):
    H = n_heads
    tq = _pick((256, 128), tq_len)
    tk = _pick((256, 128), tk_len)
    assert q_row0 % tq == 0 and k_row0 % tk_len == 0
    nq = tq_len // tq
    qb0 = q_row0 // tq
    kb0 = k_row0 // tk_len
    scale = MLA_QK ** -0.5
    return pl.pallas_call(
        functools.partial(_attn_kernel, tk_len // tk, tk, scale),
        grid=(n_seq, H, nq),
        in_specs=[pl.BlockSpec((tq, LANE), lambda s, h, i: (qb0 + s * nq + i, h)),
                  pl.BlockSpec((tq, LANE), lambda s, h, i: (qb0 + s * nq + i, H + h)),
                  pl.BlockSpec((tq, LANE), lambda s, h, i: (qb0 + s * nq + i, 0)),
                  pl.BlockSpec((tq, LANE), lambda s, h, i: (qb0 + s * nq + i, 0)),
                  pl.BlockSpec((tk_len, LANE), lambda s, h, i: (kb0 + s, 2 * h)),
                  pl.BlockSpec((tk_len, LANE), lambda s, h, i: (kb0 + s, 0)),
                  pl.BlockSpec((tk_len, LANE), lambda s, h, i: (kb0 + s, 2 * h + 1))],
        out_specs=pl.BlockSpec((tq, LANE), lambda s, h, i: (s * nq + i, h)),
        out_shape=jax.ShapeDtypeStruct((n_seq * tq_len, H * MLA_V), BF16),
        compiler_params=_cp("parallel", "parallel", "arbitrary"), name="mla_attention",
    )(q, q, cos, sin, kv, kr, kv)


def _gather_kernel(rows, idx_ref, nused_ref, src_ref, o_ref, sem):
    i = pl.program_id(0)

    @pl.when(i < nused_ref[0])
    def _():
        def start(r, c):
            t = idx_ref[i * rows + r]
            pltpu.make_async_copy(src_ref.at[pl.ds(t, 1), :], o_ref.at[pl.ds(r, 1), :], sem.at[0]).start()
            return c

        lax.fori_loop(0, rows, start, 0)

        def wait(r, c):
            pltpu.make_async_copy(src_ref.at[pl.ds(0, 1), :], o_ref.at[pl.ds(r, 1), :], sem.at[0]).wait()
            return c

        lax.fori_loop(0, rows, wait, 0)

    @pl.when(i >= nused_ref[0])
    def _():
        o_ref[...] = jnp.zeros(o_ref.shape, o_ref.dtype)


def _gather_rows(src, idx, n_used, rows):
    n, width = idx.shape[0], src.shape[1]
    grid_spec = pltpu.PrefetchScalarGridSpec(
        num_scalar_prefetch=2, grid=(n // rows,),
        in_specs=[pl.BlockSpec(memory_space=pl.ANY)],
        out_specs=pl.BlockSpec((rows, width), lambda i, idx, nu: (i, 0)),
        scratch_shapes=[pltpu.SemaphoreType.DMA((1,))])
    return pl.pallas_call(
        functools.partial(_gather_kernel, rows), grid_spec=grid_spec,
        out_shape=jax.ShapeDtypeStruct((n, width), src.dtype),
        compiler_params=_cp("arbitrary"), name="moe_gather",
    )(idx, n_used, src)


def _expert_kernel(be_ref, nused_ref, x_ref, wg_ref, wu_ref, wd_ref, gate_ref, o_ref):
    blk = pl.program_id(0)
    part = pl.program_id(1)
    used = blk < nused_ref[0]

    @pl.when(used)
    def _():
        x = x_ref[...]
        g = _dot(x, _b(wg_ref[...]))
        u = _dot(x, _b(wu_ref[...]))
        hid = _silu(g) * u
        y = _dot(_b(hid), _b(wd_ref[...])) * gate_ref[...]

        @pl.when(part == 0)
        def _():
            o_ref[...] = y

        @pl.when(part != 0)
        def _():
            o_ref[...] += y

    @pl.when(jnp.logical_and(jnp.logical_not(used), part == 0))
    def _():
        o_ref[...] = jnp.zeros(o_ref.shape, F32)


def _experts(xs, slot_gate, block_e, n_used, w_gate, w_up, w_down, layer, tm):
    n_slots, D = xs.shape
    DE = w_gate.shape[-1]
    parts = 2 if DE % 512 == 0 else 1
    dp = DE // parts
    grid_spec = pltpu.PrefetchScalarGridSpec(
        num_scalar_prefetch=2, grid=(n_slots // tm, parts),
        in_specs=[pl.BlockSpec((tm, D), lambda i, p, be, nu: (i, 0)),
                  pl.BlockSpec((None, None, D, dp), lambda i, p, be, nu: (layer, be[i], 0, p)),
                  pl.BlockSpec((None, None, D, dp), lambda i, p, be, nu: (layer, be[i], 0, p)),
                  pl.BlockSpec((None, None, dp, D), lambda i, p, be, nu: (layer, be[i], p, 0)),
                  pl.BlockSpec((tm, 1), lambda i, p, be, nu: (i, 0))],
        out_specs=pl.BlockSpec((tm, D), lambda i, p, be, nu: (i, 0)))
    return pl.pallas_call(
        _expert_kernel, grid_spec=grid_spec,
        out_shape=jax.ShapeDtypeStruct((n_slots, D), F32),
        compiler_params=_cp("parallel", "arbitrary", vmem=58 * 1024 * 1024), name="moe_experts",
    )(block_e, n_used, xs, w_gate, w_up, w_down, slot_gate.reshape(n_slots, 1))


def _combine_kernel(rows, p0_ref, p1_ref, x_ref, g_ref, ys_ref, o_ref, a_scr, b_scr, sem):
    i = pl.program_id(0)

    def start(r, c):
        pltpu.make_async_copy(ys_ref.at[pl.ds(p0_ref[i * rows + r], 1), :], a_scr.at[pl.ds(r, 1), :], sem.at[0]).start()
        pltpu.make_async_copy(ys_ref.at[pl.ds(p1_ref[i * rows + r], 1), :], b_scr.at[pl.ds(r, 1), :], sem.at[1]).start()
        return c

    lax.fori_loop(0, rows, start, 0)

    def wait(r, c):
        pltpu.make_async_copy(ys_ref.at[pl.ds(0, 1), :], a_scr.at[pl.ds(r, 1), :], sem.at[0]).wait()
        pltpu.make_async_copy(ys_ref.at[pl.ds(0, 1), :], b_scr.at[pl.ds(r, 1), :], sem.at[1]).wait()
        return c

    lax.fori_loop(0, rows, wait, 0)
    o_ref[...] = x_ref[...] + g_ref[...] * (a_scr[...] + b_scr[...])


def _combine(x, mod3, ys, p0, p1, tok, rows):
    T, D = x.shape
    grp = tok.group_of_tile(rows)
    grid_spec = pltpu.PrefetchScalarGridSpec(
        num_scalar_prefetch=2, grid=(T // rows,),
        in_specs=[pl.BlockSpec((rows, D), lambda i, a, b: (i, 0)),
                  pl.BlockSpec((None, 1, D), lambda i, a, b: (grp(i), 0, 5)),
                  pl.BlockSpec(memory_space=pl.ANY)],
        out_specs=pl.BlockSpec((rows, D), lambda i, a, b: (i, 0)),
        scratch_shapes=[pltpu.VMEM((rows, D), F32), pltpu.VMEM((rows, D), F32), pltpu.SemaphoreType.DMA((2,))])
    return pl.pallas_call(
        functools.partial(_combine_kernel, rows), grid_spec=grid_spec,
        out_shape=jax.ShapeDtypeStruct((T, D), F32),
        compiler_params=_cp("arbitrary"), name="moe_combine",
    )(p0, p1, x, mod3, ys)


def _route(logits, n_groups, n_experts, tm):
    n_tok = logits.shape[0]
    epg = n_experts // n_groups
    gl = logits[:, :n_groups]
    pg = jax.nn.softmax(gl, axis=-1)
    grp = jnp.argmax(gl, axis=-1)
    pg_sel = jnp.max(pg, axis=-1, keepdims=True)
    el = logits[:, n_groups:n_groups + n_experts].reshape(n_tok, n_groups, epg)
    el = jnp.take_along_axis(el, grp[:, None, None], axis=1)[:, 0]
    top_p, top_i = lax.top_k(jax.nn.softmax(el, axis=-1), TOP_K)
    gate = pg_sel * top_p / jnp.sum(top_p, axis=-1, keepdims=True)
    expert = grp[:, None].astype(jnp.int32) * epg + top_i.astype(jnp.int32)
    flat_e = expert.reshape(-1)
    n_assign = n_tok * TOP_K
    onehot = (flat_e[:, None] == jnp.arange(n_experts, dtype=jnp.int32)[None, :]).astype(jnp.int32)
    csum = jnp.cumsum(onehot, axis=0)
    rank = jnp.take_along_axis(csum, flat_e[:, None], axis=1)[:, 0] - 1
    counts = csum[-1]
    padded = (counts + tm - 1) // tm * tm
    pend = jnp.cumsum(padded)
    pstart = pend - padded
    dest = (pstart[flat_e] + rank).astype(jnp.int32)
    n_blocks = n_assign // tm + n_experts
    n_slots = n_blocks * tm
    slot_tok = jnp.zeros((n_slots,), jnp.int32).at[dest].set(jnp.arange(n_assign, dtype=jnp.int32) // TOP_K)
    slot_gate = jnp.zeros((n_slots,), F32).at[dest].set(gate.reshape(-1))
    block_e = jnp.minimum(jnp.searchsorted(pend, jnp.arange(n_blocks, dtype=jnp.int32) * tm, side='right'),
                          n_experts - 1).astype(jnp.int32)
    n_used = (pend[-1] // tm).astype(jnp.int32).reshape(1)
    dest2 = dest.reshape(n_tok, TOP_K)
    return slot_tok, slot_gate, block_e, n_used, dest2[:, 0], dest2[:, 1]


def _rope_rot_cols(w):
    q = MLA_ROPE // 4
    return jnp.concatenate([-w[..., q:2 * q], w[..., 0:q], -w[..., 3 * q:4 * q], w[..., 2 * q:3 * q]], axis=-1)


def _pad_cols(w, n):
    return jnp.pad(w, ((0, 0), (0, n - w.shape[1])))


def _rope_tables(tok):
    half = MLA_ROPE // 2
    nfreq = half // 2
    inv = ROPE_THETA ** (-jnp.arange(nfreq, dtype=F32) / nfreq)
    t = jnp.arange(tok.dec_seq, dtype=jnp.int32)
    ang_r = (t // GRID_W).astype(F32)[:, None] * inv
    ang_c = (t % GRID_W).astype(F32)[:, None] * inv
    z = jnp.zeros((tok.dec_seq, LANE - MLA_ROPE), F32)
    cos = jnp.concatenate([jnp.cos(ang_r), jnp.cos(ang_r), jnp.cos(ang_c), jnp.cos(ang_c), z], axis=1)
    sin = jnp.concatenate([jnp.sin(ang_r), jnp.sin(ang_r), jnp.sin(ang_c), jnp.sin(ang_c), z], axis=1)
    ctx_cos = jnp.concatenate([jnp.ones((tok.t_ctx, MLA_ROPE), F32), jnp.zeros((tok.t_ctx, LANE - MLA_ROPE), F32)], axis=1)
    cos = jnp.concatenate([ctx_cos] + [cos] * tok.n_lat_seq, axis=0)
    sin = jnp.concatenate([jnp.zeros((tok.t_ctx, LANE), F32)] + [sin] * tok.n_lat_seq, axis=0)
    return cos, sin


def kernel(x_prompt, x_sample, state_gdn, state_rwkv, cache_mla_ckv, cache_mla_krope, c, c_ctx, ada_w, ada_b, norm_mix_w, norm_ffn_w, w_in, gdn_conv_w, gdn_A_log, gdn_dt_bias, gdn_norm_w, rwkv_mu, rwkv_w0, rwkv_w_up, rwkv_a0, rwkv_a_up, rwkv_g_up, rwkv_k_k, rwkv_k_a, rwkv_r_k, rwkv_ln_w, rwkv_ln_b, mla_q_norm_w, mla_q_up, mla_kv_norm_w, mla_kv_up, w_out, router_group_w, router_group_b, router_expert_w, router_expert_b, moe_w_gate, moe_w_up, moe_w_down, final_norm_w):
    B, SEQ, D = x_prompt.shape
    DB, DSEQ, _ = x_sample.shape
    L = w_in.shape[0]
    PAST = cache_mla_ckv.shape[2]
    GH = gdn_A_log.shape[-1]
    GW = GH * GDN_HEAD
    RH = rwkv_r_k.shape[1]
    RW = RH * RWKV_HEAD
    RP = RW // LANE
    DL, AL, GL = rwkv_w_up.shape[2], rwkv_a_up.shape[2], rwkv_g_up.shape[1]
    QR = mla_q_norm_w.shape[1]
    KR = mla_kv_norm_w.shape[1]
    MH = mla_kv_up.shape[2] // (MLA_NOPE + MLA_V)
    NG = router_group_w.shape[2]
    NE = router_expert_w.shape[2]
    assert DL + AL == LANE and GL == LANE and 4 * GH <= LANE and NG + NE <= LANE and 1 + DB <= SUB
    tok = _Tok(B, SEQ, DB, DSEQ)
    T = tok.t
    tm = _pick((256, 128, 64), SEQ, DSEQ)
    tm_moe = 256

    gdn_cols = 4 * GW + 4 * GH
    rwkv_cols = 3 * RW + DL + AL + GL
    o_r = gdn_cols
    o_m = gdn_cols + rwkv_cols

    x = jnp.concatenate([x_prompt.reshape(B * SEQ, D), x_sample.reshape(DB * DSEQ, D)], axis=0)
    cvec = jnp.zeros((SUB, D), F32).at[0].set(c_ctx).at[1:1 + DB].set(c)
    mod = _ada_mod(cvec, ada_w, ada_b)
    cos, sin = _rope_tables(tok)

    new_gdn, new_rwkv, new_ckv, new_kr = [], [], [], []
    for l in range(L):
        mod3 = mod[l].reshape(SUB, 1, 6 * D)
        wl = w_in[l]
        w_gdn = _b(wl[:, :4 * GW])
        w_rwkv = _b(wl[:, o_r:o_r + 3 * RW])
        w_cq = _b(wl[:, o_m:o_m + QR])
        w_ckv = _b(wl[:, o_m + QR:o_m + QR + KR])
        w_kr = wl[:, o_m + QR + KR:o_m + QR + KR + MLA_ROPE]
        w_small = _b(jnp.concatenate([
            _pad_cols(wl[:, 4 * GW:4 * GW + 4 * GH], LANE),
            wl[:, o_r + 3 * RW:o_r + 3 * RW + DL + AL],
            wl[:, o_r + 3 * RW + DL + AL:o_r + rwkv_cols],
            w_kr, _rope_rot_cols(w_kr)], axis=1))

        h1 = _norm_mod(x, norm_mix_w[l], mod3, 0, tok, tm)
        p_gdn = _matmul(h1, w_gdn, name="proj_gdn")
        p_rwkv = _matmul(h1, w_rwkv, name="proj_rwkv")
        p_cq = _matmul(h1, w_cq, name="proj_cq")
        p_ckv = _matmul(h1, w_ckv, name="proj_ckv")
        p_small = _matmul(h1, w_small, name="proj_small")

        gate_par = jnp.zeros((SUB, LANE), F32)
        gate_par = gate_par.at[0, :2 * GH].set(-jnp.exp(gdn_A_log[l].reshape(-1)))
        gate_par = gate_par.at[1, :2 * GH].set(gdn_dt_bias[l].reshape(-1))
        qkv, gates = _gdn_pre(p_gdn, p_small, gdn_conv_w[l].T, gate_par, tok, GH, tm)
        gcol = gates[:, :4 * GH].reshape(T // CHUNK, CHUNK, 4 * GH)
        grow = jnp.swapaxes(gates[:, :2 * GH].reshape(T // CHUNK, CHUNK, 2 * GH), 1, 2)
        oc_f, sc_f = _gdn_chunk(qkv, gcol, grow, 0, B, SEQ, GH, False, emit=True)
        oc_b, sc_b = _gdn_chunk(qkv, gcol, grow, 0, B, SEQ, GH, True, emit=True)
        ol_f, = _gdn_chunk(qkv, gcol, grow, tok.t_ctx, DB, DSEQ, GH, False, s0=state_gdn, s0_idx=(l, 0))
        ol_b, = _gdn_chunk(qkv, gcol, grow, tok.t_ctx, DB, DSEQ, GH, True, s0=state_gdn, s0_idx=(l, 1))
        o_gdn = _gdn_out(jnp.concatenate([oc_f, ol_f], axis=0), jnp.concatenate([oc_b, ol_b], axis=0),
                         p_gdn, gdn_norm_w[l], GH, tm)
        new_gdn.append(jnp.stack([sc_f, sc_b], axis=1))

        ln = (rwkv_ln_w[l].reshape(1, RW), rwkv_ln_b[l].reshape(1, RW), rwkv_r_k[l].reshape(1, RW))
        y_dir, s_dir = [], []
        for d in range(2):
            mu = rwkv_mu[l, d]
            wup = jnp.concatenate([rwkv_w_up[l, d], jnp.zeros((AL, RW), F32)], axis=0)
            aup = jnp.concatenate([jnp.zeros((DL, RW), F32), rwkv_a_up[l, d]], axis=0)
            par = (mu[:RW].reshape(1, RW), mu[RW:2 * RW].reshape(1, RW), mu[2 * RW:3 * RW].reshape(1, RW),
                   mu[3 * RW:].reshape(1, LANE), rwkv_w0[l, d].reshape(1, RW), wup,
                   rwkv_a0[l, d].reshape(1, RW), aup, rwkv_k_k[l].reshape(1, RW), rwkv_k_a[l].reshape(1, RW))
            pre = _rwkv_pre(p_rwkv, p_small, d, par, tok, tm)
            s0 = state_rwkv[:, l, d].reshape(DB, RP, 2, RWKV_HEAD, RWKV_HEAD)
            s0 = jnp.einsum('bpaij,ac->bpaicj', s0, jnp.eye(2, dtype=F32)).reshape(DB, RP, LANE, LANE)
            yc, sc = _rwkv_chunk(pre, ln, 0, B, SEQ, d == 1, emit=True)
            yl, = _rwkv_chunk(pre, ln, tok.t_ctx, DB, DSEQ, d == 1, s0=s0)
            y_dir.append(jnp.concatenate([yc, yl], axis=0))
            sc = sc.reshape(B, RP, 2, RWKV_HEAD, 2, RWKV_HEAD)
            s_dir.append(jnp.einsum('bpaiaj->bpaij', sc).reshape(B, RH, RWKV_HEAD, RWKV_HEAD))
        o_rwkv = _rwkv_out(y_dir[0], y_dir[1], p_small, rwkv_g_up[l], tm)
        new_rwkv.append(jnp.stack(s_dir, axis=1))

        cqn, ckvn, krf = _mla_pre(p_cq, p_ckv, p_small, cos, sin, mla_q_norm_w[l], mla_kv_norm_w[l], tm)
        qu = mla_q_up[l].reshape(QR, MH, MLA_QK)
        q_rope = qu[:, :, MLA_NOPE:]
        w_q = _b(jnp.concatenate([qu[:, :, :MLA_NOPE].reshape(QR, MH * MLA_NOPE),
                                  jnp.concatenate([q_rope, _rope_rot_cols(q_rope)], axis=-1).reshape(QR, MH * LANE)],
                                 axis=1))
        q = _matmul(cqn, w_q, name="q_up")
        kr_cache = _b(jnp.pad(cache_mla_krope[:, l], ((0, 0), (0, 0), (0, LANE - MLA_ROPE))))
        ckv_parts, kr_parts = [], []
        for b in range(DB):
            r0 = tok.t_ctx + b * DSEQ
            ckv_parts += [_b(cache_mla_ckv[b, l]), _b(ckvn[r0:r0 + DSEQ])]
            kr_parts += [kr_cache[b], krf[r0:r0 + DSEQ]]
        ckv_all = jnp.concatenate(ckv_parts + [_b(ckvn[:tok.t_ctx])], axis=0)
        kr_all = jnp.concatenate(kr_parts + [krf[:tok.t_ctx]], axis=0)
        kv = _matmul(ckv_all, _b(mla_kv_up[l]), out_dtype=BF16, name="kv_up")
        o_c = _attention(q, cos, sin, kv, kr_all, 0, DB * (PAST + DSEQ), B, SEQ, SEQ, MH)
        o_l = _attention(q, cos, sin, kv, kr_all, tok.t_ctx, 0, DB, DSEQ, PAST + DSEQ, MH)
        o_mla = jnp.concatenate([o_c, o_l], axis=0)
        new_ckv.append(ckvn[:tok.t_ctx].reshape(B, SEQ, KR))
        new_kr.append(p_small[:tok.t_ctx, 3 * LANE:3 * LANE + MLA_ROPE].reshape(B, SEQ, MLA_ROPE))

        x = _wout_residual(x, mod3, o_gdn, o_rwkv, o_mla, _b(w_out[l]), tok)

        rw = _pad_cols(jnp.concatenate([router_group_w[l], router_expert_w[l]], axis=1), LANE)
        rb = _pad_cols(jnp.concatenate([router_group_b[l], router_expert_b[l]])[None, :], LANE)
        h2, logits = _norm_mod(x, norm_ffn_w[l], mod3, 3, tok, tm, router=(rw, rb))
        slot_tok, slot_gate, block_e, n_used, p0, p1 = _route(logits, NG, NE, tm_moe)
        h2w = lax.bitcast_convert_type(h2.reshape(T, D // 2, 2), jnp.uint32)
        xs = _gather_rows(h2w, slot_tok, n_used, tm_moe)
        xs = lax.bitcast_convert_type(xs, BF16).reshape(xs.shape[0], D)
        ys = _experts(xs, slot_gate, block_e, n_used, moe_w_gate, moe_w_up, moe_w_down, l, tm_moe)
        x = _combine(x, mod3, ys, p0, p1, tok, tm)

    y = _final_norm(x, final_norm_w, tm)
    y_prompt = y[:tok.t_ctx].reshape(B, SEQ, D)
    y_sample = y[tok.t_ctx:].reshape(DB, DSEQ, D)
    return (y_prompt, y_sample, jnp.stack(new_gdn, axis=1), jnp.stack(new_rwkv, axis=1),
            jnp.stack(new_ckv, axis=1), jnp.stack(new_kr, axis=1))
```

```python
import functools

import numpy as np
import jax
import jax.numpy as jnp
from jax import lax
from jax.experimental import pallas as pl
from jax.experimental.pallas import tpu as pltpu

F32 = jnp.float32
BF16 = jnp.bfloat16
HI = lax.Precision.HIGHEST

EPS = 1e-6
L2_EPS = 1e-6
GRID_W = 64
ROPE_THETA = 10000.0
GDN_HEAD = 128
RWKV_HEAD = 64
RWKV_GN_EPS = RWKV_HEAD * 1e-5
MLA_V = 128
MLA_NOPE = 128
MLA_ROPE = 64
MLA_QK = MLA_NOPE + MLA_ROPE
TOP_K = 2
CHUNK = 64
LANE = 128
SUB = 8
DMA_UNROLL = 8
VMEM_LIMIT = 52 * 1024 * 1024


def _cp(*sem, vmem=VMEM_LIMIT):
    return pltpu.CompilerParams(dimension_semantics=sem, vmem_limit_bytes=vmem)


def _pick(prefs, *ns):
    for p in prefs:
        if all(n % p == 0 for n in ns):
            return p
    raise ValueError(f"no tile in {prefs} divides {ns}")


def _dot(a, b, prec=None):
    return lax.dot_general(a, b, (((1,), (0,)), ((), ())), precision=prec, preferred_element_type=F32)


def _dot_nt(a, b, prec=None):
    return lax.dot_general(a, b, (((1,), (1,)), ((), ())), precision=prec, preferred_element_type=F32)


def _dot_tn(a, b, prec=None):
    return lax.dot_general(a, b, (((0,), (0,)), ((), ())), precision=prec, preferred_element_type=F32)


def _b(x):
    return x.astype(BF16)


def _sigmoid(x):
    return 1.0 / (1.0 + jnp.exp(-x))


def _silu(x):
    return x * _sigmoid(x)


def _softplus(x):
    return jnp.maximum(x, 0.0) + jnp.log(1.0 + jnp.exp(-jnp.abs(x)))


def _ada_kernel(c_ref, w_ref, b_ref, o_ref):
    s = _silu(c_ref[...])
    o_ref[...] = _dot(_b(s), _b(w_ref[...])) + b_ref[...]


def _ada_mod(cvec, ada_w, ada_b):
    L, D, N = ada_w.shape
    R = cvec.shape[0]
    tn = _pick((1024, 512, 256, 128), N)
    return pl.pallas_call(
        _ada_kernel,
        grid=(L, N // tn),
        in_specs=[pl.BlockSpec((R, D), lambda l, j: (0, 0)),
                  pl.BlockSpec((None, D, tn), lambda l, j: (l, 0, j)),
                  pl.BlockSpec((None, 1, tn), lambda l, j: (l, 0, j))],
        out_specs=pl.BlockSpec((None, R, tn), lambda l, j: (l, 0, j)),
        out_shape=jax.ShapeDtypeStruct((L, R, N), F32),
        compiler_params=_cp("parallel", "parallel"),
        name="ada_mod",
    )(cvec, ada_w, ada_b.reshape(L, 1, N))


def _norm_kernel(x_ref, w_ref, o_ref):
    x = x_ref[...]
    y = x * lax.rsqrt(jnp.mean(x * x, axis=-1, keepdims=True) + EPS) * w_ref[...]
    o_ref[...] = y.astype(o_ref.dtype)


def _norm_mod_kernel(x_ref, w_ref, sc_ref, sh_ref, o_ref):
    x = x_ref[...]
    y = x * lax.rsqrt(jnp.mean(x * x, axis=-1, keepdims=True) + EPS) * w_ref[...]
    o_ref[...] = (y * (1.0 + sc_ref[...]) + sh_ref[...]).astype(o_ref.dtype)


def _pack_bf16_halves(h):
    half = h.shape[1] // 2
    a = lax.bitcast_convert_type(h[:, :half].astype(BF16).astype(F32), jnp.uint32)
    b = lax.bitcast_convert_type(h[:, half:].astype(BF16).astype(F32), jnp.uint32)
    return a | (b >> 16)


def _unpack_bf16_halves(w):
    a = lax.bitcast_convert_type(w & jnp.uint32(0xFFFF0000), F32).astype(BF16)
    b = lax.bitcast_convert_type(w << 16, F32).astype(BF16)
    return a, b


def _norm_router_kernel(x_ref, w_ref, sc_ref, sh_ref, rw_ref, rb_ref, o_ref, lg_ref):
    x = x_ref[...]
    y = x * lax.rsqrt(jnp.mean(x * x, axis=-1, keepdims=True) + EPS) * w_ref[...]
    h = y * (1.0 + sc_ref[...]) + sh_ref[...]
    o_ref[...] = _pack_bf16_halves(h)
    lg_ref[...] = _dot(h, rw_ref[...], HI) + rb_ref[...]


class _Tok:
    def __init__(self, n_ctx_seq, seq, n_lat_seq, dec_seq):
        self.n_ctx_seq, self.seq, self.n_lat_seq, self.dec_seq = n_ctx_seq, seq, n_lat_seq, dec_seq
        self.t_ctx = n_ctx_seq * seq
        self.t_lat = n_lat_seq * dec_seq
        self.t = self.t_ctx + self.t_lat

    def group_of_tile(self, tm):
        assert self.t_ctx % tm == 0 and self.dec_seq % tm == 0
        nct, tpl = self.t_ctx // tm, self.dec_seq // tm
        return lambda i: jnp.where(i < nct, 0, 1 + (i - nct) // tpl)

    def seq_flags(self, tm):
        assert self.seq % tm == 0 and self.dec_seq % tm == 0
        starts = np.arange(0, self.t, tm)
        first = np.where(starts < self.t_ctx, starts % self.seq == 0, (starts - self.t_ctx) % self.dec_seq == 0)
        ends = starts + tm
        last = np.where(starts < self.t_ctx, ends % self.seq == 0, (ends - self.t_ctx) % self.dec_seq == 0)
        return jnp.asarray(first, jnp.int32), jnp.asarray(last, jnp.int32)


def _norm_mod(x, w, mod3, seg, tok, tm, router=None):
    T, D = x.shape
    grp = tok.group_of_tile(tm)
    in_specs = [pl.BlockSpec((tm, D), lambda i: (i, 0)),
                pl.BlockSpec((1, D), lambda i: (0, 0)),
                pl.BlockSpec((None, 1, D), lambda i: (grp(i), 0, seg + 1)),
                pl.BlockSpec((None, 1, D), lambda i: (grp(i), 0, seg))]
    if router is None:
        return pl.pallas_call(
            _norm_mod_kernel, grid=(T // tm,), in_specs=in_specs,
            out_specs=pl.BlockSpec((tm, D), lambda i: (i, 0)),
            out_shape=jax.ShapeDtypeStruct((T, D), BF16),
            compiler_params=_cp("parallel"), name="norm_mod",
        )(x, w.reshape(1, D), mod3, mod3)
    rw, rb = router
    NR = rw.shape[1]
    return pl.pallas_call(
        _norm_router_kernel, grid=(T // tm,),
        in_specs=in_specs + [pl.BlockSpec((D, NR), lambda i: (0, 0)), pl.BlockSpec((1, NR), lambda i: (0, 0))],
        out_specs=[pl.BlockSpec((tm, D // 2), lambda i: (i, 0)), pl.BlockSpec((tm, NR), lambda i: (i, 0))],
        out_shape=[jax.ShapeDtypeStruct((T, D // 2), jnp.uint32), jax.ShapeDtypeStruct((T, NR), F32)],
        compiler_params=_cp("parallel"), name="norm_router",
    )(x, w.reshape(1, D), mod3, mod3, rw, rb)


def _final_norm(x, w, tm):
    T, D = x.shape
    return pl.pallas_call(
        _norm_kernel, grid=(T // tm,),
        in_specs=[pl.BlockSpec((tm, D), lambda i: (i, 0)), pl.BlockSpec((1, D), lambda i: (0, 0))],
        out_specs=pl.BlockSpec((tm, D), lambda i: (i, 0)),
        out_shape=jax.ShapeDtypeStruct((T, D), F32),
        compiler_params=_cp("parallel"), name="final_norm",
    )(x, w.reshape(1, D))


def _mm_kernel(a_ref, b_ref, o_ref):
    o_ref[...] = _dot(a_ref[...], b_ref[...]).astype(o_ref.dtype)


def _matmul(a, b, out_dtype=F32, name="matmul"):
    M, K = a.shape
    N = b.shape[1]
    tm = _pick((512, 256, 128, 64, 8), M)
    tn = _pick((1024, 512, 256, 128), N)
    return pl.pallas_call(
        _mm_kernel, grid=(N // tn, M // tm),
        in_specs=[pl.BlockSpec((tm, K), lambda j, i: (i, 0)), pl.BlockSpec((K, tn), lambda j, i: (0, j))],
        out_specs=pl.BlockSpec((tm, tn), lambda j, i: (i, j)),
        out_shape=jax.ShapeDtypeStruct((M, N), out_dtype),
        compiler_params=_cp("parallel", "parallel"), name=name,
    )(a, b)


def _wout_kernel(x_ref, g_ref, a1_ref, a2_ref, a3_ref, w1_ref, w2_ref, w3_ref, o_ref):
    mix = _dot(a1_ref[...], w1_ref[...]) + _dot(a2_ref[...], w2_ref[...]) + _dot(a3_ref[...], w3_ref[...])
    o_ref[...] = x_ref[...] + g_ref[...] * mix


def _wout_residual(x, mod3, a1, a2, a3, w, tok):
    T, D = x.shape
    k1, k2, k3 = a1.shape[1], a2.shape[1], a3.shape[1]
    assert k1 % k2 == 0 and (k1 + k2) % k3 == 0
    tm = _pick((512, 256, 128), tok.t_ctx, tok.dec_seq)
    tn = _pick((1024, 512, 256, 128), D)
    grp = tok.group_of_tile(tm)
    nj = D // tn
    return pl.pallas_call(
        _wout_kernel, grid=(nj, T // tm),
        in_specs=[pl.BlockSpec((tm, tn), lambda j, i: (i, j)),
                  pl.BlockSpec((None, 1, tn), lambda j, i: (grp(i), 0, 2 * nj + j)),
                  pl.BlockSpec((tm, k1), lambda j, i: (i, 0)),
                  pl.BlockSpec((tm, k2), lambda j, i: (i, 0)),
                  pl.BlockSpec((tm, k3), lambda j, i: (i, 0)),
                  pl.BlockSpec((k1, tn), lambda j, i: (0, j)),
                  pl.BlockSpec((k2, tn), lambda j, i: (k1 // k2, j)),
                  pl.BlockSpec((k3, tn), lambda j, i: ((k1 + k2) // k3, j))],
        out_specs=pl.BlockSpec((tm, tn), lambda j, i: (i, j)),
        out_shape=jax.ShapeDtypeStruct((T, D), F32),
        compiler_params=_cp("parallel", "parallel"), name="wout_residual",
    )(x, mod3, a1, a2, a3, w, w, w)


def _split_bf16(x, terms):
    out = []
    for _ in range(terms):
        p = x.astype(BF16)
        out.append(p)
        x = x - p.astype(F32)
    return out


def _dot_exact_r(a01, x, terms=3):
    return sum(_dot(a01, p) for p in _split_bf16(x, terms))


def _dot_exact_l(x, b01, terms=3):
    return sum(_dot(p, b01) for p in _split_bf16(x, terms))


def _dot_nt_exact_l(x, b01, terms=3):
    return sum(_dot_nt(p, b01) for p in _split_bf16(x, terms))


def _one_bf16(mask):
    return jnp.where(mask, 1.0, 0.0).astype(BF16)


def _pair_masks(C, rev):
    M = 2 * C
    ti = lax.broadcasted_iota(jnp.int32, (M, M), 0)
    si = lax.broadcasted_iota(jnp.int32, (M, M), 1)
    shift = C.bit_length() - 1
    assert (1 << shift) == C
    same = jnp.right_shift(ti, shift) == jnp.right_shift(si, shift)
    before = (si > ti) if rev else (si < ti)
    strict = same & before
    incl = same & (before | (ti == si))
    tc = lax.broadcasted_iota(jnp.int32, (C, C), 0)
    sc = lax.broadcasted_iota(jnp.int32, (C, C), 1)
    cum = _one_bf16((sc >= tc) if rev else (sc <= tc))
    return ti, si, shift, same, strict, incl, cum


def _tri_inv_n(a_list, ti, si, top_shift):
    def same(shift):
        return jnp.right_shift(ti, shift) == jnp.right_shift(si, shift)

    n_mat = range(len(a_list))
    m3 = same(3)
    ad = [jnp.where(m3, a, 0.0) for a in a_list]
    adb = [_b(x) for x in ad]
    b2 = [_dot(x, x) for x in adb]
    b2b = [_b(x) for x in b2]
    n1 = [b2[i] - ad[i] - _dot(adb[i], b2b[i]) for i in n_mat]
    b4 = [_dot(x, x) for x in b2b]
    n = [n1[i] + b4[i] + _dot(_b(n1[i]), _b(b4[i])) for i in n_mat]
    for shift in range(3, top_shift):
        moff = same(shift + 1) & jnp.logical_not(same(shift))
        aoff = [jnp.where(moff, a, 0.0) for a in a_list]
        aob = [_b(x) for x in aoff]
        nb = [_b(x) for x in n]
        x = [aoff[i] + _dot(nb[i], aob[i]) for i in n_mat]
        n = [n[i] - x[i] - _dot(_b(x[i]), nb[i]) for i in n_mat]
    return n


def _shift_rows(x, halo_prev_ref, halo_next_ref, first, last, direction):
    tm = x.shape[0]
    rows = lax.broadcasted_iota(jnp.int32, (tm, 1), 0)
    if direction == 0:
        edge = halo_prev_ref[SUB - 1:SUB, :] * (1.0 - first)
        return jnp.where(rows == 0, edge, pltpu.roll(x, 1, 0))
    edge = halo_next_ref[0:1, :] * (1.0 - last)
    return jnp.where(rows == tm - 1, edge, pltpu.roll(x, tm - 1, 0))


def _halo_specs(tm, width, col, n_rows):
    r = tm // SUB
    nb = n_rows // SUB
    prev = pl.BlockSpec((SUB, width), lambda i, f, l: (jnp.maximum(i * r - 1, 0), col))
    nxt = pl.BlockSpec((SUB, width), lambda i, f, l: (jnp.minimum((i + 1) * r, nb - 1), col))
    return prev, nxt


def _gdn_pre_kernel(n_heads, first_ref, last_ref, x_ref, xp_ref, xn_ref, cw_ref, ab_ref, gp_ref,
                    qkv_ref, gates_ref):
    i = pl.program_id(0)
    first = first_ref[i].astype(F32)
    last = last_ref[i].astype(F32)
    x = x_ref[...]
    xm = _shift_rows(x, xp_ref, xn_ref, first, last, 0)
    xq = _shift_rows(x, xp_ref, xn_ref, first, last, 1)
    y = _silu(xm * cw_ref[0:1, :] + x * cw_ref[1:2, :] + xq * cw_ref[2:3, :])
    for hd in range(3 * n_heads):
        seg = y[:, hd * GDN_HEAD:(hd + 1) * GDN_HEAD]
        if hd < 2 * n_heads:
            seg = seg * lax.rsqrt(jnp.sum(seg * seg, axis=-1, keepdims=True) + L2_EPS)
            if hd < n_heads:
                seg = seg * (GDN_HEAD ** -0.5)
        qkv_ref[:, hd * GDN_HEAD:(hd + 1) * GDN_HEAD] = seg
    ab = ab_ref[...]
    lane = lax.broadcasted_iota(jnp.int32, ab.shape, 1)
    g = gp_ref[0:1, :] * _softplus(ab + gp_ref[1:2, :])
    gates_ref[...] = jnp.where(lane < 2 * n_heads, g, _sigmoid(ab))


def _gdn_pre(p_gdn, p_small, conv_w3, gate_par, tok, n_heads, tm):
    T = p_gdn.shape[0]
    W3 = 3 * n_heads * GDN_HEAD
    first, last = tok.seq_flags(tm)
    hp, hn = _halo_specs(tm, W3, 0, T)
    grid_spec = pltpu.PrefetchScalarGridSpec(
        num_scalar_prefetch=2, grid=(T // tm,),
        in_specs=[pl.BlockSpec((tm, W3), lambda i, f, l: (i, 0)), hp, hn,
                  pl.BlockSpec((3, W3), lambda i, f, l: (0, 0)),
                  pl.BlockSpec((tm, LANE), lambda i, f, l: (i, 0)),
                  pl.BlockSpec((SUB, LANE), lambda i, f, l: (0, 0))],
        out_specs=[pl.BlockSpec((tm, W3), lambda i, f, l: (i, 0)),
                   pl.BlockSpec((tm, LANE), lambda i, f, l: (i, 0))])
    return pl.pallas_call(
        functools.partial(_gdn_pre_kernel, n_heads), grid_spec=grid_spec,
        out_shape=[jax.ShapeDtypeStruct((T, W3), F32), jax.ShapeDtypeStruct((T, LANE), F32)],
        compiler_params=_cp("parallel"), name="gdn_pre",
    )(first, last, p_gdn, p_gdn, p_gdn, conv_w3, p_small, gate_par)


def _gdn_chunk_kernel(n_heads, rev, has_s0, emit, *refs):
    refs = list(refs)
    q_ref, k_ref, v_ref, gcol_ref, grow_ref = refs[:5]
    pos = 5
    s0_ref = None
    if has_s0:
        s0_ref = refs[pos]
        pos += 1
    o_ref = refs[pos]
    pos += 1
    so_ref = None
    if emit:
        so_ref = refs[pos]
        pos += 1
    s_scr = refs[pos]

    n = pl.program_id(1)
    n_last = pl.num_programs(1) - 1
    C = q_ref.shape[0]
    H = n_heads
    d = 1 if rev else 0

    @pl.when(n == 0)
    def _():
        if has_s0:
            s_scr[...] = s0_ref[...]
        else:
            s_scr[...] = jnp.zeros(s_scr.shape, F32)

    M = 2 * C
    HP = H // 2
    assert H % 2 == 0 and M == LANE
    ti, si, shift, same, strict, incl, cum = _pair_masks(C, rev)
    incl_b = _one_bf16(incl)
    ncol = gcol_ref.shape[1]
    er = lax.broadcasted_iota(jnp.int32, (ncol, 2 * H * LANE), 0)
    ec = lax.broadcasted_iota(jnp.int32, (ncol, 2 * H * LANE), 1) // LANE
    want = jnp.where(ec < H, d * H + ec, 2 * H + d * H + (ec - H))
    gb_all = _dot_exact_l(gcol_ref[...], _one_bf16(er == want))
    gcb_all = _dot_exact_r(cum, gb_all[:, :H * LANE])
    beta_all = gb_all[:, H * LANE:]
    gc_rows = _dot_nt_exact_l(grow_ref[d * HP:(d + 1) * HP, :], incl_b)
    last_row = 0 if rev else C - 1

    q = q_ref[...]
    k = k_ref[...]
    v = v_ref[...]
    egc_all = jnp.exp(gcb_all)
    gl_all = gcb_all[last_row:last_row + 1, :]
    kb_all = k * beta_all
    vbeta_all = v * beta_all
    kbe_all = kb_all * egc_all
    qd_all = q * egc_all
    kd_all = k * jnp.exp(gl_all - gcb_all)
    sdec_all = jnp.exp(gl_all)

    def sl(h):
        return slice(h * GDN_HEAD, (h + 1) * GDN_HEAD)

    def pair_rows(x, p):
        return jnp.concatenate([x[:, sl(2 * p)], x[:, sl(2 * p + 1)]], axis=0)

    pairs = range(HP)
    heads = range(H)
    k_st = [_b(pair_rows(k, p)) for p in pairs]
    lhs = [_b(jnp.concatenate([pair_rows(kb_all, p), pair_rows(q, p)], axis=0)) for p in pairs]
    sc = [_dot_nt(lhs[p], k_st[p]) for p in pairs]
    decay = [jnp.exp(jnp.where(incl, pair_rows(gcb_all, p) - jnp.broadcast_to(gc_rows[p:p + 1, :], (M, M)),
                               -jnp.inf)) for p in pairs]
    a_mat = [jnp.where(strict, sc[p][:M] * decay[p], 0.0) for p in pairs]
    attn_b = [_b(sc[p][M:] * decay[p]) for p in pairs]
    n_inv = _tri_inv_n(a_mat, ti, si, shift)
    rhs = [jnp.concatenate([pair_rows(vbeta_all, p), pair_rows(kbe_all, p)], axis=1) for p in pairs]
    sol = [rhs[p] + _dot(_b(n_inv[p]), _b(rhs[p])) for p in pairs]
    s_old = [s_scr[h] for h in heads]
    s_b = [_b(s) for s in s_old]
    ws = [_dot(_b(jnp.concatenate([sol[h // 2][(h % 2) * C:(h % 2 + 1) * C, GDN_HEAD:], qd_all[:, sl(h)]], axis=0)),
               s_b[h]) for h in heads]
    v_new = [sol[p][:, :GDN_HEAD] - jnp.concatenate([ws[2 * p][:C], ws[2 * p + 1][:C]], axis=0) for p in pairs]
    vn_b = [_b(x) for x in v_new]
    o = [jnp.concatenate([ws[2 * p][C:], ws[2 * p + 1][C:]], axis=0) + _dot(attn_b[p], vn_b[p]) for p in pairs]
    for p in pairs:
        o_ref[:, sl(2 * p)] = o[p][:C]
        o_ref[:, sl(2 * p + 1)] = o[p][C:]
    for h in heads:
        upd = _dot_tn(_b(kd_all[:, sl(h)]), vn_b[h // 2][(h % 2) * C:(h % 2 + 1) * C])
        s_scr[h] = s_old[h] * sdec_all[:, sl(h)] + upd

    if emit:
        @pl.when(n == n_last)
        def _():
            so_ref[...] = s_scr[...]


def _gdn_chunk(qkv, gcol, grow, row0, n_seq, seq_len, n_heads, rev, s0=None, s0_idx=None, emit=False):
    T = qkv.shape[0]
    C = CHUNK
    N = seq_len // C
    H = n_heads
    W = H * GDN_HEAD
    b0 = row0 // C

    def blk(s, n):
        nn = (N - 1 - n) if rev else n
        return b0 + s * N + nn

    in_specs = [pl.BlockSpec((C, W), lambda s, n: (blk(s, n), 0)),
                pl.BlockSpec((C, W), lambda s, n: (blk(s, n), 1)),
                pl.BlockSpec((C, W), lambda s, n: (blk(s, n), 2)),
                pl.BlockSpec((None, C, gcol.shape[2]), lambda s, n: (blk(s, n), 0, 0)),
                pl.BlockSpec((None, grow.shape[1], grow.shape[2]), lambda s, n: (blk(s, n), 0, 0))]
    args = [qkv, qkv, qkv, gcol, grow]
    if s0 is not None:
        l, d = s0_idx
        in_specs.append(pl.BlockSpec((None, None, None, H, GDN_HEAD, GDN_HEAD), lambda s, n: (s, l, d, 0, 0, 0)))
        args.append(s0)
    out_specs = [pl.BlockSpec((C, W), lambda s, n: (blk(s, n) - b0, 0))]
    out_shape = [jax.ShapeDtypeStruct((n_seq * seq_len, W), F32)]
    if emit:
        out_specs.append(pl.BlockSpec((None, H, GDN_HEAD, GDN_HEAD), lambda s, n: (s, 0, 0, 0)))
        out_shape.append(jax.ShapeDtypeStruct((n_seq, H, GDN_HEAD, GDN_HEAD), F32))
    res = pl.pallas_call(
        functools.partial(_gdn_chunk_kernel, H, rev, s0 is not None, emit),
        grid=(n_seq, N), in_specs=in_specs, out_specs=out_specs, out_shape=out_shape,
        scratch_shapes=[pltpu.VMEM((H, GDN_HEAD, GDN_HEAD), F32)],
        compiler_params=_cp("parallel", "arbitrary"), name="gdn_chunk",
    )(*args)
    return res


def _gdn_out_kernel(n_heads, of_ref, ob_ref, z_ref, nw_ref, o_ref):
    for h in range(n_heads):
        sl = slice(h * GDN_HEAD, (h + 1) * GDN_HEAD)
        o = of_ref[:, sl] + ob_ref[:, sl]
        y = o * lax.rsqrt(jnp.mean(o * o, axis=-1, keepdims=True) + EPS) * nw_ref[...]
        o_ref[:, sl] = (y * _silu(z_ref[:, sl])).astype(o_ref.dtype)


def _gdn_out(o_f, o_b, p_gdn, norm_w, n_heads, tm):
    T, W = o_f.shape
    return pl.pallas_call(
        functools.partial(_gdn_out_kernel, n_heads), grid=(T // tm,),
        in_specs=[pl.BlockSpec((tm, W), lambda i: (i, 0)), pl.BlockSpec((tm, W), lambda i: (i, 0)),
                  pl.BlockSpec((tm, W), lambda i: (i, 3)), pl.BlockSpec((1, GDN_HEAD), lambda i: (0, 0))],
        out_specs=pl.BlockSpec((tm, W), lambda i: (i, 0)),
        out_shape=jax.ShapeDtypeStruct((T, W), BF16),
        compiler_params=_cp("parallel"), name="gdn_out",
    )(o_f, o_b, p_gdn, norm_w.reshape(1, GDN_HEAD))


def _pair_ones():
    r = lax.broadcasted_iota(jnp.int32, (LANE, LANE), 0) // RWKV_HEAD
    c = lax.broadcasted_iota(jnp.int32, (LANE, LANE), 1) // RWKV_HEAD
    return r == c


def _rwkv_pre_kernel(direction, first_ref, last_ref, r_ref, k_ref, v_ref, f_ref,
                     rp_ref, rn_ref, kp_ref, kn_ref, vp_ref, vn_ref, fp_ref, fn_ref,
                     mu_r_ref, mu_k_ref, mu_v_ref, mu_f_ref, w0_ref, wup_ref, a0_ref, aup_ref,
                     kkw_ref, kaw_ref,
                     ro_ref, ko_ref, vo_ref, kko_ref, bo_ref, lwo_ref):
    i = pl.program_id(0)
    first = first_ref[i].astype(F32)
    last = last_ref[i].astype(F32)

    def mix(x_ref, p_ref, n_ref, mu_ref):
        x = x_ref[...]
        return x + (_shift_rows(x, p_ref, n_ref, first, last, direction) - x) * mu_ref[...]

    r = mix(r_ref, rp_ref, rn_ref, mu_r_ref)
    k = mix(k_ref, kp_ref, kn_ref, mu_k_ref)
    v = mix(v_ref, vp_ref, vn_ref, mu_v_ref)
    f = mix(f_ref, fp_ref, fn_ref, mu_f_ref)
    w_lin = w0_ref[...] + _dot(_b(jnp.tanh(f)), _b(wup_ref[...]))
    w_log = -_softplus(-w_lin) - 0.5
    a = _sigmoid(a0_ref[...] + _dot(_b(f), _b(aup_ref[...])))
    kraw = k * kkw_ref[...]
    ones = _pair_ones().astype(F32)
    W = kraw.shape[1]
    for p in range(W // LANE):
        sl = slice(p * LANE, (p + 1) * LANE)
        kr = kraw[:, sl]
        ss = _dot(kr * kr, ones, HI)
        kk = kr * lax.rsqrt(ss + L2_EPS)
        kko_ref[:, sl] = kk
        bo_ref[:, sl] = kk * a[:, sl]
    ro_ref[...] = r
    ko_ref[...] = k * (1.0 + (a - 1.0) * kaw_ref[...])
    vo_ref[...] = v
    lwo_ref[...] = -jnp.exp(w_log)


def _rwkv_pre(p_rwkv, p_small, direction, par, tok, tm):
    T, W3 = p_rwkv.shape
    W = W3 // 3
    first, last = tok.seq_flags(tm)
    main = [pl.BlockSpec((tm, W), lambda i, f, l, c=c: (i, c)) for c in range(3)]
    main.append(pl.BlockSpec((tm, LANE), lambda i, f, l: (i, 1)))
    halos = []
    for c in range(3):
        halos.extend(_halo_specs(tm, W, c, T))
    halos.extend(_halo_specs(tm, LANE, 1, T))
    row = lambda n: pl.BlockSpec((1, n), lambda i, f, l: (0, 0))
    full = lambda a, b: pl.BlockSpec((a, b), lambda i, f, l: (0, 0))
    params = [row(W), row(W), row(W), row(LANE), row(W), full(LANE, W), row(W), full(LANE, W), row(W), row(W)]
    grid_spec = pltpu.PrefetchScalarGridSpec(
        num_scalar_prefetch=2, grid=(T // tm,),
        in_specs=main + halos + params,
        out_specs=[pl.BlockSpec((tm, W), lambda i, f, l: (i, 0)) for _ in range(6)])
    return pl.pallas_call(
        functools.partial(_rwkv_pre_kernel, direction), grid_spec=grid_spec,
        out_shape=[jax.ShapeDtypeStruct((T, W), F32) for _ in range(6)],
        compiler_params=_cp("parallel"), name="rwkv_pre",
    )(first, last, p_rwkv, p_rwkv, p_rwkv, p_small,
      p_rwkv, p_rwkv, p_rwkv, p_rwkv, p_rwkv, p_rwkv, p_small, p_small, *par)


def _rwkv_chunk_kernel(n_pairs, rev, has_s0, emit, *refs):
    refs = list(refs)
    r_ref, k_ref, v_ref, kk_ref, b_ref, lw_ref, lnw_ref, lnb_ref, rk_ref = refs[:9]
    pos = 9
    s0_ref = None
    if has_s0:
        s0_ref = refs[pos]
        pos += 1
    y_ref = refs[pos]
    pos += 1
    so_ref = None
    if emit:
        so_ref = refs[pos]
        pos += 1
    s_scr = refs[pos]

    n = pl.program_id(1)
    n_last = pl.num_programs(1) - 1
    C = r_ref.shape[0]

    @pl.when(n == 0)
    def _():
        if has_s0:
            s_scr[...] = s0_ref[...]
        else:
            s_scr[...] = jnp.zeros(s_scr.shape, F32)

    M = 2 * C
    assert M == LANE and RWKV_HEAD == C
    ti, si, shift, same, strict, incl, cum = _pair_masks(C, rev)
    ones_b = _one_bf16(same)
    last_row = 0 if rev else C - 1
    inv_n = 1.0 / RWKV_HEAD
    pairs = range(n_pairs)
    sls = [slice(p * LANE, (p + 1) * LANE) for p in pairs]

    def st2(x):
        return jnp.concatenate([x, x], axis=0)

    def fold(x):
        return x[:C] + x[C:]

    def seg_sum(x):
        return _dot_exact_l(x, ones_b, 2)

    r = r_ref[...]
    k = k_ref[...]
    v = v_ref[...]
    lw = lw_ref[...]
    cl = _dot_exact_r(cum, lw)
    cll = cl[last_row:last_row + 1, :]
    e_neg = jnp.exp(-cl)
    rtb = _b(r * jnp.exp(cl))
    ktb = _b(k * e_neg)
    btb = _b(b_ref[...] * e_neg)
    atb = _b(-kk_ref[...] * jnp.exp(cl - lw))
    e_end = jnp.exp(cll - cl)
    keb = _b(k * e_end)
    beb = _b(b_ref[...] * e_end)
    vb = _b(v)
    s_dec = jnp.exp(cll)

    s_old = [s_scr[p] for p in pairs]
    s_b = [_b(s) for s in s_old]
    x0 = [_dot_nt(jnp.concatenate([atb[:, sl], rtb[:, sl]], axis=0), s_b[p]) for p, sl in enumerate(sls)]
    la = [jnp.concatenate([st2(atb[:, sl]) * ones_b, st2(rtb[:, sl]) * ones_b], axis=0) for sl in sls]
    sk = [_dot_nt(la[p], st2(ktb[:, sl])) for p, sl in enumerate(sls)]
    sb = [_dot_nt(la[p], st2(btb[:, sl])) for p, sl in enumerate(sls)]
    a_ak = [_b(jnp.where(strict, x[:M], 0.0)) for x in sk]
    r_k = [_b(jnp.where(incl, x[M:], 0.0)) for x in sk]
    a_neg = [jnp.where(strict, -x[:M], 0.0) for x in sb]
    r_b = [_b(jnp.where(incl, x[M:], 0.0)) for x in sb]
    n_inv = _tri_inv_n(a_neg, ti, si, shift)
    v_st = [st2(vb[:, sl]) * ones_b for sl in sls]
    wm = [x0[p][:C] + fold(_dot(a_ak[p], v_st[p])) for p in pairs]
    u = [wm[p] + fold(_dot(_b(n_inv[p]), st2(_b(wm[p])) * ones_b)) for p in pairs]
    u_b = [_b(x) for x in u]
    u_st = [st2(x) * ones_b for x in u_b]
    y = [x0[p][C:] + fold(_dot(jnp.concatenate([r_k[p], r_b[p]], axis=1),
                               jnp.concatenate([v_st[p], u_st[p]], axis=0))) for p in pairs]
    upd = [_dot_tn(jnp.concatenate([vb[:, sl], u_b[p]], axis=0), jnp.concatenate([keb[:, sl], beb[:, sl]], axis=0))
           for p, sl in enumerate(sls)]
    for p, sl in enumerate(sls):
        s_scr[p] = s_old[p] * s_dec[:, sl] + jnp.where(same, upd[p], 0.0)
    for p, sl in enumerate(sls):
        mu = seg_sum(y[p]) * inv_n
        yc = y[p] - mu
        var = seg_sum(yc * yc) * inv_n
        yn = yc * lax.rsqrt(var + RWKV_GN_EPS) * lnw_ref[:, sl] + lnb_ref[:, sl]
        bonus = seg_sum(r[:, sl] * k[:, sl] * rk_ref[:, sl])
        y_ref[:, sl] = yn + bonus * v[:, sl]

    if emit:
        @pl.when(n == n_last)
        def _():
            so_ref[...] = s_scr[...]


def _rwkv_chunk(pre, ln, row0, n_seq, seq_len, rev, s0=None, emit=False):
    r = pre[0]
    W = r.shape[1]
    P = W // LANE
    C = CHUNK
    N = seq_len // C
    b0 = row0 // C

    def blk(s, n):
        nn = (N - 1 - n) if rev else n
        return b0 + s * N + nn

    in_specs = [pl.BlockSpec((C, W), lambda s, n: (blk(s, n), 0)) for _ in range(6)]
    in_specs += [pl.BlockSpec((1, W), lambda s, n: (0, 0)) for _ in range(3)]
    args = list(pre) + list(ln)
    if s0 is not None:
        in_specs.append(pl.BlockSpec((None, P, LANE, LANE), lambda s, n: (s, 0, 0, 0)))
        args.append(s0)
    out_specs = [pl.BlockSpec((C, W), lambda s, n: (blk(s, n) - b0, 0))]
    out_shape = [jax.ShapeDtypeStruct((n_seq * seq_len, W), F32)]
    if emit:
        out_specs.append(pl.BlockSpec((None, P, LANE, LANE), lambda s, n: (s, 0, 0, 0)))
        out_shape.append(jax.ShapeDtypeStruct((n_seq, P, LANE, LANE), F32))
    return pl.pallas_call(
        functools.partial(_rwkv_chunk_kernel, P, rev, s0 is not None, emit),
        grid=(n_seq, N), in_specs=in_specs, out_specs=out_specs, out_shape=out_shape,
        scratch_shapes=[pltpu.VMEM((P, LANE, LANE), F32)],
        compiler_params=_cp("parallel", "arbitrary"), name="rwkv_chunk",
    )(*args)


def _rwkv_out_kernel(yf_ref, yb_ref, fg_ref, gup_ref, o_ref):
    g = _dot(_b(_sigmoid(fg_ref[...])), _b(gup_ref[...]))
    o_ref[...] = ((yf_ref[...] + yb_ref[...]) * g).astype(o_ref.dtype)


def _rwkv_out(y_f, y_b, p_small, g_up, tm):
    T, W = y_f.shape
    return pl.pallas_call(
        _rwkv_out_kernel, grid=(T // tm,),
        in_specs=[pl.BlockSpec((tm, W), lambda i: (i, 0)), pl.BlockSpec((tm, W), lambda i: (i, 0)),
                  pl.BlockSpec((tm, LANE), lambda i: (i, 2)), pl.BlockSpec((LANE, W), lambda i: (0, 0))],
        out_specs=pl.BlockSpec((tm, W), lambda i: (i, 0)),
        out_shape=jax.ShapeDtypeStruct((T, W), BF16),
        compiler_params=_cp("parallel"), name="rwkv_out",
    )(y_f, y_b, p_small, g_up)


def _rope128(x, cos, sin):
    return x * cos + pltpu.roll(x, MLA_ROPE, 1) * sin


def _mla_pre_kernel(cq_ref, ckv_ref, kr_ref, cos_ref, sin_ref, qw_ref, kvw_ref, cqn_ref, ckvn_ref, krf_ref):
    cq = cq_ref[...]
    cqn_ref[...] = (cq * lax.rsqrt(jnp.mean(cq * cq, axis=-1, keepdims=True) + EPS) * qw_ref[...]).astype(BF16)
    ckv = ckv_ref[...]
    ckvn_ref[...] = ckv * lax.rsqrt(jnp.mean(ckv * ckv, axis=-1, keepdims=True) + EPS) * kvw_ref[...]
    krf_ref[...] = _rope128(kr_ref[...], cos_ref[...], sin_ref[...]).astype(BF16)


def _mla_pre(p_cq, p_ckv, p_small, cos, sin, qw, kvw, tm):
    T, QR = p_cq.shape
    KR = p_ckv.shape[1]
    return pl.pallas_call(
        _mla_pre_kernel, grid=(T // tm,),
        in_specs=[pl.BlockSpec((tm, QR), lambda i: (i, 0)), pl.BlockSpec((tm, KR), lambda i: (i, 0)),
                  pl.BlockSpec((tm, LANE), lambda i: (i, 3)),
                  pl.BlockSpec((tm, LANE), lambda i: (i, 0)), pl.BlockSpec((tm, LANE), lambda i: (i, 0)),
                  pl.BlockSpec((1, QR), lambda i: (0, 0)), pl.BlockSpec((1, KR), lambda i: (0, 0))],
        out_specs=[pl.BlockSpec((tm, QR), lambda i: (i, 0)), pl.BlockSpec((tm, KR), lambda i: (i, 0)),
                   pl.BlockSpec((tm, LANE), lambda i: (i, 0))],
        out_shape=[jax.ShapeDtypeStruct((T, QR), BF16), jax.ShapeDtypeStruct((T, KR), F32),
                   jax.ShapeDtypeStruct((T, LANE), BF16)],
        compiler_params=_cp("parallel"), name="mla_pre",
    )(p_cq, p_ckv, p_small, cos, sin, qw.reshape(1, QR), kvw.reshape(1, KR))


def _attn_kernel(scale, qn_ref, qr_ref, cos_ref, sin_ref, kn_ref, kr_ref, v_ref, o_ref):
    qn = _b(qn_ref[...] * scale)
    qr = _b(_rope128(qr_ref[...], cos_ref[...], sin_ref[...]) * scale)
    s = _dot_nt(qn, kn_ref[...]) + _dot_nt(qr, kr_ref[...])
    m = jnp.max(s, axis=-1, keepdims=True)
    p = jnp.exp(s - m)
    l = jnp.sum(p, axis=-1, keepdims=True)
    o_ref[...] = (_dot(_b(p), v_ref[...]) / l).astype(o_ref.dtype)


def _attention(q, cos, sin, kv, kr, q_row0, k_row0, n_seq, tq_len, tk_len, n_heads):
    H = n_heads
    tq = _pick((256, 128), tq_len)
    assert q_row0 % tq == 0 and k_row0 % tk_len == 0
    nq = tq_len // tq
    qb0 = q_row0 // tq
    kb0 = k_row0 // tk_len
    scale = MLA_QK ** -0.5
    return pl.pallas_call(
        functools.partial(_attn_kernel, scale),
        grid=(n_seq, H, nq),
        in_specs=[pl.BlockSpec((tq, LANE), lambda s, h, i: (qb0 + s * nq + i, h)),
                  pl.BlockSpec((tq, LANE), lambda s, h, i: (qb0 + s * nq + i, H + h)),
                  pl.BlockSpec((tq, LANE), lambda s, h, i: (qb0 + s * nq + i, 0)),
                  pl.BlockSpec((tq, LANE), lambda s, h, i: (qb0 + s * nq + i, 0)),
                  pl.BlockSpec((tk_len, LANE), lambda s, h, i: (kb0 + s, 2 * h)),
                  pl.BlockSpec((tk_len, LANE), lambda s, h, i: (kb0 + s, 0)),
                  pl.BlockSpec((tk_len, LANE), lambda s, h, i: (kb0 + s, 2 * h + 1))],
        out_specs=pl.BlockSpec((tq, LANE), lambda s, h, i: (s * nq + i, h)),
        out_shape=jax.ShapeDtypeStruct((n_seq * tq_len, H * MLA_V), BF16),
        compiler_params=_cp("parallel", "parallel", "arbitrary"), name="mla_attention",
    )(q, q, cos, sin, kv, kr, kv)


def _gather_kernel(rows, idx_ref, nused_ref, src_ref, o_ref, sem):
    i = pl.program_id(0)

    @pl.when(i < nused_ref[0])
    def _():
        def start(r, c):
            t = idx_ref[i * rows + r]
            pltpu.make_async_copy(src_ref.at[pl.ds(t, 1), :], o_ref.at[pl.ds(r, 1), :], sem.at[0]).start()
            return c

        lax.fori_loop(0, rows, start, 0, unroll=DMA_UNROLL)

        def wait(r, c):
            pltpu.make_async_copy(src_ref.at[pl.ds(0, 1), :], o_ref.at[pl.ds(r, 1), :], sem.at[0]).wait()
            return c

        lax.fori_loop(0, rows, wait, 0, unroll=DMA_UNROLL)

    @pl.when(i >= nused_ref[0])
    def _():
        o_ref[...] = jnp.zeros(o_ref.shape, o_ref.dtype)


def _gather_rows(src, idx, n_used, rows):
    n, width = idx.shape[0], src.shape[1]
    grid_spec = pltpu.PrefetchScalarGridSpec(
        num_scalar_prefetch=2, grid=(n // rows,),
        in_specs=[pl.BlockSpec(memory_space=pl.ANY)],
        out_specs=pl.BlockSpec((rows, width), lambda i, idx, nu: (i, 0)),
        scratch_shapes=[pltpu.SemaphoreType.DMA((1,))])
    return pl.pallas_call(
        functools.partial(_gather_kernel, rows), grid_spec=grid_spec,
        out_shape=jax.ShapeDtypeStruct((n, width), src.dtype),
        compiler_params=_cp("arbitrary"), name="moe_gather",
    )(idx, n_used, src)


def _expert_up_kernel(be_ref, nused_ref, x_ref, wg_ref, wu_ref, hid_ref):
    blk = pl.program_id(0)

    @pl.when(blk < nused_ref[0])
    def _():
        xa, xb = _unpack_bf16_halves(x_ref[...])
        half = xa.shape[1]
        g = _dot(xa, _b(wg_ref[:half, :])) + _dot(xb, _b(wg_ref[half:, :]))
        u = _dot(xa, _b(wu_ref[:half, :])) + _dot(xb, _b(wu_ref[half:, :]))
        hid_ref[...] = (_silu(g) * u).astype(hid_ref.dtype)

    @pl.when(blk >= nused_ref[0])
    def _():
        hid_ref[...] = jnp.zeros(hid_ref.shape, hid_ref.dtype)


def _expert_down_kernel(be_ref, nused_ref, hid_ref, wd_ref, gate_ref, o_ref):
    blk = pl.program_id(0)

    @pl.when(blk < nused_ref[0])
    def _():
        o_ref[...] = _dot(hid_ref[...], _b(wd_ref[...])) * gate_ref[...]

    @pl.when(blk >= nused_ref[0])
    def _():
        o_ref[...] = jnp.zeros(o_ref.shape, F32)


def _experts(xs, slot_gate, block_e, n_used, w_gate, w_up, w_down, layer, tm):
    n_slots = xs.shape[0]
    D, DE = w_gate.shape[-2:]
    up_spec = pltpu.PrefetchScalarGridSpec(
        num_scalar_prefetch=2, grid=(n_slots // tm,),
        in_specs=[pl.BlockSpec((tm, D // 2), lambda i, be, nu: (i, 0)),
                  pl.BlockSpec((None, None, D, DE), lambda i, be, nu: (layer, be[i], 0, 0)),
                  pl.BlockSpec((None, None, D, DE), lambda i, be, nu: (layer, be[i], 0, 0))],
        out_specs=pl.BlockSpec((tm, DE), lambda i, be, nu: (i, 0)))
    hid = pl.pallas_call(
        _expert_up_kernel, grid_spec=up_spec,
        out_shape=jax.ShapeDtypeStruct((n_slots, DE), BF16),
        compiler_params=_cp("arbitrary", vmem=58 * 1024 * 1024), name="moe_up",
    )(block_e, n_used, xs, w_gate, w_up)
    down_spec = pltpu.PrefetchScalarGridSpec(
        num_scalar_prefetch=2, grid=(n_slots // tm,),
        in_specs=[pl.BlockSpec((tm, DE), lambda i, be, nu: (i, 0)),
                  pl.BlockSpec((None, None, DE, D), lambda i, be, nu: (layer, be[i], 0, 0)),
                  pl.BlockSpec((tm, 1), lambda i, be, nu: (i, 0))],
        out_specs=pl.BlockSpec((tm, D), lambda i, be, nu: (i, 0)))
    return pl.pallas_call(
        _expert_down_kernel, grid_spec=down_spec,
        out_shape=jax.ShapeDtypeStruct((n_slots, D), F32),
        compiler_params=_cp("arbitrary"), name="moe_down",
    )(block_e, n_used, hid, w_down, slot_gate.reshape(n_slots, 1))


def _combine_kernel(rows, p0_ref, p1_ref, x_ref, g_ref, ys_ref, o_ref, a_scr, b_scr, sem):
    i = pl.program_id(0)

    def start(r, c):
        pltpu.make_async_copy(ys_ref.at[pl.ds(p0_ref[i * rows + r], 1), :], a_scr.at[pl.ds(r, 1), :], sem.at[0]).start()
        pltpu.make_async_copy(ys_ref.at[pl.ds(p1_ref[i * rows + r], 1), :], b_scr.at[pl.ds(r, 1), :], sem.at[1]).start()
        return c

    lax.fori_loop(0, rows, start, 0, unroll=DMA_UNROLL)

    def wait(r, c):
        pltpu.make_async_copy(ys_ref.at[pl.ds(0, 1), :], a_scr.at[pl.ds(r, 1), :], sem.at[0]).wait()
        pltpu.make_async_copy(ys_ref.at[pl.ds(0, 1), :], b_scr.at[pl.ds(r, 1), :], sem.at[1]).wait()
        return c

    lax.fori_loop(0, rows, wait, 0, unroll=DMA_UNROLL)
    o_ref[...] = x_ref[...] + g_ref[...] * (a_scr[...] + b_scr[...])


def _combine(x, mod3, ys, p0, p1, tok, rows):
    T, D = x.shape
    grp = tok.group_of_tile(rows)
    grid_spec = pltpu.PrefetchScalarGridSpec(
        num_scalar_prefetch=2, grid=(T // rows,),
        in_specs=[pl.BlockSpec((rows, D), lambda i, a, b: (i, 0)),
                  pl.BlockSpec((None, 1, D), lambda i, a, b: (grp(i), 0, 5)),
                  pl.BlockSpec(memory_space=pl.ANY)],
        out_specs=pl.BlockSpec((rows, D), lambda i, a, b: (i, 0)),
        scratch_shapes=[pltpu.VMEM((rows, D), F32), pltpu.VMEM((rows, D), F32), pltpu.SemaphoreType.DMA((2,))])
    return pl.pallas_call(
        functools.partial(_combine_kernel, rows), grid_spec=grid_spec,
        out_shape=jax.ShapeDtypeStruct((T, D), F32),
        compiler_params=_cp("arbitrary"), name="moe_combine",
    )(p0, p1, x, mod3, ys)


def _route(logits, n_groups, n_experts, tm):
    n_tok = logits.shape[0]
    epg = n_experts // n_groups
    gl = logits[:, :n_groups]
    pg = jax.nn.softmax(gl, axis=-1)
    grp = jnp.argmax(gl, axis=-1)
    pg_sel = jnp.max(pg, axis=-1, keepdims=True)
    el = logits[:, n_groups:n_groups + n_experts].reshape(n_tok, n_groups, epg)
    el = jnp.take_along_axis(el, grp[:, None, None], axis=1)[:, 0]
    top_p, top_i = lax.top_k(jax.nn.softmax(el, axis=-1), TOP_K)
    gate = pg_sel * top_p / jnp.sum(top_p, axis=-1, keepdims=True)
    expert = grp[:, None].astype(jnp.int32) * epg + top_i.astype(jnp.int32)
    flat_e = expert.reshape(-1)
    n_assign = n_tok * TOP_K
    onehot = (flat_e[:, None] == jnp.arange(n_experts, dtype=jnp.int32)[None, :]).astype(jnp.int32)
    csum = jnp.cumsum(onehot, axis=0)
    rank = jnp.take_along_axis(csum, flat_e[:, None], axis=1)[:, 0] - 1
    counts = csum[-1]
    padded = (counts + tm - 1) // tm * tm
    pend = jnp.cumsum(padded)
    pstart = pend - padded
    dest = (pstart[flat_e] + rank).astype(jnp.int32)
    n_blocks = n_assign // tm + n_experts
    n_slots = n_blocks * tm
    slot_tok = jnp.zeros((n_slots,), jnp.int32).at[dest].set(jnp.arange(n_assign, dtype=jnp.int32) // TOP_K)
    slot_gate = jnp.zeros((n_slots,), F32).at[dest].set(gate.reshape(-1))
    block_e = jnp.minimum(jnp.searchsorted(pend, jnp.arange(n_blocks, dtype=jnp.int32) * tm, side='right'),
                          n_experts - 1).astype(jnp.int32)
    n_used = (pend[-1] // tm).astype(jnp.int32).reshape(1)
    dest2 = dest.reshape(n_tok, TOP_K)
    return slot_tok, slot_gate, block_e, n_used, dest2[:, 0], dest2[:, 1]


def _rope_rot_cols(w):
    q = MLA_ROPE // 4
    return jnp.concatenate([-w[..., q:2 * q], w[..., 0:q], -w[..., 3 * q:4 * q], w[..., 2 * q:3 * q]], axis=-1)


def _pad_cols(w, n):
    return jnp.pad(w, ((0, 0), (0, n - w.shape[1])))


def _rope_tables(tok):
    half = MLA_ROPE // 2
    nfreq = half // 2
    inv = ROPE_THETA ** (-jnp.arange(nfreq, dtype=F32) / nfreq)
    t = jnp.arange(tok.dec_seq, dtype=jnp.int32)
    ang_r = (t // GRID_W).astype(F32)[:, None] * inv
    ang_c = (t % GRID_W).astype(F32)[:, None] * inv
    z = jnp.zeros((tok.dec_seq, LANE - MLA_ROPE), F32)
    cos = jnp.concatenate([jnp.cos(ang_r), jnp.cos(ang_r), jnp.cos(ang_c), jnp.cos(ang_c), z], axis=1)
    sin = jnp.concatenate([jnp.sin(ang_r), jnp.sin(ang_r), jnp.sin(ang_c), jnp.sin(ang_c), z], axis=1)
    ctx_cos = jnp.concatenate([jnp.ones((tok.t_ctx, MLA_ROPE), F32), jnp.zeros((tok.t_ctx, LANE - MLA_ROPE), F32)], axis=1)
    cos = jnp.concatenate([ctx_cos] + [cos] * tok.n_lat_seq, axis=0)
    sin = jnp.concatenate([jnp.zeros((tok.t_ctx, LANE), F32)] + [sin] * tok.n_lat_seq, axis=0)
    return cos, sin


def kernel(x_prompt, x_sample, state_gdn, state_rwkv, cache_mla_ckv, cache_mla_krope, c, c_ctx, ada_w, ada_b, norm_mix_w, norm_ffn_w, w_in, gdn_conv_w, gdn_A_log, gdn_dt_bias, gdn_norm_w, rwkv_mu, rwkv_w0, rwkv_w_up, rwkv_a0, rwkv_a_up, rwkv_g_up, rwkv_k_k, rwkv_k_a, rwkv_r_k, rwkv_ln_w, rwkv_ln_b, mla_q_norm_w, mla_q_up, mla_kv_norm_w, mla_kv_up, w_out, router_group_w, router_group_b, router_expert_w, router_expert_b, moe_w_gate, moe_w_up, moe_w_down, final_norm_w):
    B, SEQ, D = x_prompt.shape
    DB, DSEQ, _ = x_sample.shape
    L = w_in.shape[0]
    PAST = cache_mla_ckv.shape[2]
    GH = gdn_A_log.shape[-1]
    GW = GH * GDN_HEAD
    RH = rwkv_r_k.shape[1]
    RW = RH * RWKV_HEAD
    RP = RW // LANE
    DL, AL, GL = rwkv_w_up.shape[2], rwkv_a_up.shape[2], rwkv_g_up.shape[1]
    QR = mla_q_norm_w.shape[1]
    KR = mla_kv_norm_w.shape[1]
    MH = mla_kv_up.shape[2] // (MLA_NOPE + MLA_V)
    NG = router_group_w.shape[2]
    NE = router_expert_w.shape[2]
    assert DL + AL == LANE and GL == LANE and 4 * GH <= LANE and NG + NE <= LANE and 1 + DB <= SUB
    tok = _Tok(B, SEQ, DB, DSEQ)
    T = tok.t
    tm = _pick((256, 128, 64), SEQ, DSEQ)
    tm_moe = 256

    gdn_cols = 4 * GW + 4 * GH
    rwkv_cols = 3 * RW + DL + AL + GL
    o_r = gdn_cols
    o_m = gdn_cols + rwkv_cols

    x = jnp.concatenate([x_prompt.reshape(B * SEQ, D), x_sample.reshape(DB * DSEQ, D)], axis=0)
    cvec = jnp.zeros((SUB, D), F32).at[0].set(c_ctx).at[1:1 + DB].set(c)
    mod = _ada_mod(cvec, ada_w, ada_b)
    cos, sin = _rope_tables(tok)

    new_gdn, new_rwkv, new_ckv, new_kr = [], [], [], []
    for l in range(L):
        mod3 = mod[l].reshape(SUB, 1, 6 * D)
        wl = w_in[l]
        w_gdn = _b(wl[:, :4 * GW])
        w_rwkv = _b(wl[:, o_r:o_r + 3 * RW])
        w_cq = _b(wl[:, o_m:o_m + QR])
        w_ckv = _b(wl[:, o_m + QR:o_m + QR + KR])
        w_kr = wl[:, o_m + QR + KR:o_m + QR + KR + MLA_ROPE]
        w_small = _b(jnp.concatenate([
            _pad_cols(wl[:, 4 * GW:4 * GW + 4 * GH], LANE),
            wl[:, o_r + 3 * RW:o_r + 3 * RW + DL + AL],
            wl[:, o_r + 3 * RW + DL + AL:o_r + rwkv_cols],
            w_kr, _rope_rot_cols(w_kr)], axis=1))

        h1 = _norm_mod(x, norm_mix_w[l], mod3, 0, tok, tm)
        p_gdn = _matmul(h1, w_gdn, name="proj_gdn")
        p_rwkv = _matmul(h1, w_rwkv, name="proj_rwkv")
        p_cq = _matmul(h1, w_cq, name="proj_cq")
        p_ckv = _matmul(h1, w_ckv, name="proj_ckv")
        p_small = _matmul(h1, w_small, name="proj_small")

        gate_par = jnp.zeros((SUB, LANE), F32)
        gate_par = gate_par.at[0, :2 * GH].set(-jnp.exp(gdn_A_log[l].reshape(-1)))
        gate_par = gate_par.at[1, :2 * GH].set(gdn_dt_bias[l].reshape(-1))
        qkv, gates = _gdn_pre(p_gdn, p_small, gdn_conv_w[l].T, gate_par, tok, GH, tm)
        gcol = gates[:, :4 * GH].reshape(T // CHUNK, CHUNK, 4 * GH)
        grow = jnp.swapaxes(gates[:, :2 * GH].reshape(T // CHUNK, CHUNK, 2 * GH), 1, 2).reshape(
            T // CHUNK, GH, 2 * CHUNK)
        oc_f, sc_f = _gdn_chunk(qkv, gcol, grow, 0, B, SEQ, GH, False, emit=True)
        oc_b, sc_b = _gdn_chunk(qkv, gcol, grow, 0, B, SEQ, GH, True, emit=True)
        ol_f, = _gdn_chunk(qkv, gcol, grow, tok.t_ctx, DB, DSEQ, GH, False, s0=state_gdn, s0_idx=(l, 0))
        ol_b, = _gdn_chunk(qkv, gcol, grow, tok.t_ctx, DB, DSEQ, GH, True, s0=state_gdn, s0_idx=(l, 1))
        o_gdn = _gdn_out(jnp.concatenate([oc_f, ol_f], axis=0), jnp.concatenate([oc_b, ol_b], axis=0),
                         p_gdn, gdn_norm_w[l], GH, tm)
        new_gdn.append(jnp.stack([sc_f, sc_b], axis=1))

        ln = (rwkv_ln_w[l].reshape(1, RW), rwkv_ln_b[l].reshape(1, RW), rwkv_r_k[l].reshape(1, RW))
        y_dir, s_dir = [], []
        for d in range(2):
            mu = rwkv_mu[l, d]
            wup = jnp.concatenate([rwkv_w_up[l, d], jnp.zeros((AL, RW), F32)], axis=0)
            aup = jnp.concatenate([jnp.zeros((DL, RW), F32), rwkv_a_up[l, d]], axis=0)
            par = (mu[:RW].reshape(1, RW), mu[RW:2 * RW].reshape(1, RW), mu[2 * RW:3 * RW].reshape(1, RW),
                   mu[3 * RW:].reshape(1, LANE), rwkv_w0[l, d].reshape(1, RW), wup,
                   rwkv_a0[l, d].reshape(1, RW), aup, rwkv_k_k[l].reshape(1, RW), rwkv_k_a[l].reshape(1, RW))
            pre = _rwkv_pre(p_rwkv, p_small, d, par, tok, tm)
            s0 = state_rwkv[:, l, d].reshape(DB, RP, 2, RWKV_HEAD, RWKV_HEAD)
            s0 = jnp.einsum('bpaij,ac->bpaicj', s0, jnp.eye(2, dtype=F32)).reshape(DB, RP, LANE, LANE)
            yc, sc = _rwkv_chunk(pre, ln, 0, B, SEQ, d == 1, emit=True)
            yl, = _rwkv_chunk(pre, ln, tok.t_ctx, DB, DSEQ, d == 1, s0=s0)
            y_dir.append(jnp.concatenate([yc, yl], axis=0))
            sc = sc.reshape(B, RP, 2, RWKV_HEAD, 2, RWKV_HEAD)
            s_dir.append(jnp.einsum('bpaiaj->bpaij', sc).reshape(B, RH, RWKV_HEAD, RWKV_HEAD))
        o_rwkv = _rwkv_out(y_dir[0], y_dir[1], p_small, rwkv_g_up[l], tm)
        new_rwkv.append(jnp.stack(s_dir, axis=1))

        cqn, ckvn, krf = _mla_pre(p_cq, p_ckv, p_small, cos, sin, mla_q_norm_w[l], mla_kv_norm_w[l], tm)
        qu = mla_q_up[l].reshape(QR, MH, MLA_QK)
        q_rope = qu[:, :, MLA_NOPE:]
        w_q = _b(jnp.concatenate([qu[:, :, :MLA_NOPE].reshape(QR, MH * MLA_NOPE),
                                  jnp.concatenate([q_rope, _rope_rot_cols(q_rope)], axis=-1).reshape(QR, MH * LANE)],
                                 axis=1))
        q = _matmul(cqn, w_q, name="q_up")
        kr_cache = _b(jnp.pad(cache_mla_krope[:, l], ((0, 0), (0, 0), (0, LANE - MLA_ROPE))))
        ckv_parts, kr_parts = [], []
        for b in range(DB):
            r0 = tok.t_ctx + b * DSEQ
            ckv_parts += [_b(cache_mla_ckv[b, l]), _b(ckvn[r0:r0 + DSEQ])]
            kr_parts += [kr_cache[b], krf[r0:r0 + DSEQ]]
        ckv_all = jnp.concatenate(ckv_parts + [_b(ckvn[:tok.t_ctx])], axis=0)
        kr_all = jnp.concatenate(kr_parts + [krf[:tok.t_ctx]], axis=0)
        kv = _matmul(ckv_all, _b(mla_kv_up[l]), out_dtype=BF16, name="kv_up")
        o_c = _attention(q, cos, sin, kv, kr_all, 0, DB * (PAST + DSEQ), B, SEQ, SEQ, MH)
        o_l = _attention(q, cos, sin, kv, kr_all, tok.t_ctx, 0, DB, DSEQ, PAST + DSEQ, MH)
        o_mla = jnp.concatenate([o_c, o_l], axis=0)
        new_ckv.append(ckvn[:tok.t_ctx].reshape(B, SEQ, KR))
        new_kr.append(p_small[:tok.t_ctx, 3 * LANE:3 * LANE + MLA_ROPE].reshape(B, SEQ, MLA_ROPE))

        x = _wout_residual(x, mod3, o_gdn, o_rwkv, o_mla, _b(w_out[l]), tok)

        rw = _pad_cols(jnp.concatenate([router_group_w[l], router_expert_w[l]], axis=1), LANE)
        rb = _pad_cols(jnp.concatenate([router_group_b[l], router_expert_b[l]])[None, :], LANE)
        h2, logits = _norm_mod(x, norm_ffn_w[l], mod3, 3, tok, tm, router=(rw, rb))
        slot_tok, slot_gate, block_e, n_used, p0, p1 = _route(logits, NG, NE, tm_moe)
        xs = _gather_rows(h2, slot_tok, n_used, tm_moe)
        ys = _experts(xs, slot_gate, block_e, n_used, moe_w_gate, moe_w_up, moe_w_down, l, tm_moe)
        x = _combine(x, mod3, ys, p0, p1, tok, tm)

    y = _final_norm(x, final_norm_w, tm)
    y_prompt = y[:tok.t_ctx].reshape(B, SEQ, D)
    y_sample = y[tok.t_ctx:].reshape(DB, DSEQ, D)
    return (y_prompt, y_sample, jnp.stack(new_gdn, axis=1), jnp.stack(new_rwkv, axis=1),
            jnp.stack(new_ckv, axis=1), jnp.stack(new_kr, axis=1))
```

```python
import functools

import numpy as np
import jax
import jax.numpy as jnp
from jax import lax
from jax.experimental import pallas as pl
from jax.experimental.pallas import tpu as pltpu

F32 = jnp.float32
BF16 = jnp.bfloat16
HI = lax.Precision.HIGHEST

EPS = 1e-6
L2_EPS = 1e-6
GRID_W = 64
ROPE_THETA = 10000.0
GDN_HEAD = 128
RWKV_HEAD = 64
RWKV_GN_EPS = RWKV_HEAD * 1e-5
MLA_V = 128
MLA_NOPE = 128
MLA_ROPE = 64
MLA_QK = MLA_NOPE + MLA_ROPE
TOP_K = 2
CHUNK = 64
LANE = 128
SUB = 8
DMA_UNROLL = 8
GDN_SEQ_PER_STEP = 4
RWKV_SEQ_PER_STEP = 2
VMEM_LIMIT = 52 * 1024 * 1024


def _cp(*sem, vmem=VMEM_LIMIT):
    return pltpu.CompilerParams(dimension_semantics=sem, vmem_limit_bytes=vmem)


def _pick(prefs, *ns):
    for p in prefs:
        if all(n % p == 0 for n in ns):
            return p
    raise ValueError(f"no tile in {prefs} divides {ns}")


def _dot(a, b, prec=None):
    return lax.dot_general(a, b, (((1,), (0,)), ((), ())), precision=prec, preferred_element_type=F32)


def _dot_nt(a, b, prec=None):
    return lax.dot_general(a, b, (((1,), (1,)), ((), ())), precision=prec, preferred_element_type=F32)


def _dot_tn(a, b, prec=None):
    return lax.dot_general(a, b, (((0,), (0,)), ((), ())), precision=prec, preferred_element_type=F32)


def _b(x):
    return x.astype(BF16)


def _sigmoid(x):
    return 1.0 / (1.0 + jnp.exp(-x))


def _silu(x):
    return x * _sigmoid(x)


def _softplus(x):
    return jnp.maximum(x, 0.0) + jnp.log(1.0 + jnp.exp(-jnp.abs(x)))


def _ada_kernel(c_ref, w_ref, b_ref, o_ref):
    s = _silu(c_ref[...])
    o_ref[...] = _dot(_b(s), _b(w_ref[...])) + b_ref[...]


def _ada_mod(cvec, ada_w, ada_b):
    L, D, N = ada_w.shape
    R = cvec.shape[0]
    tn = _pick((1024, 512, 256, 128), N)
    return pl.pallas_call(
        _ada_kernel,
        grid=(L, N // tn),
        in_specs=[pl.BlockSpec((R, D), lambda l, j: (0, 0)),
                  pl.BlockSpec((None, D, tn), lambda l, j: (l, 0, j)),
                  pl.BlockSpec((None, 1, tn), lambda l, j: (l, 0, j))],
        out_specs=pl.BlockSpec((None, R, tn), lambda l, j: (l, 0, j)),
        out_shape=jax.ShapeDtypeStruct((L, R, N), F32),
        compiler_params=_cp("parallel", "parallel"),
        name="ada_mod",
    )(cvec, ada_w, ada_b.reshape(L, 1, N))


def _norm_kernel(x_ref, w_ref, o_ref):
    x = x_ref[...]
    y = x * lax.rsqrt(jnp.mean(x * x, axis=-1, keepdims=True) + EPS) * w_ref[...]
    o_ref[...] = y.astype(o_ref.dtype)


def _norm_mod_kernel(x_ref, w_ref, sc_ref, sh_ref, o_ref):
    x = x_ref[...]
    y = x * lax.rsqrt(jnp.mean(x * x, axis=-1, keepdims=True) + EPS) * w_ref[...]
    o_ref[...] = (y * (1.0 + sc_ref[...]) + sh_ref[...]).astype(o_ref.dtype)


def _pack_bf16_halves(h):
    half = h.shape[1] // 2
    a = lax.bitcast_convert_type(h[:, :half].astype(BF16).astype(F32), jnp.uint32)
    b = lax.bitcast_convert_type(h[:, half:].astype(BF16).astype(F32), jnp.uint32)
    return a | (b >> 16)


def _unpack_bf16_halves(w):
    a = lax.bitcast_convert_type(w & jnp.uint32(0xFFFF0000), F32).astype(BF16)
    b = lax.bitcast_convert_type(w << 16, F32).astype(BF16)
    return a, b


def _norm_router_kernel(x_ref, w_ref, sc_ref, sh_ref, rw_ref, rb_ref, o_ref, lg_ref):
    x = x_ref[...]
    y = x * lax.rsqrt(jnp.mean(x * x, axis=-1, keepdims=True) + EPS) * w_ref[...]
    h = y * (1.0 + sc_ref[...]) + sh_ref[...]
    o_ref[...] = _pack_bf16_halves(h)
    lg_ref[...] = _dot(h, rw_ref[...], HI) + rb_ref[...]


class _Tok:
    def __init__(self, n_ctx_seq, seq, n_lat_seq, dec_seq):
        self.n_ctx_seq, self.seq, self.n_lat_seq, self.dec_seq = n_ctx_seq, seq, n_lat_seq, dec_seq
        self.t_ctx = n_ctx_seq * seq
        self.t_lat = n_lat_seq * dec_seq
        self.t = self.t_ctx + self.t_lat

    def group_of_tile(self, tm):
        assert self.t_ctx % tm == 0 and self.dec_seq % tm == 0
        nct, tpl = self.t_ctx // tm, self.dec_seq // tm
        return lambda i: jnp.where(i < nct, 0, 1 + (i - nct) // tpl)

    def seq_flags(self, tm):
        assert self.seq % tm == 0 and self.dec_seq % tm == 0
        starts = np.arange(0, self.t, tm)
        first = np.where(starts < self.t_ctx, starts % self.seq == 0, (starts - self.t_ctx) % self.dec_seq == 0)
        ends = starts + tm
        last = np.where(starts < self.t_ctx, ends % self.seq == 0, (ends - self.t_ctx) % self.dec_seq == 0)
        return jnp.asarray(first, jnp.int32), jnp.asarray(last, jnp.int32)


def _norm_mod(x, w, mod3, seg, tok, tm, router=None):
    T, D = x.shape
    grp = tok.group_of_tile(tm)
    in_specs = [pl.BlockSpec((tm, D), lambda i: (i, 0)),
                pl.BlockSpec((1, D), lambda i: (0, 0)),
                pl.BlockSpec((None, 1, D), lambda i: (grp(i), 0, seg + 1)),
                pl.BlockSpec((None, 1, D), lambda i: (grp(i), 0, seg))]
    if router is None:
        return pl.pallas_call(
            _norm_mod_kernel, grid=(T // tm,), in_specs=in_specs,
            out_specs=pl.BlockSpec((tm, D), lambda i: (i, 0)),
            out_shape=jax.ShapeDtypeStruct((T, D), BF16),
            compiler_params=_cp("parallel"), name="norm_mod",
        )(x, w.reshape(1, D), mod3, mod3)
    rw, rb = router
    NR = rw.shape[1]
    return pl.pallas_call(
        _norm_router_kernel, grid=(T // tm,),
        in_specs=in_specs + [pl.BlockSpec((D, NR), lambda i: (0, 0)), pl.BlockSpec((1, NR), lambda i: (0, 0))],
        out_specs=[pl.BlockSpec((tm, D // 2), lambda i: (i, 0)), pl.BlockSpec((tm, NR), lambda i: (i, 0))],
        out_shape=[jax.ShapeDtypeStruct((T, D // 2), jnp.uint32), jax.ShapeDtypeStruct((T, NR), F32)],
        compiler_params=_cp("parallel"), name="norm_router",
    )(x, w.reshape(1, D), mod3, mod3, rw, rb)


def _final_norm(x, w, tm):
    T, D = x.shape
    return pl.pallas_call(
        _norm_kernel, grid=(T // tm,),
        in_specs=[pl.BlockSpec((tm, D), lambda i: (i, 0)), pl.BlockSpec((1, D), lambda i: (0, 0))],
        out_specs=pl.BlockSpec((tm, D), lambda i: (i, 0)),
        out_shape=jax.ShapeDtypeStruct((T, D), F32),
        compiler_params=_cp("parallel"), name="final_norm",
    )(x, w.reshape(1, D))


def _mm_kernel(a_ref, b_ref, o_ref):
    o_ref[...] = _dot(a_ref[...], b_ref[...]).astype(o_ref.dtype)


def _matmul(a, b, out_dtype=F32, name="matmul"):
    M, K = a.shape
    N = b.shape[1]
    tm = _pick((512, 256, 128, 64, 8), M)
    tn = _pick((1024, 512, 256, 128), N)
    return pl.pallas_call(
        _mm_kernel, grid=(N // tn, M // tm),
        in_specs=[pl.BlockSpec((tm, K), lambda j, i: (i, 0)), pl.BlockSpec((K, tn), lambda j, i: (0, j))],
        out_specs=pl.BlockSpec((tm, tn), lambda j, i: (i, j)),
        out_shape=jax.ShapeDtypeStruct((M, N), out_dtype),
        compiler_params=_cp("parallel", "parallel"), name=name,
    )(a, b)


def _wout_kernel(x_ref, g_ref, a1_ref, a2_ref, a3_ref, w1_ref, w2_ref, w3_ref, o_ref):
    mix = _dot(a1_ref[...], w1_ref[...]) + _dot(a2_ref[...], w2_ref[...]) + _dot(a3_ref[...], w3_ref[...])
    o_ref[...] = x_ref[...] + g_ref[...] * mix


def _wout_residual(x, mod3, a1, a2, a3, w, tok):
    T, D = x.shape
    k1, k2, k3 = a1.shape[1], a2.shape[1], a3.shape[1]
    assert k1 % k2 == 0 and (k1 + k2) % k3 == 0
    tm = _pick((512, 256, 128), tok.t_ctx, tok.dec_seq)
    tn = _pick((1024, 512, 256, 128), D)
    grp = tok.group_of_tile(tm)
    nj = D // tn
    return pl.pallas_call(
        _wout_kernel, grid=(nj, T // tm),
        in_specs=[pl.BlockSpec((tm, tn), lambda j, i: (i, j)),
                  pl.BlockSpec((None, 1, tn), lambda j, i: (grp(i), 0, 2 * nj + j)),
                  pl.BlockSpec((tm, k1), lambda j, i: (i, 0)),
                  pl.BlockSpec((tm, k2), lambda j, i: (i, 0)),
                  pl.BlockSpec((tm, k3), lambda j, i: (i, 0)),
                  pl.BlockSpec((k1, tn), lambda j, i: (0, j)),
                  pl.BlockSpec((k2, tn), lambda j, i: (k1 // k2, j)),
                  pl.BlockSpec((k3, tn), lambda j, i: ((k1 + k2) // k3, j))],
        out_specs=pl.BlockSpec((tm, tn), lambda j, i: (i, j)),
        out_shape=jax.ShapeDtypeStruct((T, D), F32),
        compiler_params=_cp("parallel", "parallel"), name="wout_residual",
    )(x, mod3, a1, a2, a3, w, w, w)


def _split_bf16(x, terms):
    out = []
    for _ in range(terms):
        p = x.astype(BF16)
        out.append(p)
        x = x - p.astype(F32)
    return out


def _dot_exact_r(a01, x, terms=3):
    return sum(_dot(a01, p) for p in _split_bf16(x, terms))


def _dot_exact_l(x, b01, terms=3):
    return sum(_dot(p, b01) for p in _split_bf16(x, terms))


def _dot_nt_exact_l(x, b01, terms=3):
    return sum(_dot_nt(p, b01) for p in _split_bf16(x, terms))


def _one_bf16(mask):
    return jnp.where(mask, 1.0, 0.0).astype(BF16)


def _pair_masks(C, rev):
    M = 2 * C
    ti = lax.broadcasted_iota(jnp.int32, (M, M), 0)
    si = lax.broadcasted_iota(jnp.int32, (M, M), 1)
    shift = C.bit_length() - 1
    assert (1 << shift) == C
    same = jnp.right_shift(ti, shift) == jnp.right_shift(si, shift)
    before = (si > ti) if rev else (si < ti)
    strict = same & before
    incl = same & (before | (ti == si))
    tc = lax.broadcasted_iota(jnp.int32, (C, C), 0)
    sc = lax.broadcasted_iota(jnp.int32, (C, C), 1)
    cum = _one_bf16((sc >= tc) if rev else (sc <= tc))
    return ti, si, shift, same, strict, incl, cum


def _tri_inv_n(a_list, ti, si, top_shift):
    def same(shift):
        return jnp.right_shift(ti, shift) == jnp.right_shift(si, shift)

    n_mat = range(len(a_list))
    m3 = same(3)
    ad = [jnp.where(m3, a, 0.0) for a in a_list]
    adb = [_b(x) for x in ad]
    b2 = [_dot(x, x) for x in adb]
    b2b = [_b(x) for x in b2]
    n1 = [b2[i] - ad[i] - _dot(adb[i], b2b[i]) for i in n_mat]
    b4 = [_dot(x, x) for x in b2b]
    n = [n1[i] + b4[i] + _dot(_b(n1[i]), _b(b4[i])) for i in n_mat]
    for shift in range(3, top_shift):
        moff = same(shift + 1) & jnp.logical_not(same(shift))
        aoff = [jnp.where(moff, a, 0.0) for a in a_list]
        aob = [_b(x) for x in aoff]
        nb = [_b(x) for x in n]
        x = [aoff[i] + _dot(nb[i], aob[i]) for i in n_mat]
        n = [n[i] - x[i] - _dot(_b(x[i]), nb[i]) for i in n_mat]
    return n


def _shift_rows(x, halo_prev_ref, halo_next_ref, first, last, direction):
    tm = x.shape[0]
    rows = lax.broadcasted_iota(jnp.int32, (tm, 1), 0)
    if direction == 0:
        edge = halo_prev_ref[SUB - 1:SUB, :] * (1.0 - first)
        return jnp.where(rows == 0, edge, pltpu.roll(x, 1, 0))
    edge = halo_next_ref[0:1, :] * (1.0 - last)
    return jnp.where(rows == tm - 1, edge, pltpu.roll(x, tm - 1, 0))


def _halo_specs(tm, width, col, n_rows):
    r = tm // SUB
    nb = n_rows // SUB
    prev = pl.BlockSpec((SUB, width), lambda i, f, l: (jnp.maximum(i * r - 1, 0), col))
    nxt = pl.BlockSpec((SUB, width), lambda i, f, l: (jnp.minimum((i + 1) * r, nb - 1), col))
    return prev, nxt


def _gdn_pre_kernel(n_heads, first_ref, last_ref, x_ref, xp_ref, xn_ref, cw_ref, ab_ref, gp_ref,
                    qkv_ref, gates_ref):
    i = pl.program_id(0)
    first = first_ref[i].astype(F32)
    last = last_ref[i].astype(F32)
    x = x_ref[...]
    xm = _shift_rows(x, xp_ref, xn_ref, first, last, 0)
    xq = _shift_rows(x, xp_ref, xn_ref, first, last, 1)
    y = _silu(xm * cw_ref[0:1, :] + x * cw_ref[1:2, :] + xq * cw_ref[2:3, :])
    for hd in range(3 * n_heads):
        seg = y[:, hd * GDN_HEAD:(hd + 1) * GDN_HEAD]
        if hd < 2 * n_heads:
            seg = seg * lax.rsqrt(jnp.sum(seg * seg, axis=-1, keepdims=True) + L2_EPS)
            if hd < n_heads:
                seg = seg * (GDN_HEAD ** -0.5)
        qkv_ref[:, hd * GDN_HEAD:(hd + 1) * GDN_HEAD] = seg
    ab = ab_ref[...]
    lane = lax.broadcasted_iota(jnp.int32, ab.shape, 1)
    g = gp_ref[0:1, :] * _softplus(ab + gp_ref[1:2, :])
    gates_ref[...] = jnp.where(lane < 2 * n_heads, g, _sigmoid(ab))


def _gdn_pre(p_gdn, p_small, conv_w3, gate_par, tok, n_heads, tm):
    T = p_gdn.shape[0]
    W3 = 3 * n_heads * GDN_HEAD
    first, last = tok.seq_flags(tm)
    hp, hn = _halo_specs(tm, W3, 0, T)
    grid_spec = pltpu.PrefetchScalarGridSpec(
        num_scalar_prefetch=2, grid=(T // tm,),
        in_specs=[pl.BlockSpec((tm, W3), lambda i, f, l: (i, 0)), hp, hn,
                  pl.BlockSpec((3, W3), lambda i, f, l: (0, 0)),
                  pl.BlockSpec((tm, LANE), lambda i, f, l: (i, 0)),
                  pl.BlockSpec((SUB, LANE), lambda i, f, l: (0, 0))],
        out_specs=[pl.BlockSpec((tm, W3), lambda i, f, l: (i, 0)),
                   pl.BlockSpec((tm, LANE), lambda i, f, l: (i, 0))])
    return pl.pallas_call(
        functools.partial(_gdn_pre_kernel, n_heads), grid_spec=grid_spec,
        out_shape=[jax.ShapeDtypeStruct((T, W3), F32), jax.ShapeDtypeStruct((T, LANE), F32)],
        compiler_params=_cp("parallel"), name="gdn_pre",
    )(first, last, p_gdn, p_gdn, p_gdn, conv_w3, p_small, gate_par)


def _gdn_chunk_kernel(n_heads, n_grp, rev, has_s0, emit, *refs):
    refs = list(refs)
    G = n_grp
    in_refs = [refs[5 * g:5 * g + 5] for g in range(G)]
    pos = 5 * G
    s0_ref = None
    if has_s0:
        s0_ref = refs[pos]
        pos += 1
    o_ref = refs[pos]
    pos += 1
    so_ref = None
    if emit:
        so_ref = refs[pos]
        pos += 1
    s_scr = refs[pos]

    n = pl.program_id(1)
    n_last = pl.num_programs(1) - 1
    C = in_refs[0][0].shape[0]
    H = n_heads
    d = 1 if rev else 0

    @pl.when(n == 0)
    def _():
        if has_s0:
            s_scr[...] = s0_ref[...]
        else:
            s_scr[...] = jnp.zeros(s_scr.shape, F32)

    M = 2 * C
    HP = H // 2
    assert H % 2 == 0 and M == LANE
    ti, si, shift, same, strict, incl, cum = _pair_masks(C, rev)
    incl_b = _one_bf16(incl)
    ncol = in_refs[0][3].shape[1]
    er = lax.broadcasted_iota(jnp.int32, (ncol, 2 * H * LANE), 0)
    ec = lax.broadcasted_iota(jnp.int32, (ncol, 2 * H * LANE), 1) // LANE
    want = jnp.where(ec < H, d * H + ec, 2 * H + d * H + (ec - H))
    expand = _one_bf16(er == want)
    last_row = 0 if rev else C - 1

    def sl(h):
        return slice(h * GDN_HEAD, (h + 1) * GDN_HEAD)

    def pair_rows(x, p):
        return jnp.concatenate([x[:, sl(2 * p)], x[:, sl(2 * p + 1)]], axis=0)

    arr = []
    for g in range(G):
        q_ref, k_ref, v_ref, gcol_ref, grow_ref = in_refs[g]
        gb_all = _dot_exact_l(gcol_ref[...], expand)
        gcb = _dot_exact_r(cum, gb_all[:, :H * LANE])
        beta = gb_all[:, H * LANE:]
        gc_rows = _dot_nt_exact_l(grow_ref[d * HP:(d + 1) * HP, :], incl_b)
        q = q_ref[...]
        k = k_ref[...]
        egc = jnp.exp(gcb)
        gl = gcb[last_row:last_row + 1, :]
        kb = k * beta
        arr.append(dict(q=q, k=k, gcb=gcb, gc_rows=gc_rows, kb=kb, vbeta=v_ref[...] * beta, kbe=kb * egc,
                        qd=q * egc, kd=k * jnp.exp(gl - gcb), sdec=jnp.exp(gl)))

    pair_units = [(g, p) for g in range(G) for p in range(HP)]
    head_units = [(g, h) for g in range(G) for h in range(H)]
    pu = range(len(pair_units))

    def pidx(g, p):
        return g * HP + p

    k_st = [_b(pair_rows(arr[g]['k'], p)) for g, p in pair_units]
    lhs = [_b(jnp.concatenate([pair_rows(arr[g]['kb'], p), pair_rows(arr[g]['q'], p)], axis=0)) for g, p in pair_units]
    sc = [_dot_nt(lhs[u], k_st[u]) for u in pu]
    decay = [jnp.exp(jnp.where(incl, pair_rows(arr[g]['gcb'], p)
                               - jnp.broadcast_to(arr[g]['gc_rows'][p:p + 1, :], (M, M)), -jnp.inf))
             for g, p in pair_units]
    a_mat = [jnp.where(strict, sc[u][:M] * decay[u], 0.0) for u in pu]
    attn_b = [_b(sc[u][M:] * decay[u]) for u in pu]
    n_inv = _tri_inv_n(a_mat, ti, si, shift)
    rhs = [jnp.concatenate([pair_rows(arr[g]['vbeta'], p), pair_rows(arr[g]['kbe'], p)], axis=1)
           for g, p in pair_units]
    sol = [rhs[u] + _dot(_b(n_inv[u]), _b(rhs[u])) for u in pu]
    s_old = [s_scr[g, h] for g, h in head_units]
    s_b = [_b(s) for s in s_old]
    ws = [_dot(_b(jnp.concatenate([sol[pidx(g, h // 2)][(h % 2) * C:(h % 2 + 1) * C, GDN_HEAD:],
                                   arr[g]['qd'][:, sl(h)]], axis=0)), s_b[g * H + h])
          for g, h in head_units]
    v_new = [sol[u][:, :GDN_HEAD] - jnp.concatenate([ws[g * H + 2 * p][:C], ws[g * H + 2 * p + 1][:C]], axis=0)
             for u, (g, p) in enumerate(pair_units)]
    vn_b = [_b(x) for x in v_new]
    o = [jnp.concatenate([ws[g * H + 2 * p][C:], ws[g * H + 2 * p + 1][C:]], axis=0) + _dot(attn_b[u], vn_b[u])
         for u, (g, p) in enumerate(pair_units)]
    for u, (g, p) in enumerate(pair_units):
        o_ref[g, :, sl(2 * p)] = o[u][:C]
        o_ref[g, :, sl(2 * p + 1)] = o[u][C:]
    for g, h in head_units:
        upd = _dot_tn(_b(arr[g]['kd'][:, sl(h)]), vn_b[pidx(g, h // 2)][(h % 2) * C:(h % 2 + 1) * C])
        s_scr[g, h] = s_old[g * H + h] * arr[g]['sdec'][:, sl(h)] + upd

    if emit:
        @pl.when(n == n_last)
        def _():
            so_ref[...] = s_scr[...]


def _gdn_chunk(qkv, gcol, grow, row0, n_seq, seq_len, n_heads, rev, s0=None, s0_idx=None, emit=False):
    T = qkv.shape[0]
    C = CHUNK
    N = seq_len // C
    H = n_heads
    W = H * GDN_HEAD
    b0 = row0 // C

    G = min(GDN_SEQ_PER_STEP, n_seq)
    assert n_seq % G == 0

    def chunk(n):
        return (N - 1 - n) if rev else n

    def blk(s, n, g):
        return b0 + (s * G + g) * N + chunk(n)

    in_specs, args = [], []
    for g in range(G):
        in_specs += [pl.BlockSpec((C, W), lambda s, n, g=g: (blk(s, n, g), 0)),
                     pl.BlockSpec((C, W), lambda s, n, g=g: (blk(s, n, g), 1)),
                     pl.BlockSpec((C, W), lambda s, n, g=g: (blk(s, n, g), 2)),
                     pl.BlockSpec((None, C, gcol.shape[2]), lambda s, n, g=g: (blk(s, n, g), 0, 0)),
                     pl.BlockSpec((None, grow.shape[1], grow.shape[2]), lambda s, n, g=g: (blk(s, n, g), 0, 0))]
        args += [qkv, qkv, qkv, gcol, grow]
    if s0 is not None:
        l, d = s0_idx
        in_specs.append(pl.BlockSpec((G, None, None, H, GDN_HEAD, GDN_HEAD), lambda s, n: (s, l, d, 0, 0, 0)))
        args.append(s0)
    out_specs = [pl.BlockSpec((G, C, W), lambda s, n: (s, chunk(n), 0))]
    out_shape = [jax.ShapeDtypeStruct((n_seq, seq_len, W), F32)]
    if emit:
        out_specs.append(pl.BlockSpec((G, H, GDN_HEAD, GDN_HEAD), lambda s, n: (s, 0, 0, 0)))
        out_shape.append(jax.ShapeDtypeStruct((n_seq, H, GDN_HEAD, GDN_HEAD), F32))
    res = pl.pallas_call(
        functools.partial(_gdn_chunk_kernel, H, G, rev, s0 is not None, emit),
        grid=(n_seq // G, N), in_specs=in_specs, out_specs=out_specs, out_shape=out_shape,
        scratch_shapes=[pltpu.VMEM((G, H, GDN_HEAD, GDN_HEAD), F32)],
        compiler_params=_cp("parallel", "arbitrary"), name="gdn_chunk",
    )(*args)
    res = list(res)
    res[0] = res[0].reshape(n_seq * seq_len, W)
    return res


def _gdn_out_kernel(n_heads, of_ref, ob_ref, z_ref, nw_ref, o_ref):
    for h in range(n_heads):
        sl = slice(h * GDN_HEAD, (h + 1) * GDN_HEAD)
        o = of_ref[:, sl] + ob_ref[:, sl]
        y = o * lax.rsqrt(jnp.mean(o * o, axis=-1, keepdims=True) + EPS) * nw_ref[...]
        o_ref[:, sl] = (y * _silu(z_ref[:, sl])).astype(o_ref.dtype)


def _gdn_out(o_f, o_b, p_gdn, norm_w, n_heads, tm):
    T, W = o_f.shape
    return pl.pallas_call(
        functools.partial(_gdn_out_kernel, n_heads), grid=(T // tm,),
        in_specs=[pl.BlockSpec((tm, W), lambda i: (i, 0)), pl.BlockSpec((tm, W), lambda i: (i, 0)),
                  pl.BlockSpec((tm, W), lambda i: (i, 3)), pl.BlockSpec((1, GDN_HEAD), lambda i: (0, 0))],
        out_specs=pl.BlockSpec((tm, W), lambda i: (i, 0)),
        out_shape=jax.ShapeDtypeStruct((T, W), BF16),
        compiler_params=_cp("parallel"), name="gdn_out",
    )(o_f, o_b, p_gdn, norm_w.reshape(1, GDN_HEAD))


def _pair_ones():
    r = lax.broadcasted_iota(jnp.int32, (LANE, LANE), 0) // RWKV_HEAD
    c = lax.broadcasted_iota(jnp.int32, (LANE, LANE), 1) // RWKV_HEAD
    return r == c


def _rwkv_pre_kernel(direction, first_ref, last_ref, r_ref, k_ref, v_ref, f_ref,
                     rp_ref, rn_ref, kp_ref, kn_ref, vp_ref, vn_ref, fp_ref, fn_ref,
                     mu_r_ref, mu_k_ref, mu_v_ref, mu_f_ref, w0_ref, wup_ref, a0_ref, aup_ref,
                     kkw_ref, kaw_ref,
                     ro_ref, ko_ref, vo_ref, kko_ref, bo_ref, lwo_ref):
    i = pl.program_id(0)
    first = first_ref[i].astype(F32)
    last = last_ref[i].astype(F32)

    def mix(x_ref, p_ref, n_ref, mu_ref):
        x = x_ref[...]
        return x + (_shift_rows(x, p_ref, n_ref, first, last, direction) - x) * mu_ref[...]

    r = mix(r_ref, rp_ref, rn_ref, mu_r_ref)
    k = mix(k_ref, kp_ref, kn_ref, mu_k_ref)
    v = mix(v_ref, vp_ref, vn_ref, mu_v_ref)
    f = mix(f_ref, fp_ref, fn_ref, mu_f_ref)
    w_lin = w0_ref[...] + _dot(_b(jnp.tanh(f)), _b(wup_ref[...]))
    w_log = -_softplus(-w_lin) - 0.5
    a = _sigmoid(a0_ref[...] + _dot(_b(f), _b(aup_ref[...])))
    kraw = k * kkw_ref[...]
    ones = _pair_ones().astype(F32)
    W = kraw.shape[1]
    for p in range(W // LANE):
        sl = slice(p * LANE, (p + 1) * LANE)
        kr = kraw[:, sl]
        ss = _dot(kr * kr, ones, HI)
        kk = kr * lax.rsqrt(ss + L2_EPS)
        kko_ref[:, sl] = kk
        bo_ref[:, sl] = kk * a[:, sl]
    ro_ref[...] = r
    ko_ref[...] = k * (1.0 + (a - 1.0) * kaw_ref[...])
    vo_ref[...] = v
    lwo_ref[...] = -jnp.exp(w_log)


def _rwkv_pre(p_rwkv, p_small, direction, par, tok, tm):
    T, W3 = p_rwkv.shape
    W = W3 // 3
    first, last = tok.seq_flags(tm)
    main = [pl.BlockSpec((tm, W), lambda i, f, l, c=c: (i, c)) for c in range(3)]
    main.append(pl.BlockSpec((tm, LANE), lambda i, f, l: (i, 1)))
    halos = []
    for c in range(3):
        halos.extend(_halo_specs(tm, W, c, T))
    halos.extend(_halo_specs(tm, LANE, 1, T))
    row = lambda n: pl.BlockSpec((1, n), lambda i, f, l: (0, 0))
    full = lambda a, b: pl.BlockSpec((a, b), lambda i, f, l: (0, 0))
    params = [row(W), row(W), row(W), row(LANE), row(W), full(LANE, W), row(W), full(LANE, W), row(W), row(W)]
    grid_spec = pltpu.PrefetchScalarGridSpec(
        num_scalar_prefetch=2, grid=(T // tm,),
        in_specs=main + halos + params,
        out_specs=[pl.BlockSpec((tm, W), lambda i, f, l: (i, 0)) for _ in range(6)])
    return pl.pallas_call(
        functools.partial(_rwkv_pre_kernel, direction), grid_spec=grid_spec,
        out_shape=[jax.ShapeDtypeStruct((T, W), F32) for _ in range(6)],
        compiler_params=_cp("parallel"), name="rwkv_pre",
    )(first, last, p_rwkv, p_rwkv, p_rwkv, p_small,
      p_rwkv, p_rwkv, p_rwkv, p_rwkv, p_rwkv, p_rwkv, p_small, p_small, *par)


def _rwkv_chunk_kernel(n_pairs, n_grp, rev, has_s0, emit, *refs):
    refs = list(refs)
    G = n_grp
    in_refs = [refs[6 * g:6 * g + 6] for g in range(G)]
    pos = 6 * G
    lnw_ref, lnb_ref, rk_ref = refs[pos:pos + 3]
    pos += 3
    s0_ref = None
    if has_s0:
        s0_ref = refs[pos]
        pos += 1
    y_ref = refs[pos]
    pos += 1
    so_ref = None
    if emit:
        so_ref = refs[pos]
        pos += 1
    s_scr = refs[pos]

    n = pl.program_id(1)
    n_last = pl.num_programs(1) - 1
    C = in_refs[0][0].shape[0]

    @pl.when(n == 0)
    def _():
        if has_s0:
            s_scr[...] = s0_ref[...]
        else:
            s_scr[...] = jnp.zeros(s_scr.shape, F32)

    M = 2 * C
    assert M == LANE and RWKV_HEAD == C
    ti, si, shift, same, strict, incl, cum = _pair_masks(C, rev)
    ones_b = _one_bf16(same)
    last_row = 0 if rev else C - 1
    inv_n = 1.0 / RWKV_HEAD
    pairs = range(n_pairs)
    sls = [slice(p * LANE, (p + 1) * LANE) for p in pairs]

    def st2(x):
        return jnp.concatenate([x, x], axis=0)

    def fold(x):
        return x[:C] + x[C:]

    def seg_sum(x):
        return _dot_exact_l(x, ones_b, 2)

    arr = []
    for g in range(G):
        r_ref, k_ref, v_ref, kk_ref, b_ref, lw_ref = in_refs[g]
        r = r_ref[...]
        k = k_ref[...]
        v = v_ref[...]
        lw = lw_ref[...]
        cl = _dot_exact_r(cum, lw)
        cll = cl[last_row:last_row + 1, :]
        e_neg = jnp.exp(-cl)
        e_end = jnp.exp(cll - cl)
        arr.append(dict(r=r, k=k, v=v, rtb=_b(r * jnp.exp(cl)), ktb=_b(k * e_neg), btb=_b(b_ref[...] * e_neg),
                        atb=_b(-kk_ref[...] * jnp.exp(cl - lw)), keb=_b(k * e_end), beb=_b(b_ref[...] * e_end),
                        vb=_b(v), s_dec=jnp.exp(cll)))

    units = [(g, p) for g in range(G) for p in pairs]
    un = range(len(units))

    def col(name, u):
        g, p = units[u]
        return arr[g][name][:, sls[p]]

    s_old = [s_scr[g, p] for g, p in units]
    s_b = [_b(s) for s in s_old]
    x0 = [_dot_nt(jnp.concatenate([col('atb', u), col('rtb', u)], axis=0), s_b[u]) for u in un]
    la = [jnp.concatenate([st2(col('atb', u)) * ones_b, st2(col('rtb', u)) * ones_b], axis=0) for u in un]
    sk = [_dot_nt(la[u], st2(col('ktb', u))) for u in un]
    sb = [_dot_nt(la[u], st2(col('btb', u))) for u in un]
    a_ak = [_b(jnp.where(strict, x[:M], 0.0)) for x in sk]
    r_k = [_b(jnp.where(incl, x[M:], 0.0)) for x in sk]
    a_neg = [jnp.where(strict, -x[:M], 0.0) for x in sb]
    r_b = [_b(jnp.where(incl, x[M:], 0.0)) for x in sb]
    n_inv = _tri_inv_n(a_neg, ti, si, shift)
    v_st = [st2(col('vb', u)) * ones_b for u in un]
    wm = [x0[u][:C] + fold(_dot(a_ak[u], v_st[u])) for u in un]
    uu = [wm[u] + fold(_dot(_b(n_inv[u]), st2(_b(wm[u])) * ones_b)) for u in un]
    u_b = [_b(x) for x in uu]
    u_st = [st2(x) * ones_b for x in u_b]
    y = [x0[u][C:] + fold(_dot(jnp.concatenate([r_k[u], r_b[u]], axis=1),
                               jnp.concatenate([v_st[u], u_st[u]], axis=0))) for u in un]
    upd = [_dot_tn(jnp.concatenate([col('vb', u), u_b[u]], axis=0),
                   jnp.concatenate([col('keb', u), col('beb', u)], axis=0)) for u in un]
    for u, (g, p) in enumerate(units):
        s_scr[g, p] = s_old[u] * col('s_dec', u) + jnp.where(same, upd[u], 0.0)
    for u, (g, p) in enumerate(units):
        sl = sls[p]
        mu = seg_sum(y[u]) * inv_n
        yc = y[u] - mu
        var = seg_sum(yc * yc) * inv_n
        yn = yc * lax.rsqrt(var + RWKV_GN_EPS) * lnw_ref[:, sl] + lnb_ref[:, sl]
        bonus = seg_sum(col('r', u) * col('k', u) * rk_ref[:, sl])
        y_ref[g, :, sl] = yn + bonus * col('v', u)

    if emit:
        @pl.when(n == n_last)
        def _():
            so_ref[...] = s_scr[...]


def _rwkv_chunk(pre, ln, row0, n_seq, seq_len, rev, s0=None, emit=False):
    r = pre[0]
    W = r.shape[1]
    P = W // LANE
    C = CHUNK
    N = seq_len // C
    b0 = row0 // C

    G = min(RWKV_SEQ_PER_STEP, n_seq)
    assert n_seq % G == 0

    def chunk(n):
        return (N - 1 - n) if rev else n

    in_specs, args = [], []
    for g in range(G):
        in_specs += [pl.BlockSpec((C, W), lambda s, n, g=g: (b0 + (s * G + g) * N + chunk(n), 0)) for _ in range(6)]
        args += list(pre)
    in_specs += [pl.BlockSpec((1, W), lambda s, n: (0, 0)) for _ in range(3)]
    args += list(ln)
    if s0 is not None:
        in_specs.append(pl.BlockSpec((G, P, LANE, LANE), lambda s, n: (s, 0, 0, 0)))
        args.append(s0)
    out_specs = [pl.BlockSpec((G, C, W), lambda s, n: (s, chunk(n), 0))]
    out_shape = [jax.ShapeDtypeStruct((n_seq, seq_len, W), F32)]
    if emit:
        out_specs.append(pl.BlockSpec((G, P, LANE, LANE), lambda s, n: (s, 0, 0, 0)))
        out_shape.append(jax.ShapeDtypeStruct((n_seq, P, LANE, LANE), F32))
    res = list(pl.pallas_call(
        functools.partial(_rwkv_chunk_kernel, P, G, rev, s0 is not None, emit),
        grid=(n_seq // G, N), in_specs=in_specs, out_specs=out_specs, out_shape=out_shape,
        scratch_shapes=[pltpu.VMEM((G, P, LANE, LANE), F32)],
        compiler_params=_cp("parallel", "arbitrary"), name="rwkv_chunk",
    )(*args))
    res[0] = res[0].reshape(n_seq * seq_len, W)
    return res


def _rwkv_out_kernel(yf_ref, yb_ref, fg_ref, gup_ref, o_ref):
    g = _dot(_b(_sigmoid(fg_ref[...])), _b(gup_ref[...]))
    o_ref[...] = ((yf_ref[...] + yb_ref[...]) * g).astype(o_ref.dtype)


def _rwkv_out(y_f, y_b, p_small, g_up, tm):
    T, W = y_f.shape
    return pl.pallas_call(
        _rwkv_out_kernel, grid=(T // tm,),
        in_specs=[pl.BlockSpec((tm, W), lambda i: (i, 0)), pl.BlockSpec((tm, W), lambda i: (i, 0)),
                  pl.BlockSpec((tm, LANE), lambda i: (i, 2)), pl.BlockSpec((LANE, W), lambda i: (0, 0))],
        out_specs=pl.BlockSpec((tm, W), lambda i: (i, 0)),
        out_shape=jax.ShapeDtypeStruct((T, W), BF16),
        compiler_params=_cp("parallel"), name="rwkv_out",
    )(y_f, y_b, p_small, g_up)


def _rope128(x, cos, sin):
    return x * cos + pltpu.roll(x, MLA_ROPE, 1) * sin


def _mla_pre_kernel(cq_ref, ckv_ref, kr_ref, cos_ref, sin_ref, qw_ref, kvw_ref, cqn_ref, ckvn_ref, krf_ref):
    cq = cq_ref[...]
    cqn_ref[...] = (cq * lax.rsqrt(jnp.mean(cq * cq, axis=-1, keepdims=True) + EPS) * qw_ref[...]).astype(BF16)
    ckv = ckv_ref[...]
    ckvn_ref[...] = ckv * lax.rsqrt(jnp.mean(ckv * ckv, axis=-1, keepdims=True) + EPS) * kvw_ref[...]
    krf_ref[...] = _rope128(kr_ref[...], cos_ref[...], sin_ref[...]).astype(BF16)


def _mla_pre(p_cq, p_ckv, p_small, cos, sin, qw, kvw, tm):
    T, QR = p_cq.shape
    KR = p_ckv.shape[1]
    return pl.pallas_call(
        _mla_pre_kernel, grid=(T // tm,),
        in_specs=[pl.BlockSpec((tm, QR), lambda i: (i, 0)), pl.BlockSpec((tm, KR), lambda i: (i, 0)),
                  pl.BlockSpec((tm, LANE), lambda i: (i, 3)),
                  pl.BlockSpec((tm, LANE), lambda i: (i, 0)), pl.BlockSpec((tm, LANE), lambda i: (i, 0)),
                  pl.BlockSpec((1, QR), lambda i: (0, 0)), pl.BlockSpec((1, KR), lambda i: (0, 0))],
        out_specs=[pl.BlockSpec((tm, QR), lambda i: (i, 0)), pl.BlockSpec((tm, KR), lambda i: (i, 0)),
                   pl.BlockSpec((tm, LANE), lambda i: (i, 0))],
        out_shape=[jax.ShapeDtypeStruct((T, QR), BF16), jax.ShapeDtypeStruct((T, KR), F32),
                   jax.ShapeDtypeStruct((T, LANE), BF16)],
        compiler_params=_cp("parallel"), name="mla_pre",
    )(p_cq, p_ckv, p_small, cos, sin, qw.reshape(1, QR), kvw.reshape(1, KR))


def _attn_kernel(scale, qn_ref, qr_ref, cos_ref, sin_ref, kn_ref, kr_ref, v_ref, o_ref):
    qn = _b(qn_ref[...] * scale)
    qr = _b(_rope128(qr_ref[...], cos_ref[...], sin_ref[...]) * scale)
    s = _dot_nt(qn, kn_ref[...]) + _dot_nt(qr, kr_ref[...])
    m = jnp.max(s, axis=-1, keepdims=True)
    p = jnp.exp(s - m)
    l = jnp.sum(p, axis=-1, keepdims=True)
    o_ref[...] = (_dot(_b(p), v_ref[...]) / l).astype(o_ref.dtype)


def _attention(q, cos, sin, kv, kr, q_row0, k_row0, n_seq, tq_len, tk_len, n_heads):
    H = n_heads
    tq = _pick((256, 128), tq_len)
    assert q_row0 % tq == 0 and k_row0 % tk_len == 0
    nq = tq_len // tq
    qb0 = q_row0 // tq
    kb0 = k_row0 // tk_len
    scale = MLA_QK ** -0.5
    return pl.pallas_call(
        functools.partial(_attn_kernel, scale),
        grid=(n_seq, H, nq),
        in_specs=[pl.BlockSpec((tq, LANE), lambda s, h, i: (qb0 + s * nq + i, h)),
                  pl.BlockSpec((tq, LANE), lambda s, h, i: (qb0 + s * nq + i, H + h)),
                  pl.BlockSpec((tq, LANE), lambda s, h, i: (qb0 + s * nq + i, 0)),
                  pl.BlockSpec((tq, LANE), lambda s, h, i: (qb0 + s * nq + i, 0)),
                  pl.BlockSpec((tk_len, LANE), lambda s, h, i: (kb0 + s, 2 * h)),
                  pl.BlockSpec((tk_len, LANE), lambda s, h, i: (kb0 + s, 0)),
                  pl.BlockSpec((tk_len, LANE), lambda s, h, i: (kb0 + s, 2 * h + 1))],
        out_specs=pl.BlockSpec((tq, LANE), lambda s, h, i: (s * nq + i, h)),
        out_shape=jax.ShapeDtypeStruct((n_seq * tq_len, H * MLA_V), BF16),
        compiler_params=_cp("parallel", "parallel", "arbitrary"), name="mla_attention",
    )(q, q, cos, sin, kv, kr, kv)


def _row_copy(src_ref, row, dst_ref, slot, r, sem):
    return pltpu.make_async_copy(src_ref.at[pl.ds(row, 1), :], dst_ref.at[slot, pl.ds(r, 1), :], sem.at[slot])


def _expert_up_kernel(rows, tok_ref, be_ref, nused_ref, h_ref, wg_ref, wu_ref, hid_ref, x_buf, sem):
    i = pl.program_id(0)
    n_used = nused_ref[0]
    slot = lax.rem(i, 2)

    def gather(blk, dst_slot):
        def start(r, c):
            _row_copy(h_ref, tok_ref[blk * rows + r], x_buf, dst_slot, r, sem).start()
            return c

        lax.fori_loop(0, rows, start, 0, unroll=DMA_UNROLL)

    @pl.when(jnp.logical_and(i == 0, n_used > 0))
    def _():
        gather(0, 0)

    @pl.when(i + 1 < n_used)
    def _():
        gather(i + 1, 1 - slot)

    @pl.when(i < n_used)
    def _():
        def wait(r, c):
            _row_copy(h_ref, 0, x_buf, slot, r, sem).wait()
            return c

        lax.fori_loop(0, rows, wait, 0, unroll=DMA_UNROLL)
        xa, xb = _unpack_bf16_halves(x_buf[slot])
        half = xa.shape[1]
        g = _dot(xa, _b(wg_ref[:half, :])) + _dot(xb, _b(wg_ref[half:, :]))
        u = _dot(xa, _b(wu_ref[:half, :])) + _dot(xb, _b(wu_ref[half:, :]))
        hid_ref[...] = (_silu(g) * u).astype(hid_ref.dtype)

    @pl.when(i >= n_used)
    def _():
        hid_ref[...] = jnp.zeros(hid_ref.shape, hid_ref.dtype)


def _expert_down_kernel(be_ref, nused_ref, hid_ref, wd_ref, o_ref):
    blk = pl.program_id(0)

    @pl.when(blk < nused_ref[0])
    def _():
        o_ref[...] = _dot(hid_ref[...], _b(wd_ref[...]))

    @pl.when(blk >= nused_ref[0])
    def _():
        o_ref[...] = jnp.zeros(o_ref.shape, F32)


def _experts(h_packed, slot_tok, block_e, n_used, w_gate, w_up, w_down, layer, tm):
    n_slots = slot_tok.shape[0]
    D, DE = w_gate.shape[-2:]
    up_spec = pltpu.PrefetchScalarGridSpec(
        num_scalar_prefetch=3, grid=(n_slots // tm,),
        in_specs=[pl.BlockSpec(memory_space=pl.ANY),
                  pl.BlockSpec((None, None, D, DE), lambda i, tk, be, nu: (layer, be[i], 0, 0)),
                  pl.BlockSpec((None, None, D, DE), lambda i, tk, be, nu: (layer, be[i], 0, 0))],
        out_specs=pl.BlockSpec((tm, DE), lambda i, tk, be, nu: (i, 0)),
        scratch_shapes=[pltpu.VMEM((2, tm, D // 2), jnp.uint32), pltpu.SemaphoreType.DMA((2,))])
    hid = pl.pallas_call(
        functools.partial(_expert_up_kernel, tm), grid_spec=up_spec,
        out_shape=jax.ShapeDtypeStruct((n_slots, DE), BF16),
        compiler_params=_cp("arbitrary", vmem=58 * 1024 * 1024), name="moe_up",
    )(slot_tok, block_e, n_used, h_packed, w_gate, w_up)
    down_spec = pltpu.PrefetchScalarGridSpec(
        num_scalar_prefetch=2, grid=(n_slots // tm,),
        in_specs=[pl.BlockSpec((tm, DE), lambda i, be, nu: (i, 0)),
                  pl.BlockSpec((None, None, DE, D), lambda i, be, nu: (layer, be[i], 0, 0))],
        out_specs=pl.BlockSpec((tm, D), lambda i, be, nu: (i, 0)))
    return pl.pallas_call(
        _expert_down_kernel, grid_spec=down_spec,
        out_shape=jax.ShapeDtypeStruct((n_slots, D), F32),
        compiler_params=_cp("arbitrary"), name="moe_down",
    )(block_e, n_used, hid, w_down)


def _combine_kernel(rows, p0_ref, p1_ref, x_ref, g_ref, ga_ref, gb_ref, ys_ref, o_ref, a_scr, b_scr, sem):
    i = pl.program_id(0)

    def start(r, c):
        pltpu.make_async_copy(ys_ref.at[pl.ds(p0_ref[i * rows + r], 1), :], a_scr.at[pl.ds(r, 1), :], sem.at[0]).start()
        pltpu.make_async_copy(ys_ref.at[pl.ds(p1_ref[i * rows + r], 1), :], b_scr.at[pl.ds(r, 1), :], sem.at[1]).start()
        return c

    lax.fori_loop(0, rows, start, 0, unroll=DMA_UNROLL)

    def wait(r, c):
        pltpu.make_async_copy(ys_ref.at[pl.ds(0, 1), :], a_scr.at[pl.ds(r, 1), :], sem.at[0]).wait()
        pltpu.make_async_copy(ys_ref.at[pl.ds(0, 1), :], b_scr.at[pl.ds(r, 1), :], sem.at[1]).wait()
        return c

    lax.fori_loop(0, rows, wait, 0, unroll=DMA_UNROLL)
    o_ref[...] = x_ref[...] + g_ref[...] * (a_scr[...] * ga_ref[...] + b_scr[...] * gb_ref[...])


def _combine(x, mod3, ys, p0, p1, gate, tok, rows):
    T, D = x.shape
    grp = tok.group_of_tile(rows)
    grid_spec = pltpu.PrefetchScalarGridSpec(
        num_scalar_prefetch=2, grid=(T // rows,),
        in_specs=[pl.BlockSpec((rows, D), lambda i, a, b: (i, 0)),
                  pl.BlockSpec((None, 1, D), lambda i, a, b: (grp(i), 0, 5)),
                  pl.BlockSpec((rows, 1), lambda i, a, b: (i, 0)),
                  pl.BlockSpec((rows, 1), lambda i, a, b: (i, 0)),
                  pl.BlockSpec(memory_space=pl.ANY)],
        out_specs=pl.BlockSpec((rows, D), lambda i, a, b: (i, 0)),
        scratch_shapes=[pltpu.VMEM((rows, D), F32), pltpu.VMEM((rows, D), F32), pltpu.SemaphoreType.DMA((2,))])
    return pl.pallas_call(
        functools.partial(_combine_kernel, rows), grid_spec=grid_spec,
        out_shape=jax.ShapeDtypeStruct((T, D), F32),
        compiler_params=_cp("arbitrary"), name="moe_combine",
    )(p0, p1, x, mod3, gate[:, 0:1], gate[:, 1:2], ys)


def _route(logits, n_groups, n_experts, tm):
    n_tok = logits.shape[0]
    epg = n_experts // n_groups
    gl = logits[:, :n_groups]
    pg = jax.nn.softmax(gl, axis=-1)
    grp = jnp.argmax(gl, axis=-1)
    pg_sel = jnp.max(pg, axis=-1, keepdims=True)
    el = logits[:, n_groups:n_groups + n_experts].reshape(n_tok, n_groups, epg)
    el = jnp.take_along_axis(el, grp[:, None, None], axis=1)[:, 0]
    top_p, top_i = lax.top_k(jax.nn.softmax(el, axis=-1), TOP_K)
    gate = pg_sel * top_p / jnp.sum(top_p, axis=-1, keepdims=True)
    expert = grp[:, None].astype(jnp.int32) * epg + top_i.astype(jnp.int32)
    flat_e = expert.reshape(-1)
    n_assign = n_tok * TOP_K
    onehot = (flat_e[:, None] == jnp.arange(n_experts, dtype=jnp.int32)[None, :]).astype(jnp.int32)
    csum = jnp.cumsum(onehot, axis=0)
    rank = jnp.take_along_axis(csum, flat_e[:, None], axis=1)[:, 0] - 1
    counts = csum[-1]
    padded = (counts + tm - 1) // tm * tm
    pend = jnp.cumsum(padded)
    pstart = pend - padded
    dest = (pstart[flat_e] + rank).astype(jnp.int32)
    n_blocks = n_assign // tm + n_experts
    n_slots = n_blocks * tm
    slot_tok = jnp.zeros((n_slots,), jnp.int32).at[dest].set(jnp.arange(n_assign, dtype=jnp.int32) // TOP_K)
    block_e = jnp.minimum(jnp.searchsorted(pend, jnp.arange(n_blocks, dtype=jnp.int32) * tm, side='right'),
                          n_experts - 1).astype(jnp.int32)
    n_used = (pend[-1] // tm).astype(jnp.int32).reshape(1)
    dest2 = dest.reshape(n_tok, TOP_K)
    return slot_tok, gate, block_e, n_used, dest2[:, 0], dest2[:, 1]


def _rope_rot_cols(w):
    q = MLA_ROPE // 4
    return jnp.concatenate([-w[..., q:2 * q], w[..., 0:q], -w[..., 3 * q:4 * q], w[..., 2 * q:3 * q]], axis=-1)


def _pad_cols(w, n):
    return jnp.pad(w, ((0, 0), (0, n - w.shape[1])))


def _rope_tables(tok):
    half = MLA_ROPE // 2
    nfreq = half // 2
    inv = ROPE_THETA ** (-jnp.arange(nfreq, dtype=F32) / nfreq)
    t = jnp.arange(tok.dec_seq, dtype=jnp.int32)
    ang_r = (t // GRID_W).astype(F32)[:, None] * inv
    ang_c = (t % GRID_W).astype(F32)[:, None] * inv
    z = jnp.zeros((tok.dec_seq, LANE - MLA_ROPE), F32)
    cos = jnp.concatenate([jnp.cos(ang_r), jnp.cos(ang_r), jnp.cos(ang_c), jnp.cos(ang_c), z], axis=1)
    sin = jnp.concatenate([jnp.sin(ang_r), jnp.sin(ang_r), jnp.sin(ang_c), jnp.sin(ang_c), z], axis=1)
    ctx_cos = jnp.concatenate([jnp.ones((tok.t_ctx, MLA_ROPE), F32), jnp.zeros((tok.t_ctx, LANE - MLA_ROPE), F32)], axis=1)
    cos = jnp.concatenate([ctx_cos] + [cos] * tok.n_lat_seq, axis=0)
    sin = jnp.concatenate([jnp.zeros((tok.t_ctx, LANE), F32)] + [sin] * tok.n_lat_seq, axis=0)
    return cos, sin


def kernel(x_prompt, x_sample, state_gdn, state_rwkv, cache_mla_ckv, cache_mla_krope, c, c_ctx, ada_w, ada_b, norm_mix_w, norm_ffn_w, w_in, gdn_conv_w, gdn_A_log, gdn_dt_bias, gdn_norm_w, rwkv_mu, rwkv_w0, rwkv_w_up, rwkv_a0, rwkv_a_up, rwkv_g_up, rwkv_k_k, rwkv_k_a, rwkv_r_k, rwkv_ln_w, rwkv_ln_b, mla_q_norm_w, mla_q_up, mla_kv_norm_w, mla_kv_up, w_out, router_group_w, router_group_b, router_expert_w, router_expert_b, moe_w_gate, moe_w_up, moe_w_down, final_norm_w):
    B, SEQ, D = x_prompt.shape
    DB, DSEQ, _ = x_sample.shape
    L = w_in.shape[0]
    PAST = cache_mla_ckv.shape[2]
    GH = gdn_A_log.shape[-1]
    GW = GH * GDN_HEAD
    RH = rwkv_r_k.shape[1]
    RW = RH * RWKV_HEAD
    RP = RW // LANE
    DL, AL, GL = rwkv_w_up.shape[2], rwkv_a_up.shape[2], rwkv_g_up.shape[1]
    QR = mla_q_norm_w.shape[1]
    KR = mla_kv_norm_w.shape[1]
    MH = mla_kv_up.shape[2] // (MLA_NOPE + MLA_V)
    NG = router_group_w.shape[2]
    NE = router_expert_w.shape[2]
    assert DL + AL == LANE and GL == LANE and 4 * GH <= LANE and NG + NE <= LANE and 1 + DB <= SUB
    tok = _Tok(B, SEQ, DB, DSEQ)
    T = tok.t
    tm = _pick((256, 128, 64), SEQ, DSEQ)
    tm_moe = 256

    gdn_cols = 4 * GW + 4 * GH
    rwkv_cols = 3 * RW + DL + AL + GL
    o_r = gdn_cols
    o_m = gdn_cols + rwkv_cols

    x = jnp.concatenate([x_prompt.reshape(B * SEQ, D), x_sample.reshape(DB * DSEQ, D)], axis=0)
    cvec = jnp.zeros((SUB, D), F32).at[0].set(c_ctx).at[1:1 + DB].set(c)
    mod = _ada_mod(cvec, ada_w, ada_b)
    cos, sin = _rope_tables(tok)

    new_gdn, new_rwkv, new_ckv, new_kr = [], [], [], []
    for l in range(L):
        mod3 = mod[l].reshape(SUB, 1, 6 * D)
        wl = w_in[l]
        w_gdn = _b(wl[:, :4 * GW])
        w_rwkv = _b(wl[:, o_r:o_r + 3 * RW])
        w_cq = _b(wl[:, o_m:o_m + QR])
        w_ckv = _b(wl[:, o_m + QR:o_m + QR + KR])
        w_kr = wl[:, o_m + QR + KR:o_m + QR + KR + MLA_ROPE]
        w_small = _b(jnp.concatenate([
            _pad_cols(wl[:, 4 * GW:4 * GW + 4 * GH], LANE),
            wl[:, o_r + 3 * RW:o_r + 3 * RW + DL + AL],
            wl[:, o_r + 3 * RW + DL + AL:o_r + rwkv_cols],
            w_kr, _rope_rot_cols(w_kr)], axis=1))

        h1 = _norm_mod(x, norm_mix_w[l], mod3, 0, tok, tm)
        p_gdn = _matmul(h1, w_gdn, name="proj_gdn")
        p_rwkv = _matmul(h1, w_rwkv, name="proj_rwkv")
        p_cq = _matmul(h1, w_cq, name="proj_cq")
        p_ckv = _matmul(h1, w_ckv, name="proj_ckv")
        p_small = _matmul(h1, w_small, name="proj_small")

        gate_par = jnp.zeros((SUB, LANE), F32)
        gate_par = gate_par.at[0, :2 * GH].set(-jnp.exp(gdn_A_log[l].reshape(-1)))
        gate_par = gate_par.at[1, :2 * GH].set(gdn_dt_bias[l].reshape(-1))
        qkv, gates = _gdn_pre(p_gdn, p_small, gdn_conv_w[l].T, gate_par, tok, GH, tm)
        gcol = gates[:, :4 * GH].reshape(T // CHUNK, CHUNK, 4 * GH)
        grow = jnp.swapaxes(gates[:, :2 * GH].reshape(T // CHUNK, CHUNK, 2 * GH), 1, 2).reshape(
            T // CHUNK, GH, 2 * CHUNK)
        oc_f, sc_f = _gdn_chunk(qkv, gcol, grow, 0, B, SEQ, GH, False, emit=True)
        oc_b, sc_b = _gdn_chunk(qkv, gcol, grow, 0, B, SEQ, GH, True, emit=True)
        ol_f, = _gdn_chunk(qkv, gcol, grow, tok.t_ctx, DB, DSEQ, GH, False, s0=state_gdn, s0_idx=(l, 0))
        ol_b, = _gdn_chunk(qkv, gcol, grow, tok.t_ctx, DB, DSEQ, GH, True, s0=state_gdn, s0_idx=(l, 1))
        o_gdn = _gdn_out(jnp.concatenate([oc_f, ol_f], axis=0), jnp.concatenate([oc_b, ol_b], axis=0),
                         p_gdn, gdn_norm_w[l], GH, tm)
        new_gdn.append(jnp.stack([sc_f, sc_b], axis=1))

        ln = (rwkv_ln_w[l].reshape(1, RW), rwkv_ln_b[l].reshape(1, RW), rwkv_r_k[l].reshape(1, RW))
        y_dir, s_dir = [], []
        for d in range(2):
            mu = rwkv_mu[l, d]
            wup = jnp.concatenate([rwkv_w_up[l, d], jnp.zeros((AL, RW), F32)], axis=0)
            aup = jnp.concatenate([jnp.zeros((DL, RW), F32), rwkv_a_up[l, d]], axis=0)
            par = (mu[:RW].reshape(1, RW), mu[RW:2 * RW].reshape(1, RW), mu[2 * RW:3 * RW].reshape(1, RW),
                   mu[3 * RW:].reshape(1, LANE), rwkv_w0[l, d].reshape(1, RW), wup,
                   rwkv_a0[l, d].reshape(1, RW), aup, rwkv_k_k[l].reshape(1, RW), rwkv_k_a[l].reshape(1, RW))
            pre = _rwkv_pre(p_rwkv, p_small, d, par, tok, tm)
            s0 = state_rwkv[:, l, d].reshape(DB, RP, 2, RWKV_HEAD, RWKV_HEAD)
            s0 = jnp.einsum('bpaij,ac->bpaicj', s0, jnp.eye(2, dtype=F32)).reshape(DB, RP, LANE, LANE)
            yc, sc = _rwkv_chunk(pre, ln, 0, B, SEQ, d == 1, emit=True)
            yl, = _rwkv_chunk(pre, ln, tok.t_ctx, DB, DSEQ, d == 1, s0=s0)
            y_dir.append(jnp.concatenate([yc, yl], axis=0))
            sc = sc.reshape(B, RP, 2, RWKV_HEAD, 2, RWKV_HEAD)
            s_dir.append(jnp.einsum('bpaiaj->bpaij', sc).reshape(B, RH, RWKV_HEAD, RWKV_HEAD))
        o_rwkv = _rwkv_out(y_dir[0], y_dir[1], p_small, rwkv_g_up[l], tm)
        new_rwkv.append(jnp.stack(s_dir, axis=1))

        cqn, ckvn, krf = _mla_pre(p_cq, p_ckv, p_small, cos, sin, mla_q_norm_w[l], mla_kv_norm_w[l], tm)
        qu = mla_q_up[l].reshape(QR, MH, MLA_QK)
        q_rope = qu[:, :, MLA_NOPE:]
        w_q = _b(jnp.concatenate([qu[:, :, :MLA_NOPE].reshape(QR, MH * MLA_NOPE),
                                  jnp.concatenate([q_rope, _rope_rot_cols(q_rope)], axis=-1).reshape(QR, MH * LANE)],
                                 axis=1))
        q = _matmul(cqn, w_q, name="q_up")
        kr_cache = _b(jnp.pad(cache_mla_krope[:, l], ((0, 0), (0, 0), (0, LANE - MLA_ROPE))))
        ckv_parts, kr_parts = [], []
        for b in range(DB):
            r0 = tok.t_ctx + b * DSEQ
            ckv_parts += [_b(cache_mla_ckv[b, l]), _b(ckvn[r0:r0 + DSEQ])]
            kr_parts += [kr_cache[b], krf[r0:r0 + DSEQ]]
        ckv_all = jnp.concatenate(ckv_parts + [_b(ckvn[:tok.t_ctx])], axis=0)
        kr_all = jnp.concatenate(kr_parts + [krf[:tok.t_ctx]], axis=0)
        kv = _matmul(ckv_all, _b(mla_kv_up[l]), out_dtype=BF16, name="kv_up")
        o_c = _attention(q, cos, sin, kv, kr_all, 0, DB * (PAST + DSEQ), B, SEQ, SEQ, MH)
        o_l = _attention(q, cos, sin, kv, kr_all, tok.t_ctx, 0, DB, DSEQ, PAST + DSEQ, MH)
        o_mla = jnp.concatenate([o_c, o_l], axis=0)
        new_ckv.append(ckvn[:tok.t_ctx].reshape(B, SEQ, KR))
        new_kr.append(p_small[:tok.t_ctx, 3 * LANE:3 * LANE + MLA_ROPE].reshape(B, SEQ, MLA_ROPE))

        x = _wout_residual(x, mod3, o_gdn, o_rwkv, o_mla, _b(w_out[l]), tok)

        rw = _pad_cols(jnp.concatenate([router_group_w[l], router_expert_w[l]], axis=1), LANE)
        rb = _pad_cols(jnp.concatenate([router_group_b[l], router_expert_b[l]])[None, :], LANE)
        h2, logits = _norm_mod(x, norm_ffn_w[l], mod3, 3, tok, tm, router=(rw, rb))
        slot_tok, gate, block_e, n_used, p0, p1 = _route(logits, NG, NE, tm_moe)
        ys = _experts(h2, slot_tok, block_e, n_used, moe_w_gate, moe_w_up, moe_w_down, l, tm_moe)
        x = _combine(x, mod3, ys, p0, p1, gate, tok, tm)

    y = _final_norm(x, final_norm_w, tm)
    y_prompt = y[:tok.t_ctx].reshape(B, SEQ, D)
    y_sample = y[tok.t_ctx:].reshape(DB, DSEQ, D)
    return (y_prompt, y_sample, jnp.stack(new_gdn, axis=1), jnp.stack(new_rwkv, axis=1),
            jnp.stack(new_ckv, axis=1), jnp.stack(new_kr, axis=1))
```

```python
import functools

import numpy as np
import jax
import jax.numpy as jnp
from jax import lax
from jax.experimental import pallas as pl
from jax.experimental.pallas import tpu as pltpu

F32 = jnp.float32
BF16 = jnp.bfloat16
HI = lax.Precision.HIGHEST

EPS = 1e-6
L2_EPS = 1e-6
GRID_W = 64
ROPE_THETA = 10000.0
GDN_HEAD = 128
RWKV_HEAD = 64
RWKV_GN_EPS = RWKV_HEAD * 1e-5
MLA_V = 128
MLA_NOPE = 128
MLA_ROPE = 64
MLA_QK = MLA_NOPE + MLA_ROPE
TOP_K = 2
CHUNK = 64
LANE = 128
SUB = 8
DMA_UNROLL = 8
GDN_SEQ_PER_STEP = 4
RWKV_SEQ_PER_STEP = 2
VMEM_LIMIT = 52 * 1024 * 1024


def _cp(*sem, vmem=VMEM_LIMIT):
    return pltpu.CompilerParams(dimension_semantics=sem, vmem_limit_bytes=vmem)


def _pick(prefs, *ns):
    for p in prefs:
        if all(n % p == 0 for n in ns):
            return p
    raise ValueError(f"no tile in {prefs} divides {ns}")


def _dot(a, b, prec=None):
    return lax.dot_general(a, b, (((1,), (0,)), ((), ())), precision=prec, preferred_element_type=F32)


def _dot_nt(a, b, prec=None):
    return lax.dot_general(a, b, (((1,), (1,)), ((), ())), precision=prec, preferred_element_type=F32)


def _dot_tn(a, b, prec=None):
    return lax.dot_general(a, b, (((0,), (0,)), ((), ())), precision=prec, preferred_element_type=F32)


def _b(x):
    return x.astype(BF16)


def _sigmoid(x):
    return 1.0 / (1.0 + jnp.exp(-x))


def _silu(x):
    return x * _sigmoid(x)


def _softplus(x):
    return jnp.maximum(x, 0.0) + jnp.log(1.0 + jnp.exp(-jnp.abs(x)))


def _ada_kernel(c_ref, w_ref, b_ref, o_ref):
    s = _silu(c_ref[...])
    o_ref[...] = _dot(_b(s), _b(w_ref[...])) + b_ref[...]


def _ada_mod(cvec, ada_w, ada_b):
    L, D, N = ada_w.shape
    R = cvec.shape[0]
    tn = _pick((1024, 512, 256, 128), N)
    return pl.pallas_call(
        _ada_kernel,
        grid=(L, N // tn),
        in_specs=[pl.BlockSpec((R, D), lambda l, j: (0, 0)),
                  pl.BlockSpec((None, D, tn), lambda l, j: (l, 0, j)),
                  pl.BlockSpec((None, 1, tn), lambda l, j: (l, 0, j))],
        out_specs=pl.BlockSpec((None, R, tn), lambda l, j: (l, 0, j)),
        out_shape=jax.ShapeDtypeStruct((L, R, N), F32),
        compiler_params=_cp("parallel", "parallel"),
        name="ada_mod",
    )(cvec, ada_w, ada_b.reshape(L, 1, N))


def _norm_kernel(x_ref, w_ref, o_ref):
    x = x_ref[...]
    y = x * lax.rsqrt(jnp.mean(x * x, axis=-1, keepdims=True) + EPS) * w_ref[...]
    o_ref[...] = y.astype(o_ref.dtype)


def _norm_mod_kernel(x_ref, w_ref, sc_ref, sh_ref, o_ref):
    x = x_ref[...]
    y = x * lax.rsqrt(jnp.mean(x * x, axis=-1, keepdims=True) + EPS) * w_ref[...]
    o_ref[...] = (y * (1.0 + sc_ref[...]) + sh_ref[...]).astype(o_ref.dtype)


def _pack_bf16_halves(h):
    half = h.shape[1] // 2
    a = lax.bitcast_convert_type(h[:, :half].astype(BF16).astype(F32), jnp.uint32)
    b = lax.bitcast_convert_type(h[:, half:].astype(BF16).astype(F32), jnp.uint32)
    return a | (b >> 16)


def _unpack_bf16_halves(w):
    a = lax.bitcast_convert_type(w & jnp.uint32(0xFFFF0000), F32).astype(BF16)
    b = lax.bitcast_convert_type(w << 16, F32).astype(BF16)
    return a, b


def _norm_router_kernel(x_ref, w_ref, sc_ref, sh_ref, rw_ref, rb_ref, o_ref, lg_ref):
    x = x_ref[...]
    y = x * lax.rsqrt(jnp.mean(x * x, axis=-1, keepdims=True) + EPS) * w_ref[...]
    h = y * (1.0 + sc_ref[...]) + sh_ref[...]
    o_ref[...] = _pack_bf16_halves(h)
    lg_ref[...] = _dot(h, rw_ref[...], HI) + rb_ref[...]


class _Tok:
    def __init__(self, n_ctx_seq, seq, n_lat_seq, dec_seq):
        self.n_ctx_seq, self.seq, self.n_lat_seq, self.dec_seq = n_ctx_seq, seq, n_lat_seq, dec_seq
        self.t_ctx = n_ctx_seq * seq
        self.t_lat = n_lat_seq * dec_seq
        self.t = self.t_ctx + self.t_lat

    def group_of_tile(self, tm):
        assert self.t_ctx % tm == 0 and self.dec_seq % tm == 0
        nct, tpl = self.t_ctx // tm, self.dec_seq // tm
        return lambda i: jnp.where(i < nct, 0, 1 + (i - nct) // tpl)

    def seq_flags(self, tm):
        assert self.seq % tm == 0 and self.dec_seq % tm == 0
        starts = np.arange(0, self.t, tm)
        first = np.where(starts < self.t_ctx, starts % self.seq == 0, (starts - self.t_ctx) % self.dec_seq == 0)
        ends = starts + tm
        last = np.where(starts < self.t_ctx, ends % self.seq == 0, (ends - self.t_ctx) % self.dec_seq == 0)
        return jnp.asarray(first, jnp.int32), jnp.asarray(last, jnp.int32)


def _norm_mod(x, w, mod3, seg, tok, tm, router=None):
    T, D = x.shape
    grp = tok.group_of_tile(tm)
    in_specs = [pl.BlockSpec((tm, D), lambda i: (i, 0)),
                pl.BlockSpec((1, D), lambda i: (0, 0)),
                pl.BlockSpec((None, 1, D), lambda i: (grp(i), 0, seg + 1)),
                pl.BlockSpec((None, 1, D), lambda i: (grp(i), 0, seg))]
    if router is None:
        return pl.pallas_call(
            _norm_mod_kernel, grid=(T // tm,), in_specs=in_specs,
            out_specs=pl.BlockSpec((tm, D), lambda i: (i, 0)),
            out_shape=jax.ShapeDtypeStruct((T, D), BF16),
            compiler_params=_cp("parallel"), name="norm_mod",
        )(x, w.reshape(1, D), mod3, mod3)
    rw, rb = router
    NR = rw.shape[1]
    return pl.pallas_call(
        _norm_router_kernel, grid=(T // tm,),
        in_specs=in_specs + [pl.BlockSpec((D, NR), lambda i: (0, 0)), pl.BlockSpec((1, NR), lambda i: (0, 0))],
        out_specs=[pl.BlockSpec((tm, D // 2), lambda i: (i, 0)), pl.BlockSpec((tm, NR), lambda i: (i, 0))],
        out_shape=[jax.ShapeDtypeStruct((T, D // 2), jnp.uint32), jax.ShapeDtypeStruct((T, NR), F32)],
        compiler_params=_cp("parallel"), name="norm_router",
    )(x, w.reshape(1, D), mod3, mod3, rw, rb)


def _final_norm(x, w, tm):
    T, D = x.shape
    return pl.pallas_call(
        _norm_kernel, grid=(T // tm,),
        in_specs=[pl.BlockSpec((tm, D), lambda i: (i, 0)), pl.BlockSpec((1, D), lambda i: (0, 0))],
        out_specs=pl.BlockSpec((tm, D), lambda i: (i, 0)),
        out_shape=jax.ShapeDtypeStruct((T, D), F32),
        compiler_params=_cp("parallel"), name="final_norm",
    )(x, w.reshape(1, D))


def _mm_kernel(a_ref, b_ref, o_ref):
    o_ref[...] = _dot(a_ref[...], b_ref[...]).astype(o_ref.dtype)


def _matmul(a, b, out_dtype=F32, name="matmul"):
    M, K = a.shape
    N = b.shape[1]
    tm = _pick((1024, 512, 256, 128, 64, 8), M)
    tn = _pick((1024, 512, 256, 128), N)
    return pl.pallas_call(
        _mm_kernel, grid=(N // tn, M // tm),
        in_specs=[pl.BlockSpec((tm, K), lambda j, i: (i, 0)), pl.BlockSpec((K, tn), lambda j, i: (0, j))],
        out_specs=pl.BlockSpec((tm, tn), lambda j, i: (i, j)),
        out_shape=jax.ShapeDtypeStruct((M, N), out_dtype),
        compiler_params=_cp("parallel", "parallel"), name=name,
    )(a, b)


def _wout_kernel(x_ref, g_ref, a1_ref, a2_ref, a3_ref, w1_ref, w2_ref, w3_ref, o_ref):
    mix = _dot(a1_ref[...], w1_ref[...]) + _dot(a2_ref[...], w2_ref[...]) + _dot(a3_ref[...], w3_ref[...])
    o_ref[...] = x_ref[...] + g_ref[...] * mix


def _wout_residual(x, mod3, a1, a2, a3, w, tok):
    T, D = x.shape
    k1, k2, k3 = a1.shape[1], a2.shape[1], a3.shape[1]
    assert k1 % k2 == 0 and (k1 + k2) % k3 == 0
    tm = _pick((512, 256, 128), tok.t_ctx, tok.dec_seq)
    tn = _pick((1024, 512, 256, 128), D)
    grp = tok.group_of_tile(tm)
    nj = D // tn
    return pl.pallas_call(
        _wout_kernel, grid=(nj, T // tm),
        in_specs=[pl.BlockSpec((tm, tn), lambda j, i: (i, j)),
                  pl.BlockSpec((None, 1, tn), lambda j, i: (grp(i), 0, 2 * nj + j)),
                  pl.BlockSpec((tm, k1), lambda j, i: (i, 0)),
                  pl.BlockSpec((tm, k2), lambda j, i: (i, 0)),
                  pl.BlockSpec((tm, k3), lambda j, i: (i, 0)),
                  pl.BlockSpec((k1, tn), lambda j, i: (0, j)),
                  pl.BlockSpec((k2, tn), lambda j, i: (k1 // k2, j)),
                  pl.BlockSpec((k3, tn), lambda j, i: ((k1 + k2) // k3, j))],
        out_specs=pl.BlockSpec((tm, tn), lambda j, i: (i, j)),
        out_shape=jax.ShapeDtypeStruct((T, D), F32),
        compiler_params=_cp("parallel", "parallel"), name="wout_residual",
    )(x, mod3, a1, a2, a3, w, w, w)


def _split_bf16(x, terms):
    out = []
    for _ in range(terms):
        p = x.astype(BF16)
        out.append(p)
        x = x - p.astype(F32)
    return out


def _dot_exact_r(a01, x, terms=3):
    return sum(_dot(a01, p) for p in _split_bf16(x, terms))


def _dot_exact_l(x, b01, terms=3):
    return sum(_dot(p, b01) for p in _split_bf16(x, terms))


def _dot_nt_exact_l(x, b01, terms=3):
    return sum(_dot_nt(p, b01) for p in _split_bf16(x, terms))


def _one_bf16(mask):
    return jnp.where(mask, 1.0, 0.0).astype(BF16)


def _pair_masks(C, rev):
    M = 2 * C
    ti = lax.broadcasted_iota(jnp.int32, (M, M), 0)
    si = lax.broadcasted_iota(jnp.int32, (M, M), 1)
    shift = C.bit_length() - 1
    assert (1 << shift) == C
    same = jnp.right_shift(ti, shift) == jnp.right_shift(si, shift)
    before = (si > ti) if rev else (si < ti)
    strict = same & before
    incl = same & (before | (ti == si))
    tc = lax.broadcasted_iota(jnp.int32, (C, C), 0)
    sc = lax.broadcasted_iota(jnp.int32, (C, C), 1)
    cum = _one_bf16((sc >= tc) if rev else (sc <= tc))
    return ti, si, shift, same, strict, incl, cum


def _tri_inv_n(a_list, ti, si, top_shift):
    def same(shift):
        return jnp.right_shift(ti, shift) == jnp.right_shift(si, shift)

    n_mat = range(len(a_list))
    m3 = same(3)
    ad = [jnp.where(m3, a, 0.0) for a in a_list]
    adb = [_b(x) for x in ad]
    b2 = [_dot(x, x) for x in adb]
    b2b = [_b(x) for x in b2]
    n1 = [b2[i] - ad[i] - _dot(adb[i], b2b[i]) for i in n_mat]
    b4 = [_dot(x, x) for x in b2b]
    n = [n1[i] + b4[i] + _dot(_b(n1[i]), _b(b4[i])) for i in n_mat]
    for shift in range(3, top_shift):
        moff = same(shift + 1) & jnp.logical_not(same(shift))
        aoff = [jnp.where(moff, a, 0.0) for a in a_list]
        aob = [_b(x) for x in aoff]
        nb = [_b(x) for x in n]
        x = [aoff[i] + _dot(nb[i], aob[i]) for i in n_mat]
        n = [n[i] - x[i] - _dot(_b(x[i]), nb[i]) for i in n_mat]
    return n


def _shift_rows(x, halo_prev_ref, halo_next_ref, first, last, direction):
    tm = x.shape[0]
    rows = lax.broadcasted_iota(jnp.int32, (tm, 1), 0)
    if direction == 0:
        edge = halo_prev_ref[SUB - 1:SUB, :] * (1.0 - first)
        return jnp.where(rows == 0, edge, pltpu.roll(x, 1, 0))
    edge = halo_next_ref[0:1, :] * (1.0 - last)
    return jnp.where(rows == tm - 1, edge, pltpu.roll(x, tm - 1, 0))


def _halo_specs(tm, width, col, n_rows):
    r = tm // SUB
    nb = n_rows // SUB
    prev = pl.BlockSpec((SUB, width), lambda i, f, l: (jnp.maximum(i * r - 1, 0), col))
    nxt = pl.BlockSpec((SUB, width), lambda i, f, l: (jnp.minimum((i + 1) * r, nb - 1), col))
    return prev, nxt


def _gdn_pre_kernel(n_heads, first_ref, last_ref, x_ref, xp_ref, xn_ref, cw_ref, ab_ref, gp_ref,
                    qkv_ref, gates_ref):
    i = pl.program_id(0)
    first = first_ref[i].astype(F32)
    last = last_ref[i].astype(F32)
    x = x_ref[...]
    xm = _shift_rows(x, xp_ref, xn_ref, first, last, 0)
    xq = _shift_rows(x, xp_ref, xn_ref, first, last, 1)
    y = _silu(xm * cw_ref[0:1, :] + x * cw_ref[1:2, :] + xq * cw_ref[2:3, :])
    for hd in range(3 * n_heads):
        seg = y[:, hd * GDN_HEAD:(hd + 1) * GDN_HEAD]
        if hd < 2 * n_heads:
            seg = seg * lax.rsqrt(jnp.sum(seg * seg, axis=-1, keepdims=True) + L2_EPS)
            if hd < n_heads:
                seg = seg * (GDN_HEAD ** -0.5)
        qkv_ref[:, hd * GDN_HEAD:(hd + 1) * GDN_HEAD] = seg
    ab = ab_ref[...]
    lane = lax.broadcasted_iota(jnp.int32, ab.shape, 1)
    g = gp_ref[0:1, :] * _softplus(ab + gp_ref[1:2, :])
    gates_ref[...] = jnp.where(lane < 2 * n_heads, g, _sigmoid(ab))


def _gdn_pre(p_gdn, p_small, conv_w3, gate_par, tok, n_heads, tm):
    T = p_gdn.shape[0]
    W3 = 3 * n_heads * GDN_HEAD
    first, last = tok.seq_flags(tm)
    hp, hn = _halo_specs(tm, W3, 0, T)
    grid_spec = pltpu.PrefetchScalarGridSpec(
        num_scalar_prefetch=2, grid=(T // tm,),
        in_specs=[pl.BlockSpec((tm, W3), lambda i, f, l: (i, 0)), hp, hn,
                  pl.BlockSpec((3, W3), lambda i, f, l: (0, 0)),
                  pl.BlockSpec((tm, LANE), lambda i, f, l: (i, 0)),
                  pl.BlockSpec((SUB, LANE), lambda i, f, l: (0, 0))],
        out_specs=[pl.BlockSpec((tm, W3), lambda i, f, l: (i, 0)),
                   pl.BlockSpec((tm, LANE), lambda i, f, l: (i, 0))])
    return pl.pallas_call(
        functools.partial(_gdn_pre_kernel, n_heads), grid_spec=grid_spec,
        out_shape=[jax.ShapeDtypeStruct((T, W3), F32), jax.ShapeDtypeStruct((T, LANE), F32)],
        compiler_params=_cp("parallel"), name="gdn_pre",
    )(first, last, p_gdn, p_gdn, p_gdn, conv_w3, p_small, gate_par)


def _gdn_chunk_kernel(n_heads, n_grp, rev, has_s0, emit, *refs):
    refs = list(refs)
    G = n_grp
    in_refs = [refs[5 * g:5 * g + 5] for g in range(G)]
    pos = 5 * G
    s0_ref = None
    if has_s0:
        s0_ref = refs[pos]
        pos += 1
    o_ref = refs[pos]
    pos += 1
    so_ref = None
    if emit:
        so_ref = refs[pos]
        pos += 1
    s_scr = refs[pos]

    n = pl.program_id(1)
    n_last = pl.num_programs(1) - 1
    C = in_refs[0][0].shape[0]
    H = n_heads
    d = 1 if rev else 0

    @pl.when(n == 0)
    def _():
        if has_s0:
            s_scr[...] = s0_ref[...]
        else:
            s_scr[...] = jnp.zeros(s_scr.shape, F32)

    M = 2 * C
    HP = H // 2
    assert H % 2 == 0 and M == LANE
    ti, si, shift, same, strict, incl, cum = _pair_masks(C, rev)
    incl_b = _one_bf16(incl)
    ncol = in_refs[0][3].shape[1]
    er = lax.broadcasted_iota(jnp.int32, (ncol, 2 * H * LANE), 0)
    ec = lax.broadcasted_iota(jnp.int32, (ncol, 2 * H * LANE), 1) // LANE
    want = jnp.where(ec < H, d * H + ec, 2 * H + d * H + (ec - H))
    expand = _one_bf16(er == want)
    last_row = 0 if rev else C - 1

    def sl(h):
        return slice(h * GDN_HEAD, (h + 1) * GDN_HEAD)

    def pair_rows(x, p):
        return jnp.concatenate([x[:, sl(2 * p)], x[:, sl(2 * p + 1)]], axis=0)

    arr = []
    for g in range(G):
        q_ref, k_ref, v_ref, gcol_ref, grow_ref = in_refs[g]
        gb_all = _dot_exact_l(gcol_ref[...], expand)
        gcb = _dot_exact_r(cum, gb_all[:, :H * LANE])
        beta = gb_all[:, H * LANE:]
        gc_rows = _dot_nt_exact_l(grow_ref[d * HP:(d + 1) * HP, :], incl_b)
        q = q_ref[...]
        k = k_ref[...]
        egc = jnp.exp(gcb)
        gl = gcb[last_row:last_row + 1, :]
        kb = k * beta
        arr.append(dict(q=q, k=k, gcb=gcb, gc_rows=gc_rows, kb=kb, vbeta=v_ref[...] * beta, kbe=kb * egc,
                        qd=q * egc, kd=k * jnp.exp(gl - gcb), sdec=jnp.exp(gl)))

    pair_units = [(g, p) for g in range(G) for p in range(HP)]
    head_units = [(g, h) for g in range(G) for h in range(H)]
    pu = range(len(pair_units))

    def pidx(g, p):
        return g * HP + p

    k_st = [_b(pair_rows(arr[g]['k'], p)) for g, p in pair_units]
    lhs = [_b(jnp.concatenate([pair_rows(arr[g]['kb'], p), pair_rows(arr[g]['q'], p)], axis=0)) for g, p in pair_units]
    sc = [_dot_nt(lhs[u], k_st[u]) for u in pu]
    decay = [jnp.exp(jnp.where(incl, pair_rows(arr[g]['gcb'], p)
                               - jnp.broadcast_to(arr[g]['gc_rows'][p:p + 1, :], (M, M)), -jnp.inf))
             for g, p in pair_units]
    a_mat = [jnp.where(strict, sc[u][:M] * decay[u], 0.0) for u in pu]
    attn_b = [_b(sc[u][M:] * decay[u]) for u in pu]
    n_inv = _tri_inv_n(a_mat, ti, si, shift)
    rhs = [jnp.concatenate([pair_rows(arr[g]['vbeta'], p), pair_rows(arr[g]['kbe'], p)], axis=1)
           for g, p in pair_units]
    sol = [rhs[u] + _dot(_b(n_inv[u]), _b(rhs[u])) for u in pu]
    s_old = [s_scr[g, h] for g, h in head_units]
    s_b = [_b(s) for s in s_old]
    ws = [_dot(_b(jnp.concatenate([sol[pidx(g, h // 2)][(h % 2) * C:(h % 2 + 1) * C, GDN_HEAD:],
                                   arr[g]['qd'][:, sl(h)]], axis=0)), s_b[g * H + h])
          for g, h in head_units]
    v_new = [sol[u][:, :GDN_HEAD] - jnp.concatenate([ws[g * H + 2 * p][:C], ws[g * H + 2 * p + 1][:C]], axis=0)
             for u, (g, p) in enumerate(pair_units)]
    vn_b = [_b(x) for x in v_new]
    o = [jnp.concatenate([ws[g * H + 2 * p][C:], ws[g * H + 2 * p + 1][C:]], axis=0) + _dot(attn_b[u], vn_b[u])
         for u, (g, p) in enumerate(pair_units)]
    for u, (g, p) in enumerate(pair_units):
        o_ref[g, :, sl(2 * p)] = o[u][:C]
        o_ref[g, :, sl(2 * p + 1)] = o[u][C:]
    for g, h in head_units:
        upd = _dot_tn(_b(arr[g]['kd'][:, sl(h)]), vn_b[pidx(g, h // 2)][(h % 2) * C:(h % 2 + 1) * C])
        s_scr[g, h] = s_old[g * H + h] * arr[g]['sdec'][:, sl(h)] + upd

    if emit:
        @pl.when(n == n_last)
        def _():
            so_ref[...] = s_scr[...]


def _gdn_chunk(qkv, gcol, grow, row0, n_seq, seq_len, n_heads, rev, s0=None, s0_idx=None, emit=False):
    T = qkv.shape[0]
    C = CHUNK
    N = seq_len // C
    H = n_heads
    W = H * GDN_HEAD
    b0 = row0 // C

    G = min(GDN_SEQ_PER_STEP, n_seq)
    assert n_seq % G == 0

    def chunk(n):
        return (N - 1 - n) if rev else n

    def blk(s, n, g):
        return b0 + (s * G + g) * N + chunk(n)

    in_specs, args = [], []
    for g in range(G):
        in_specs += [pl.BlockSpec((C, W), lambda s, n, g=g: (blk(s, n, g), 0)),
                     pl.BlockSpec((C, W), lambda s, n, g=g: (blk(s, n, g), 1)),
                     pl.BlockSpec((C, W), lambda s, n, g=g: (blk(s, n, g), 2)),
                     pl.BlockSpec((None, C, gcol.shape[2]), lambda s, n, g=g: (blk(s, n, g), 0, 0)),
                     pl.BlockSpec((None, grow.shape[1], grow.shape[2]), lambda s, n, g=g: (blk(s, n, g), 0, 0))]
        args += [qkv, qkv, qkv, gcol, grow]
    if s0 is not None:
        l, d = s0_idx
        in_specs.append(pl.BlockSpec((G, None, None, H, GDN_HEAD, GDN_HEAD), lambda s, n: (s, l, d, 0, 0, 0)))
        args.append(s0)
    out_specs = [pl.BlockSpec((G, C, W), lambda s, n: (s, chunk(n), 0))]
    out_shape = [jax.ShapeDtypeStruct((n_seq, seq_len, W), F32)]
    if emit:
        out_specs.append(pl.BlockSpec((G, H, GDN_HEAD, GDN_HEAD), lambda s, n: (s, 0, 0, 0)))
        out_shape.append(jax.ShapeDtypeStruct((n_seq, H, GDN_HEAD, GDN_HEAD), F32))
    res = pl.pallas_call(
        functools.partial(_gdn_chunk_kernel, H, G, rev, s0 is not None, emit),
        grid=(n_seq // G, N), in_specs=in_specs, out_specs=out_specs, out_shape=out_shape,
        scratch_shapes=[pltpu.VMEM((G, H, GDN_HEAD, GDN_HEAD), F32)],
        compiler_params=_cp("parallel", "arbitrary"), name="gdn_chunk",
    )(*args)
    res = list(res)
    res[0] = res[0].reshape(n_seq * seq_len, W)
    return res


def _gdn_out_kernel(n_heads, of_ref, ob_ref, z_ref, nw_ref, o_ref):
    for h in range(n_heads):
        sl = slice(h * GDN_HEAD, (h + 1) * GDN_HEAD)
        o = of_ref[:, sl] + ob_ref[:, sl]
        y = o * lax.rsqrt(jnp.mean(o * o, axis=-1, keepdims=True) + EPS) * nw_ref[...]
        o_ref[:, sl] = (y * _silu(z_ref[:, sl])).astype(o_ref.dtype)


def _gdn_out(o_f, o_b, p_gdn, norm_w, n_heads, tm):
    T, W = o_f.shape
    return pl.pallas_call(
        functools.partial(_gdn_out_kernel, n_heads), grid=(T // tm,),
        in_specs=[pl.BlockSpec((tm, W), lambda i: (i, 0)), pl.BlockSpec((tm, W), lambda i: (i, 0)),
                  pl.BlockSpec((tm, W), lambda i: (i, 3)), pl.BlockSpec((1, GDN_HEAD), lambda i: (0, 0))],
        out_specs=pl.BlockSpec((tm, W), lambda i: (i, 0)),
        out_shape=jax.ShapeDtypeStruct((T, W), BF16),
        compiler_params=_cp("parallel"), name="gdn_out",
    )(o_f, o_b, p_gdn, norm_w.reshape(1, GDN_HEAD))


def _pair_ones():
    r = lax.broadcasted_iota(jnp.int32, (LANE, LANE), 0) // RWKV_HEAD
    c = lax.broadcasted_iota(jnp.int32, (LANE, LANE), 1) // RWKV_HEAD
    return r == c


def _rwkv_features(rev, at_edge, in_ref, par_refs, ones_b):
    r_ref, k_ref, v_ref, f_ref, rh_ref, kh_ref, vh_ref, fh_ref = in_ref
    mu_r_ref, mu_k_ref, mu_v_ref, mu_f_ref, w0_ref, wup_ref, a0_ref, aup_ref, kkw_ref, kaw_ref = par_refs
    C = r_ref.shape[0]
    rows = lax.broadcasted_iota(jnp.int32, (C, 1), 0)
    keep = 1.0 - at_edge

    def mix(x_ref, h_ref, mu_ref):
        x = x_ref[...]
        if rev:
            sh = jnp.where(rows == C - 1, h_ref[0:1, :] * keep, pltpu.roll(x, C - 1, 0))
        else:
            sh = jnp.where(rows == 0, h_ref[SUB - 1:SUB, :] * keep, pltpu.roll(x, 1, 0))
        return x + (sh - x) * mu_ref[...]

    r = mix(r_ref, rh_ref, mu_r_ref)
    k = mix(k_ref, kh_ref, mu_k_ref)
    v = mix(v_ref, vh_ref, mu_v_ref)
    f = mix(f_ref, fh_ref, mu_f_ref)
    w_lin = w0_ref[...] + _dot(_b(jnp.tanh(f)), _b(wup_ref[...]))
    lw = -jnp.exp(-_softplus(-w_lin) - 0.5)
    a = _sigmoid(a0_ref[...] + _dot(_b(f), _b(aup_ref[...])))
    kraw = k * kkw_ref[...]
    sq = kraw * kraw
    ss = jnp.concatenate([_dot_exact_l(sq[:, p * LANE:(p + 1) * LANE], ones_b, 2)
                          for p in range(kraw.shape[1] // LANE)], axis=1)
    kk = kraw * lax.rsqrt(ss + L2_EPS)
    return r, k * (1.0 + (a - 1.0) * kaw_ref[...]), v, kk, kk * a, lw


def _rwkv_chunk_kernel(n_pairs, n_grp, rev, has_s0, emit, *refs):
    refs = list(refs)
    G = n_grp
    in_refs = [refs[8 * g:8 * g + 8] for g in range(G)]
    pos = 8 * G
    par_refs = refs[pos:pos + 10]
    pos += 10
    lnw_ref, lnb_ref, rk_ref = refs[pos:pos + 3]
    pos += 3
    s0_ref = None
    if has_s0:
        s0_ref = refs[pos]
        pos += 1
    y_ref = refs[pos]
    pos += 1
    so_ref = None
    if emit:
        so_ref = refs[pos]
        pos += 1
    s_scr = refs[pos]

    n = pl.program_id(1)
    n_last = pl.num_programs(1) - 1
    C = in_refs[0][0].shape[0]

    @pl.when(n == 0)
    def _():
        if has_s0:
            s_scr[...] = s0_ref[...]
        else:
            s_scr[...] = jnp.zeros(s_scr.shape, F32)

    M = 2 * C
    assert M == LANE and RWKV_HEAD == C
    ti, si, shift, same, strict, incl, cum = _pair_masks(C, rev)
    ones_b = _one_bf16(same)
    last_row = 0 if rev else C - 1
    inv_n = 1.0 / RWKV_HEAD
    pairs = range(n_pairs)
    sls = [slice(p * LANE, (p + 1) * LANE) for p in pairs]

    def st2(x):
        return jnp.concatenate([x, x], axis=0)

    def fold(x):
        return x[:C] + x[C:]

    def seg_sum(x):
        return _dot_exact_l(x, ones_b, 2)

    at_edge = (n == 0).astype(F32)
    arr = []
    for g in range(G):
        r, k, v, kk, b, lw = _rwkv_features(rev, at_edge, in_refs[g], par_refs, ones_b)
        cl = _dot_exact_r(cum, lw)
        cll = cl[last_row:last_row + 1, :]
        e_neg = jnp.exp(-cl)
        e_end = jnp.exp(cll - cl)
        arr.append(dict(r=r, k=k, v=v, rtb=_b(r * jnp.exp(cl)), ktb=_b(k * e_neg), btb=_b(b * e_neg),
                        atb=_b(-kk * jnp.exp(cl - lw)), keb=_b(k * e_end), beb=_b(b * e_end),
                        vb=_b(v), s_dec=jnp.exp(cll)))

    units = [(g, p) for g in range(G) for p in pairs]
    un = range(len(units))

    def col(name, u):
        g, p = units[u]
        return arr[g][name][:, sls[p]]

    s_old = [s_scr[g, p] for g, p in units]
    s_b = [_b(s) for s in s_old]
    x0 = [_dot_nt(jnp.concatenate([col('atb', u), col('rtb', u)], axis=0), s_b[u]) for u in un]
    la = [jnp.concatenate([st2(col('atb', u)) * ones_b, st2(col('rtb', u)) * ones_b], axis=0) for u in un]
    sk = [_dot_nt(la[u], st2(col('ktb', u))) for u in un]
    sb = [_dot_nt(la[u], st2(col('btb', u))) for u in un]
    a_ak = [_b(jnp.where(strict, x[:M], 0.0)) for x in sk]
    r_k = [_b(jnp.where(incl, x[M:], 0.0)) for x in sk]
    a_neg = [jnp.where(strict, -x[:M], 0.0) for x in sb]
    r_b = [_b(jnp.where(incl, x[M:], 0.0)) for x in sb]
    n_inv = _tri_inv_n(a_neg, ti, si, shift)
    v_st = [st2(col('vb', u)) * ones_b for u in un]
    wm = [x0[u][:C] + fold(_dot(a_ak[u], v_st[u])) for u in un]
    uu = [wm[u] + fold(_dot(_b(n_inv[u]), st2(_b(wm[u])) * ones_b)) for u in un]
    u_b = [_b(x) for x in uu]
    u_st = [st2(x) * ones_b for x in u_b]
    y = [x0[u][C:] + fold(_dot(jnp.concatenate([r_k[u], r_b[u]], axis=1),
                               jnp.concatenate([v_st[u], u_st[u]], axis=0))) for u in un]
    upd = [_dot_tn(jnp.concatenate([col('vb', u), u_b[u]], axis=0),
                   jnp.concatenate([col('keb', u), col('beb', u)], axis=0)) for u in un]
    for u, (g, p) in enumerate(units):
        s_scr[g, p] = s_old[u] * col('s_dec', u) + jnp.where(same, upd[u], 0.0)
    for u, (g, p) in enumerate(units):
        sl = sls[p]
        mu = seg_sum(y[u]) * inv_n
        yc = y[u] - mu
        var = seg_sum(yc * yc) * inv_n
        yn = yc * lax.rsqrt(var + RWKV_GN_EPS) * lnw_ref[:, sl] + lnb_ref[:, sl]
        bonus = seg_sum(col('r', u) * col('k', u) * rk_ref[:, sl])
        y_ref[g, :, sl] = yn + bonus * col('v', u)

    if emit:
        @pl.when(n == n_last)
        def _():
            so_ref[...] = s_scr[...]


def _rwkv_chunk(p_rwkv, p_small, par, ln, row0, n_seq, seq_len, rev, s0=None, emit=False):
    T, W3 = p_rwkv.shape
    W = W3 // 3
    P = W // LANE
    C = CHUNK
    N = seq_len // C
    b0 = row0 // C
    cs = C // SUB
    n_sub = T // SUB

    G = min(RWKV_SEQ_PER_STEP, n_seq)
    assert n_seq % G == 0

    def chunk(n):
        return (N - 1 - n) if rev else n

    def blk(s, n, g):
        return b0 + (s * G + g) * N + chunk(n)

    def halo(s, n, g):
        if rev:
            return jnp.minimum((blk(s, n, g) + 1) * cs, n_sub - 1)
        return jnp.maximum(blk(s, n, g) * cs - 1, 0)

    in_specs, args = [], []
    for g in range(G):
        in_specs += [pl.BlockSpec((C, W), lambda s, n, g=g, c=c: (blk(s, n, g), c)) for c in range(3)]
        in_specs.append(pl.BlockSpec((C, LANE), lambda s, n, g=g: (blk(s, n, g), 1)))
        in_specs += [pl.BlockSpec((SUB, W), lambda s, n, g=g, c=c: (halo(s, n, g), c)) for c in range(3)]
        in_specs.append(pl.BlockSpec((SUB, LANE), lambda s, n, g=g: (halo(s, n, g), 1)))
        args += [p_rwkv, p_rwkv, p_rwkv, p_small, p_rwkv, p_rwkv, p_rwkv, p_small]
    row = lambda width: pl.BlockSpec((1, width), lambda s, n: (0, 0))
    full = lambda a, b: pl.BlockSpec((a, b), lambda s, n: (0, 0))
    in_specs += [row(W), row(W), row(W), row(LANE), row(W), full(LANE, W), row(W), full(LANE, W), row(W), row(W)]
    args += list(par)
    in_specs += [row(W) for _ in range(3)]
    args += list(ln)
    if s0 is not None:
        in_specs.append(pl.BlockSpec((G, P, LANE, LANE), lambda s, n: (s, 0, 0, 0)))
        args.append(s0)
    out_specs = [pl.BlockSpec((G, C, W), lambda s, n: (s, chunk(n), 0))]
    out_shape = [jax.ShapeDtypeStruct((n_seq, seq_len, W), F32)]
    if emit:
        out_specs.append(pl.BlockSpec((G, P, LANE, LANE), lambda s, n: (s, 0, 0, 0)))
        out_shape.append(jax.ShapeDtypeStruct((n_seq, P, LANE, LANE), F32))
    res = list(pl.pallas_call(
        functools.partial(_rwkv_chunk_kernel, P, G, rev, s0 is not None, emit),
        grid=(n_seq // G, N), in_specs=in_specs, out_specs=out_specs, out_shape=out_shape,
        scratch_shapes=[pltpu.VMEM((G, P, LANE, LANE), F32)],
        compiler_params=_cp("parallel", "arbitrary"), name="rwkv_chunk",
    )(*args))
    res[0] = res[0].reshape(n_seq * seq_len, W)
    return res


def _rwkv_out_kernel(yf_ref, yb_ref, fg_ref, gup_ref, o_ref):
    g = _dot(_b(_sigmoid(fg_ref[...])), _b(gup_ref[...]))
    o_ref[...] = ((yf_ref[...] + yb_ref[...]) * g).astype(o_ref.dtype)


def _rwkv_out(y_f, y_b, p_small, g_up, tm):
    T, W = y_f.shape
    return pl.pallas_call(
        _rwkv_out_kernel, grid=(T // tm,),
        in_specs=[pl.BlockSpec((tm, W), lambda i: (i, 0)), pl.BlockSpec((tm, W), lambda i: (i, 0)),
                  pl.BlockSpec((tm, LANE), lambda i: (i, 2)), pl.BlockSpec((LANE, W), lambda i: (0, 0))],
        out_specs=pl.BlockSpec((tm, W), lambda i: (i, 0)),
        out_shape=jax.ShapeDtypeStruct((T, W), BF16),
        compiler_params=_cp("parallel"), name="rwkv_out",
    )(y_f, y_b, p_small, g_up)


def _rope128(x, cos, sin):
    return x * cos + pltpu.roll(x, MLA_ROPE, 1) * sin


def _mla_pre_kernel(cq_ref, ckv_ref, kr_ref, cos_ref, sin_ref, qw_ref, kvw_ref, cqn_ref, ckvn_ref, krf_ref):
    cq = cq_ref[...]
    cqn_ref[...] = (cq * lax.rsqrt(jnp.mean(cq * cq, axis=-1, keepdims=True) + EPS) * qw_ref[...]).astype(BF16)
    ckv = ckv_ref[...]
    ckvn_ref[...] = ckv * lax.rsqrt(jnp.mean(ckv * ckv, axis=-1, keepdims=True) + EPS) * kvw_ref[...]
    krf_ref[...] = _rope128(kr_ref[...], cos_ref[...], sin_ref[...]).astype(BF16)


def _mla_pre(p_cq, p_ckv, p_small, cos, sin, qw, kvw, tm):
    T, QR = p_cq.shape
    KR = p_ckv.shape[1]
    return pl.pallas_call(
        _mla_pre_kernel, grid=(T // tm,),
        in_specs=[pl.BlockSpec((tm, QR), lambda i: (i, 0)), pl.BlockSpec((tm, KR), lambda i: (i, 0)),
                  pl.BlockSpec((tm, LANE), lambda i: (i, 3)),
                  pl.BlockSpec((tm, LANE), lambda i: (i, 0)), pl.BlockSpec((tm, LANE), lambda i: (i, 0)),
                  pl.BlockSpec((1, QR), lambda i: (0, 0)), pl.BlockSpec((1, KR), lambda i: (0, 0))],
        out_specs=[pl.BlockSpec((tm, QR), lambda i: (i, 0)), pl.BlockSpec((tm, KR), lambda i: (i, 0)),
                   pl.BlockSpec((tm, LANE), lambda i: (i, 0))],
        out_shape=[jax.ShapeDtypeStruct((T, QR), BF16), jax.ShapeDtypeStruct((T, KR), F32),
                   jax.ShapeDtypeStruct((T, LANE), BF16)],
        compiler_params=_cp("parallel"), name="mla_pre",
    )(p_cq, p_ckv, p_small, cos, sin, qw.reshape(1, QR), kvw.reshape(1, KR))


def _attn_kernel(scale, qn_ref, qr_ref, cos_ref, sin_ref, kn_ref, kr_ref, v_ref, o_ref, k_scr):
    @pl.when(pl.program_id(2) == 0)
    def _():
        k_scr[...] = jnp.concatenate([kn_ref[...], kr_ref[...]], axis=1)

    qr = _rope128(qr_ref[...], cos_ref[...], sin_ref[...])
    q = _b(jnp.concatenate([qn_ref[...], qr], axis=1) * scale)
    s = _dot_nt(q, k_scr[...])
    m = jnp.max(s, axis=-1, keepdims=True)
    p = jnp.exp(s - m)
    l = jnp.sum(p, axis=-1, keepdims=True)
    o_ref[...] = (_dot(_b(p), v_ref[...]) / l).astype(o_ref.dtype)


def _attention(q, cos, sin, kv, kr, q_row0, k_row0, n_seq, tq_len, tk_len, n_heads):
    H = n_heads
    tq = _pick((256, 128), tq_len)
    assert q_row0 % tq == 0 and k_row0 % tk_len == 0
    nq = tq_len // tq
    qb0 = q_row0 // tq
    kb0 = k_row0 // tk_len
    scale = MLA_QK ** -0.5
    return pl.pallas_call(
        functools.partial(_attn_kernel, scale),
        grid=(n_seq, H, nq),
        in_specs=[pl.BlockSpec((tq, LANE), lambda s, h, i: (qb0 + s * nq + i, h)),
                  pl.BlockSpec((tq, LANE), lambda s, h, i: (qb0 + s * nq + i, H + h)),
                  pl.BlockSpec((tq, LANE), lambda s, h, i: (qb0 + s * nq + i, 0)),
                  pl.BlockSpec((tq, LANE), lambda s, h, i: (qb0 + s * nq + i, 0)),
                  pl.BlockSpec((tk_len, LANE), lambda s, h, i: (kb0 + s, 2 * h)),
                  pl.BlockSpec((tk_len, LANE), lambda s, h, i: (kb0 + s, 0)),
                  pl.BlockSpec((tk_len, LANE), lambda s, h, i: (kb0 + s, 2 * h + 1))],
        out_specs=pl.BlockSpec((tq, LANE), lambda s, h, i: (s * nq + i, h)),
        out_shape=jax.ShapeDtypeStruct((n_seq * tq_len, H * MLA_V), BF16),
        scratch_shapes=[pltpu.VMEM((tk_len, 2 * LANE), BF16)],
        compiler_params=_cp("parallel", "parallel", "arbitrary"), name="mla_attention",
    )(q, q, cos, sin, kv, kr, kv)


def _row_copy(src_ref, row, dst_ref, slot, r, sem):
    return pltpu.make_async_copy(src_ref.at[pl.ds(row, 1), :], dst_ref.at[slot, pl.ds(r, 1), :], sem.at[slot])


def _expert_up_kernel(rows, tok_ref, be_ref, nvalid_ref, nused_ref, h_ref, wg_ref, wu_ref, hid_ref, x_buf, sem):
    i = pl.program_id(0)
    n_used = nused_ref[0]
    slot = lax.rem(i, 2)

    def row_groups(blk, fn):
        def group(g, c):
            @pl.when(g * DMA_UNROLL < nvalid_ref[blk])
            def _():
                for j in range(DMA_UNROLL):
                    fn(g * DMA_UNROLL + j)
            return c

        lax.fori_loop(0, rows // DMA_UNROLL, group, 0)

    def gather(blk, dst_slot):
        row_groups(blk, lambda r: _row_copy(h_ref, tok_ref[blk * rows + r], x_buf, dst_slot, r, sem).start())

    @pl.when(i == 0)
    def _():
        x_buf[...] = jnp.zeros(x_buf.shape, x_buf.dtype)

    @pl.when(jnp.logical_and(i == 0, n_used > 0))
    def _():
        gather(0, 0)

    @pl.when(i + 1 < n_used)
    def _():
        gather(i + 1, 1 - slot)

    @pl.when(i < n_used)
    def _():
        row_groups(i, lambda r: _row_copy(h_ref, 0, x_buf, slot, r, sem).wait())
        xa, xb = _unpack_bf16_halves(x_buf[slot])
        half = xa.shape[1]
        g = _dot(xa, _b(wg_ref[:half, :])) + _dot(xb, _b(wg_ref[half:, :]))
        u = _dot(xa, _b(wu_ref[:half, :])) + _dot(xb, _b(wu_ref[half:, :]))
        hid_ref[...] = (_silu(g) * u).astype(hid_ref.dtype)

    @pl.when(i >= n_used)
    def _():
        hid_ref[...] = jnp.zeros(hid_ref.shape, hid_ref.dtype)


def _expert_down_kernel(be_ref, nused_ref, hid_ref, wd_ref, o_ref):
    blk = pl.program_id(0)

    @pl.when(blk < nused_ref[0])
    def _():
        o_ref[...] = _dot(hid_ref[...], _b(wd_ref[...]))

    @pl.when(blk >= nused_ref[0])
    def _():
        o_ref[...] = jnp.zeros(o_ref.shape, F32)


def _experts(h_packed, slot_tok, block_e, n_valid, n_used, w_gate, w_up, w_down, layer, tm):
    n_slots = slot_tok.shape[0]
    D, DE = w_gate.shape[-2:]
    assert tm % DMA_UNROLL == 0
    up_spec = pltpu.PrefetchScalarGridSpec(
        num_scalar_prefetch=4, grid=(n_slots // tm,),
        in_specs=[pl.BlockSpec(memory_space=pl.ANY),
                  pl.BlockSpec((None, None, D, DE), lambda i, tk, be, nv, nu: (layer, be[i], 0, 0)),
                  pl.BlockSpec((None, None, D, DE), lambda i, tk, be, nv, nu: (layer, be[i], 0, 0))],
        out_specs=pl.BlockSpec((tm, DE), lambda i, tk, be, nv, nu: (i, 0)),
        scratch_shapes=[pltpu.VMEM((2, tm, D // 2), jnp.uint32), pltpu.SemaphoreType.DMA((2,))])
    hid = pl.pallas_call(
        functools.partial(_expert_up_kernel, tm), grid_spec=up_spec,
        out_shape=jax.ShapeDtypeStruct((n_slots, DE), BF16),
        compiler_params=_cp("arbitrary", vmem=58 * 1024 * 1024), name="moe_up",
    )(slot_tok, block_e, n_valid, n_used, h_packed, w_gate, w_up)
    down_spec = pltpu.PrefetchScalarGridSpec(
        num_scalar_prefetch=2, grid=(n_slots // tm,),
        in_specs=[pl.BlockSpec((tm, DE), lambda i, be, nu: (i, 0)),
                  pl.BlockSpec((None, None, DE, D), lambda i, be, nu: (layer, be[i], 0, 0))],
        out_specs=pl.BlockSpec((tm, D), lambda i, be, nu: (i, 0)))
    return pl.pallas_call(
        _expert_down_kernel, grid_spec=down_spec,
        out_shape=jax.ShapeDtypeStruct((n_slots, D), F32),
        compiler_params=_cp("arbitrary"), name="moe_down",
    )(block_e, n_used, hid, w_down)


def _combine_kernel(rows, p0_ref, p1_ref, x_ref, g_ref, ga_ref, gb_ref, ys_ref, o_ref, a_scr, b_scr, sem):
    i = pl.program_id(0)

    def start(r, c):
        pltpu.make_async_copy(ys_ref.at[pl.ds(p0_ref[i * rows + r], 1), :], a_scr.at[pl.ds(r, 1), :], sem.at[0]).start()
        pltpu.make_async_copy(ys_ref.at[pl.ds(p1_ref[i * rows + r], 1), :], b_scr.at[pl.ds(r, 1), :], sem.at[1]).start()
        return c

    lax.fori_loop(0, rows, start, 0, unroll=DMA_UNROLL)

    def wait(r, c):
        pltpu.make_async_copy(ys_ref.at[pl.ds(0, 1), :], a_scr.at[pl.ds(r, 1), :], sem.at[0]).wait()
        pltpu.make_async_copy(ys_ref.at[pl.ds(0, 1), :], b_scr.at[pl.ds(r, 1), :], sem.at[1]).wait()
        return c

    lax.fori_loop(0, rows, wait, 0, unroll=DMA_UNROLL)
    o_ref[...] = x_ref[...] + g_ref[...] * (a_scr[...] * ga_ref[...] + b_scr[...] * gb_ref[...])


def _combine(x, mod3, ys, p0, p1, gate, tok, rows):
    T, D = x.shape
    grp = tok.group_of_tile(rows)
    grid_spec = pltpu.PrefetchScalarGridSpec(
        num_scalar_prefetch=2, grid=(T // rows,),
        in_specs=[pl.BlockSpec((rows, D), lambda i, a, b: (i, 0)),
                  pl.BlockSpec((None, 1, D), lambda i, a, b: (grp(i), 0, 5)),
                  pl.BlockSpec((rows, 1), lambda i, a, b: (i, 0)),
                  pl.BlockSpec((rows, 1), lambda i, a, b: (i, 0)),
                  pl.BlockSpec(memory_space=pl.ANY)],
        out_specs=pl.BlockSpec((rows, D), lambda i, a, b: (i, 0)),
        scratch_shapes=[pltpu.VMEM((rows, D), F32), pltpu.VMEM((rows, D), F32), pltpu.SemaphoreType.DMA((2,))])
    return pl.pallas_call(
        functools.partial(_combine_kernel, rows), grid_spec=grid_spec,
        out_shape=jax.ShapeDtypeStruct((T, D), F32),
        compiler_params=_cp("arbitrary"), name="moe_combine",
    )(p0, p1, x, mod3, gate[:, 0:1], gate[:, 1:2], ys)


def _route(logits, n_groups, n_experts, tm):
    n_tok = logits.shape[0]
    epg = n_experts // n_groups
    gl = logits[:, :n_groups]
    pg = jax.nn.softmax(gl, axis=-1)
    grp = jnp.argmax(gl, axis=-1)
    pg_sel = jnp.max(pg, axis=-1, keepdims=True)
    el = logits[:, n_groups:n_groups + n_experts].reshape(n_tok, n_groups, epg)
    el = jnp.take_along_axis(el, grp[:, None, None], axis=1)[:, 0]
    top_p, top_i = lax.top_k(jax.nn.softmax(el, axis=-1), TOP_K)
    gate = pg_sel * top_p / jnp.sum(top_p, axis=-1, keepdims=True)
    expert = grp[:, None].astype(jnp.int32) * epg + top_i.astype(jnp.int32)
    flat_e = expert.reshape(-1)
    n_assign = n_tok * TOP_K
    onehot = (flat_e[:, None] == jnp.arange(n_experts, dtype=jnp.int32)[None, :]).astype(jnp.int32)
    csum = jnp.cumsum(onehot, axis=0)
    rank = jnp.take_along_axis(csum, flat_e[:, None], axis=1)[:, 0] - 1
    counts = csum[-1]
    padded = (counts + tm - 1) // tm * tm
    pend = jnp.cumsum(padded)
    pstart = pend - padded
    dest = (pstart[flat_e] + rank).astype(jnp.int32)
    n_blocks = n_assign // tm + n_experts
    n_slots = n_blocks * tm
    slot_tok = jnp.zeros((n_slots,), jnp.int32).at[dest].set(jnp.arange(n_assign, dtype=jnp.int32) // TOP_K)
    block_e = jnp.minimum(jnp.searchsorted(pend, jnp.arange(n_blocks, dtype=jnp.int32) * tm, side='right'),
                          n_experts - 1).astype(jnp.int32)
    n_used = (pend[-1] // tm).astype(jnp.int32).reshape(1)
    blk_start = jnp.arange(n_blocks, dtype=jnp.int32) * tm
    n_valid = jnp.clip((pstart + counts)[block_e] - blk_start, 0, tm).astype(jnp.int32)
    dest2 = dest.reshape(n_tok, TOP_K)
    return slot_tok, gate, block_e, n_valid, n_used, dest2[:, 0], dest2[:, 1]


def _rope_rot_cols(w):
    q = MLA_ROPE // 4
    return jnp.concatenate([-w[..., q:2 * q], w[..., 0:q], -w[..., 3 * q:4 * q], w[..., 2 * q:3 * q]], axis=-1)


def _pad_cols(w, n):
    return jnp.pad(w, ((0, 0), (0, n - w.shape[1])))


def _rope_tables(tok):
    half = MLA_ROPE // 2
    nfreq = half // 2
    inv = ROPE_THETA ** (-jnp.arange(nfreq, dtype=F32) / nfreq)
    t = jnp.arange(tok.dec_seq, dtype=jnp.int32)
    ang_r = (t // GRID_W).astype(F32)[:, None] * inv
    ang_c = (t % GRID_W).astype(F32)[:, None] * inv
    z = jnp.zeros((tok.dec_seq, LANE - MLA_ROPE), F32)
    cos = jnp.concatenate([jnp.cos(ang_r), jnp.cos(ang_r), jnp.cos(ang_c), jnp.cos(ang_c), z], axis=1)
    sin = jnp.concatenate([jnp.sin(ang_r), jnp.sin(ang_r), jnp.sin(ang_c), jnp.sin(ang_c), z], axis=1)
    ctx_cos = jnp.concatenate([jnp.ones((tok.t_ctx, MLA_ROPE), F32), jnp.zeros((tok.t_ctx, LANE - MLA_ROPE), F32)], axis=1)
    cos = jnp.concatenate([ctx_cos] + [cos] * tok.n_lat_seq, axis=0)
    sin = jnp.concatenate([jnp.zeros((tok.t_ctx, LANE), F32)] + [sin] * tok.n_lat_seq, axis=0)
    return cos, sin


def kernel(x_prompt, x_sample, state_gdn, state_rwkv, cache_mla_ckv, cache_mla_krope, c, c_ctx, ada_w, ada_b, norm_mix_w, norm_ffn_w, w_in, gdn_conv_w, gdn_A_log, gdn_dt_bias, gdn_norm_w, rwkv_mu, rwkv_w0, rwkv_w_up, rwkv_a0, rwkv_a_up, rwkv_g_up, rwkv_k_k, rwkv_k_a, rwkv_r_k, rwkv_ln_w, rwkv_ln_b, mla_q_norm_w, mla_q_up, mla_kv_norm_w, mla_kv_up, w_out, router_group_w, router_group_b, router_expert_w, router_expert_b, moe_w_gate, moe_w_up, moe_w_down, final_norm_w):
    B, SEQ, D = x_prompt.shape
    DB, DSEQ, _ = x_sample.shape
    L = w_in.shape[0]
    PAST = cache_mla_ckv.shape[2]
    GH = gdn_A_log.shape[-1]
    GW = GH * GDN_HEAD
    RH = rwkv_r_k.shape[1]
    RW = RH * RWKV_HEAD
    RP = RW // LANE
    DL, AL, GL = rwkv_w_up.shape[2], rwkv_a_up.shape[2], rwkv_g_up.shape[1]
    QR = mla_q_norm_w.shape[1]
    KR = mla_kv_norm_w.shape[1]
    MH = mla_kv_up.shape[2] // (MLA_NOPE + MLA_V)
    NG = router_group_w.shape[2]
    NE = router_expert_w.shape[2]
    assert DL + AL == LANE and GL == LANE and 4 * GH <= LANE and NG + NE <= LANE and 1 + DB <= SUB
    tok = _Tok(B, SEQ, DB, DSEQ)
    T = tok.t
    tm = _pick((256, 128, 64), SEQ, DSEQ)
    tm_moe = 256

    gdn_cols = 4 * GW + 4 * GH
    rwkv_cols = 3 * RW + DL + AL + GL
    o_r = gdn_cols
    o_m = gdn_cols + rwkv_cols

    x = jnp.concatenate([x_prompt.reshape(B * SEQ, D), x_sample.reshape(DB * DSEQ, D)], axis=0)
    cvec = jnp.zeros((SUB, D), F32).at[0].set(c_ctx).at[1:1 + DB].set(c)
    mod = _ada_mod(cvec, ada_w, ada_b)
    cos, sin = _rope_tables(tok)

    new_gdn, new_rwkv, new_ckv, new_kr = [], [], [], []
    for l in range(L):
        mod3 = mod[l].reshape(SUB, 1, 6 * D)
        wl = w_in[l]
        w_gdn = _b(wl[:, :4 * GW])
        w_rwkv = _b(wl[:, o_r:o_r + 3 * RW])
        w_cq = _b(wl[:, o_m:o_m + QR])
        w_ckv = _b(wl[:, o_m + QR:o_m + QR + KR])
        w_kr = wl[:, o_m + QR + KR:o_m + QR + KR + MLA_ROPE]
        w_small = _b(jnp.concatenate([
            _pad_cols(wl[:, 4 * GW:4 * GW + 4 * GH], LANE),
            wl[:, o_r + 3 * RW:o_r + 3 * RW + DL + AL],
            wl[:, o_r + 3 * RW + DL + AL:o_r + rwkv_cols],
            w_kr, _rope_rot_cols(w_kr)], axis=1))

        h1 = _norm_mod(x, norm_mix_w[l], mod3, 0, tok, tm)
        p_gdn = _matmul(h1, w_gdn, name="proj_gdn")
        p_rwkv = _matmul(h1, w_rwkv, name="proj_rwkv")
        p_cq = _matmul(h1, w_cq, name="proj_cq")
        p_ckv = _matmul(h1, w_ckv, name="proj_ckv")
        p_small = _matmul(h1, w_small, name="proj_small")

        gate_par = jnp.zeros((SUB, LANE), F32)
        gate_par = gate_par.at[0, :2 * GH].set(-jnp.exp(gdn_A_log[l].reshape(-1)))
        gate_par = gate_par.at[1, :2 * GH].set(gdn_dt_bias[l].reshape(-1))
        qkv, gates = _gdn_pre(p_gdn, p_small, gdn_conv_w[l].T, gate_par, tok, GH, tm)
        gcol = gates[:, :4 * GH].reshape(T // CHUNK, CHUNK, 4 * GH)
        grow = jnp.swapaxes(gates[:, :2 * GH].reshape(T // CHUNK, CHUNK, 2 * GH), 1, 2).reshape(
            T // CHUNK, GH, 2 * CHUNK)
        oc_f, sc_f = _gdn_chunk(qkv, gcol, grow, 0, B, SEQ, GH, False, emit=True)
        oc_b, sc_b = _gdn_chunk(qkv, gcol, grow, 0, B, SEQ, GH, True, emit=True)
        ol_f, = _gdn_chunk(qkv, gcol, grow, tok.t_ctx, DB, DSEQ, GH, False, s0=state_gdn, s0_idx=(l, 0))
        ol_b, = _gdn_chunk(qkv, gcol, grow, tok.t_ctx, DB, DSEQ, GH, True, s0=state_gdn, s0_idx=(l, 1))
        o_gdn = _gdn_out(jnp.concatenate([oc_f, ol_f], axis=0), jnp.concatenate([oc_b, ol_b], axis=0),
                         p_gdn, gdn_norm_w[l], GH, tm)
        new_gdn.append(jnp.stack([sc_f, sc_b], axis=1))

        ln = (rwkv_ln_w[l].reshape(1, RW), rwkv_ln_b[l].reshape(1, RW), rwkv_r_k[l].reshape(1, RW))
        y_dir, s_dir = [], []
        for d in range(2):
            mu = rwkv_mu[l, d]
            wup = jnp.concatenate([rwkv_w_up[l, d], jnp.zeros((AL, RW), F32)], axis=0)
            aup = jnp.concatenate([jnp.zeros((DL, RW), F32), rwkv_a_up[l, d]], axis=0)
            par = (mu[:RW].reshape(1, RW), mu[RW:2 * RW].reshape(1, RW), mu[2 * RW:3 * RW].reshape(1, RW),
                   mu[3 * RW:].reshape(1, LANE), rwkv_w0[l, d].reshape(1, RW), wup,
                   rwkv_a0[l, d].reshape(1, RW), aup, rwkv_k_k[l].reshape(1, RW), rwkv_k_a[l].reshape(1, RW))
            s0 = state_rwkv[:, l, d].reshape(DB, RP, 2, RWKV_HEAD, RWKV_HEAD)
            s0 = jnp.einsum('bpaij,ac->bpaicj', s0, jnp.eye(2, dtype=F32)).reshape(DB, RP, LANE, LANE)
            yc, sc = _rwkv_chunk(p_rwkv, p_small, par, ln, 0, B, SEQ, d == 1, emit=True)
            yl, = _rwkv_chunk(p_rwkv, p_small, par, ln, tok.t_ctx, DB, DSEQ, d == 1, s0=s0)
            y_dir.append(jnp.concatenate([yc, yl], axis=0))
            sc = sc.reshape(B, RP, 2, RWKV_HEAD, 2, RWKV_HEAD)
            s_dir.append(jnp.einsum('bpaiaj->bpaij', sc).reshape(B, RH, RWKV_HEAD, RWKV_HEAD))
        o_rwkv = _rwkv_out(y_dir[0], y_dir[1], p_small, rwkv_g_up[l], tm)
        new_rwkv.append(jnp.stack(s_dir, axis=1))

        cqn, ckvn, krf = _mla_pre(p_cq, p_ckv, p_small, cos, sin, mla_q_norm_w[l], mla_kv_norm_w[l], tm)
        qu = mla_q_up[l].reshape(QR, MH, MLA_QK)
        q_rope = qu[:, :, MLA_NOPE:]
        w_q = _b(jnp.concatenate([qu[:, :, :MLA_NOPE].reshape(QR, MH * MLA_NOPE),
                                  jnp.concatenate([q_rope, _rope_rot_cols(q_rope)], axis=-1).reshape(QR, MH * LANE)],
                                 axis=1))
        q = _matmul(cqn, w_q, name="q_up")
        kr_cache = _b(jnp.pad(cache_mla_krope[:, l], ((0, 0), (0, 0), (0, LANE - MLA_ROPE))))
        ckv_parts, kr_parts = [], []
        for b in range(DB):
            r0 = tok.t_ctx + b * DSEQ
            ckv_parts += [_b(cache_mla_ckv[b, l]), _b(ckvn[r0:r0 + DSEQ])]
            kr_parts += [kr_cache[b], krf[r0:r0 + DSEQ]]
        ckv_all = jnp.concatenate(ckv_parts + [_b(ckvn[:tok.t_ctx])], axis=0)
        kr_all = jnp.concatenate(kr_parts + [krf[:tok.t_ctx]], axis=0)
        kv = _matmul(ckv_all, _b(mla_kv_up[l]), out_dtype=BF16, name="kv_up")
        o_c = _attention(q, cos, sin, kv, kr_all, 0, DB * (PAST + DSEQ), B, SEQ, SEQ, MH)
        o_l = _attention(q, cos, sin, kv, kr_all, tok.t_ctx, 0, DB, DSEQ, PAST + DSEQ, MH)
        o_mla = jnp.concatenate([o_c, o_l], axis=0)
        new_ckv.append(ckvn[:tok.t_ctx].reshape(B, SEQ, KR))
        new_kr.append(p_small[:tok.t_ctx, 3 * LANE:3 * LANE + MLA_ROPE].reshape(B, SEQ, MLA_ROPE))

        x = _wout_residual(x, mod3, o_gdn, o_rwkv, o_mla, _b(w_out[l]), tok)

        rw = _pad_cols(jnp.concatenate([router_group_w[l], router_expert_w[l]], axis=1), LANE)
        rb = _pad_cols(jnp.concatenate([router_group_b[l], router_expert_b[l]])[None, :], LANE)
        h2, logits = _norm_mod(x, norm_ffn_w[l], mod3, 3, tok, tm, router=(rw, rb))
        slot_tok, gate, block_e, n_valid, n_used, p0, p1 = _route(logits, NG, NE, tm_moe)
        ys = _experts(h2, slot_tok, block_e, n_valid, n_used, moe_w_gate, moe_w_up, moe_w_down, l, tm_moe)
        x = _combine(x, mod3, ys, p0, p1, gate, tok, tm)

    y = _final_norm(x, final_norm_w, tm)
    y_prompt = y[:tok.t_ctx].reshape(B, SEQ, D)
    y_sample = y[tok.t_ctx:].reshape(DB, DSEQ, D)
    return (y_prompt, y_sample, jnp.stack(new_gdn, axis=1), jnp.stack(new_rwkv, axis=1),
            jnp.stack(new_ckv, axis=1), jnp.stack(new_kr, axis=1))
```

```python
import functools

import numpy as np
import jax
import jax.numpy as jnp
from jax import lax
from jax.experimental import pallas as pl
from jax.experimental.pallas import tpu as pltpu

F32 = jnp.float32
BF16 = jnp.bfloat16
HI = lax.Precision.HIGHEST

EPS = 1e-6
L2_EPS = 1e-6
GRID_W = 64
ROPE_THETA = 10000.0
GDN_HEAD = 128
RWKV_HEAD = 64
RWKV_GN_EPS = RWKV_HEAD * 1e-5
MLA_V = 128
MLA_NOPE = 128
MLA_ROPE = 64
MLA_QK = MLA_NOPE + MLA_ROPE
TOP_K = 2
CHUNK = 64
LANE = 128
SUB = 8
DMA_UNROLL = 8
GDN_SEQ_PER_STEP = 4
RWKV_SEQ_PER_STEP = 2
VMEM_LIMIT = 52 * 1024 * 1024


def _cp(*sem, vmem=VMEM_LIMIT):
    return pltpu.CompilerParams(dimension_semantics=sem, vmem_limit_bytes=vmem)


def _pick(prefs, *ns):
    for p in prefs:
        if all(n % p == 0 for n in ns):
            return p
    raise ValueError(f"no tile in {prefs} divides {ns}")


def _dot(a, b, prec=None):
    return lax.dot_general(a, b, (((1,), (0,)), ((), ())), precision=prec, preferred_element_type=F32)


def _dot_nt(a, b, prec=None):
    return lax.dot_general(a, b, (((1,), (1,)), ((), ())), precision=prec, preferred_element_type=F32)


def _dot_tn(a, b, prec=None):
    return lax.dot_general(a, b, (((0,), (0,)), ((), ())), precision=prec, preferred_element_type=F32)


def _b(x):
    return x.astype(BF16)


def _sigmoid(x):
    return 1.0 / (1.0 + jnp.exp(-x))


def _silu(x):
    return x * _sigmoid(x)


def _softplus(x):
    return jnp.maximum(x, 0.0) + jnp.log(1.0 + jnp.exp(-jnp.abs(x)))


def _ada_kernel(c_ref, w_ref, b_ref, o_ref):
    s = _silu(c_ref[...])
    o_ref[...] = _dot(_b(s), _b(w_ref[...])) + b_ref[...]


def _ada_mod(cvec, ada_w, ada_b):
    L, D, N = ada_w.shape
    R = cvec.shape[0]
    tn = _pick((1024, 512, 256, 128), N)
    return pl.pallas_call(
        _ada_kernel,
        grid=(L, N // tn),
        in_specs=[pl.BlockSpec((R, D), lambda l, j: (0, 0)),
                  pl.BlockSpec((None, D, tn), lambda l, j: (l, 0, j)),
                  pl.BlockSpec((None, 1, tn), lambda l, j: (l, 0, j))],
        out_specs=pl.BlockSpec((None, R, tn), lambda l, j: (l, 0, j)),
        out_shape=jax.ShapeDtypeStruct((L, R, N), F32),
        compiler_params=_cp("parallel", "parallel"),
        name="ada_mod",
    )(cvec, ada_w, ada_b.reshape(L, 1, N))


def _norm_kernel(x_ref, w_ref, o_ref):
    x = x_ref[...]
    y = x * lax.rsqrt(jnp.mean(x * x, axis=-1, keepdims=True) + EPS) * w_ref[...]
    o_ref[...] = y.astype(o_ref.dtype)


def _norm_mod_kernel(x_ref, w_ref, sc_ref, sh_ref, o_ref):
    x = x_ref[...]
    y = x * lax.rsqrt(jnp.mean(x * x, axis=-1, keepdims=True) + EPS) * w_ref[...]
    o_ref[...] = (y * (1.0 + sc_ref[...]) + sh_ref[...]).astype(o_ref.dtype)


def _pack_bf16_halves(h):
    half = h.shape[1] // 2
    a = lax.bitcast_convert_type(h[:, :half].astype(BF16).astype(F32), jnp.uint32)
    b = lax.bitcast_convert_type(h[:, half:].astype(BF16).astype(F32), jnp.uint32)
    return a | (b >> 16)


def _unpack_bf16_halves(w):
    a = lax.bitcast_convert_type(w & jnp.uint32(0xFFFF0000), F32).astype(BF16)
    b = lax.bitcast_convert_type(w << 16, F32).astype(BF16)
    return a, b


def _norm_router_kernel(x_ref, w_ref, sc_ref, sh_ref, rw_ref, rb_ref, o_ref, lg_ref):
    x = x_ref[...]
    y = x * lax.rsqrt(jnp.mean(x * x, axis=-1, keepdims=True) + EPS) * w_ref[...]
    h = y * (1.0 + sc_ref[...]) + sh_ref[...]
    o_ref[...] = _pack_bf16_halves(h)
    lg_ref[...] = _dot(h, rw_ref[...], HI) + rb_ref[...]


class _Tok:
    def __init__(self, n_ctx_seq, seq, n_lat_seq, dec_seq):
        self.n_ctx_seq, self.seq, self.n_lat_seq, self.dec_seq = n_ctx_seq, seq, n_lat_seq, dec_seq
        self.t_ctx = n_ctx_seq * seq
        self.t_lat = n_lat_seq * dec_seq
        self.t = self.t_ctx + self.t_lat

    def group_of_tile(self, tm):
        assert self.t_ctx % tm == 0 and self.dec_seq % tm == 0
        nct, tpl = self.t_ctx // tm, self.dec_seq // tm
        return lambda i: jnp.where(i < nct, 0, 1 + (i - nct) // tpl)

    def seq_flags(self, tm):
        assert self.seq % tm == 0 and self.dec_seq % tm == 0
        starts = np.arange(0, self.t, tm)
        first = np.where(starts < self.t_ctx, starts % self.seq == 0, (starts - self.t_ctx) % self.dec_seq == 0)
        ends = starts + tm
        last = np.where(starts < self.t_ctx, ends % self.seq == 0, (ends - self.t_ctx) % self.dec_seq == 0)
        return jnp.asarray(first, jnp.int32), jnp.asarray(last, jnp.int32)


def _norm_mod(x, w, mod3, seg, tok, tm, router=None):
    T, D = x.shape
    grp = tok.group_of_tile(tm)
    in_specs = [pl.BlockSpec((tm, D), lambda i: (i, 0)),
                pl.BlockSpec((1, D), lambda i: (0, 0)),
                pl.BlockSpec((None, 1, D), lambda i: (grp(i), 0, seg + 1)),
                pl.BlockSpec((None, 1, D), lambda i: (grp(i), 0, seg))]
    if router is None:
        return pl.pallas_call(
            _norm_mod_kernel, grid=(T // tm,), in_specs=in_specs,
            out_specs=pl.BlockSpec((tm, D), lambda i: (i, 0)),
            out_shape=jax.ShapeDtypeStruct((T, D), BF16),
            compiler_params=_cp("parallel"), name="norm_mod",
        )(x, w.reshape(1, D), mod3, mod3)
    rw, rb = router
    NR = rw.shape[1]
    return pl.pallas_call(
        _norm_router_kernel, grid=(T // tm,),
        in_specs=in_specs + [pl.BlockSpec((D, NR), lambda i: (0, 0)), pl.BlockSpec((1, NR), lambda i: (0, 0))],
        out_specs=[pl.BlockSpec((tm, D // 2), lambda i: (i, 0)), pl.BlockSpec((tm, NR), lambda i: (i, 0))],
        out_shape=[jax.ShapeDtypeStruct((T, D // 2), jnp.uint32), jax.ShapeDtypeStruct((T, NR), F32)],
        compiler_params=_cp("parallel"), name="norm_router",
    )(x, w.reshape(1, D), mod3, mod3, rw, rb)


def _final_norm(x, w, tm):
    T, D = x.shape
    return pl.pallas_call(
        _norm_kernel, grid=(T // tm,),
        in_specs=[pl.BlockSpec((tm, D), lambda i: (i, 0)), pl.BlockSpec((1, D), lambda i: (0, 0))],
        out_specs=pl.BlockSpec((tm, D), lambda i: (i, 0)),
        out_shape=jax.ShapeDtypeStruct((T, D), F32),
        compiler_params=_cp("parallel"), name="final_norm",
    )(x, w.reshape(1, D))


def _mm_kernel(a_ref, b_ref, o_ref):
    o_ref[...] = _dot(a_ref[...], b_ref[...]).astype(o_ref.dtype)


def _matmul(a, b, out_dtype=F32, name="matmul"):
    M, K = a.shape
    N = b.shape[1]
    tm = _pick((1024, 512, 256, 128, 64, 8), M)
    tn = _pick((1024, 512, 256, 128), N)
    return pl.pallas_call(
        _mm_kernel, grid=(N // tn, M // tm),
        in_specs=[pl.BlockSpec((tm, K), lambda j, i: (i, 0)), pl.BlockSpec((K, tn), lambda j, i: (0, j))],
        out_specs=pl.BlockSpec((tm, tn), lambda j, i: (i, j)),
        out_shape=jax.ShapeDtypeStruct((M, N), out_dtype),
        compiler_params=_cp("parallel", "parallel"), name=name,
    )(a, b)


def _wout_kernel(x_ref, g_ref, a1_ref, a2_ref, a3_ref, w1_ref, w2_ref, w3_ref, o_ref):
    mix = _dot(a1_ref[...], w1_ref[...]) + _dot(a2_ref[...], w2_ref[...]) + _dot(a3_ref[...], w3_ref[...])
    o_ref[...] = x_ref[...] + g_ref[...] * mix


def _wout_residual(x, mod3, a1, a2, a3, w, tok):
    T, D = x.shape
    k1, k2, k3 = a1.shape[1], a2.shape[1], a3.shape[1]
    assert k1 % k2 == 0 and (k1 + k2) % k3 == 0
    tm = _pick((512, 256, 128), tok.t_ctx, tok.dec_seq)
    tn = _pick((1024, 512, 256, 128), D)
    grp = tok.group_of_tile(tm)
    nj = D // tn
    return pl.pallas_call(
        _wout_kernel, grid=(nj, T // tm),
        in_specs=[pl.BlockSpec((tm, tn), lambda j, i: (i, j)),
                  pl.BlockSpec((None, 1, tn), lambda j, i: (grp(i), 0, 2 * nj + j)),
                  pl.BlockSpec((tm, k1), lambda j, i: (i, 0)),
                  pl.BlockSpec((tm, k2), lambda j, i: (i, 0)),
                  pl.BlockSpec((tm, k3), lambda j, i: (i, 0)),
                  pl.BlockSpec((k1, tn), lambda j, i: (0, j)),
                  pl.BlockSpec((k2, tn), lambda j, i: (k1 // k2, j)),
                  pl.BlockSpec((k3, tn), lambda j, i: ((k1 + k2) // k3, j))],
        out_specs=pl.BlockSpec((tm, tn), lambda j, i: (i, j)),
        out_shape=jax.ShapeDtypeStruct((T, D), F32),
        compiler_params=_cp("parallel", "parallel"), name="wout_residual",
    )(x, mod3, a1, a2, a3, w, w, w)


def _split_bf16(x, terms):
    out = []
    for _ in range(terms):
        p = x.astype(BF16)
        out.append(p)
        x = x - p.astype(F32)
    return out


def _dot_exact_r(a01, x, terms=3):
    return sum(_dot(a01, p) for p in _split_bf16(x, terms))


def _dot_exact_l(x, b01, terms=3):
    return sum(_dot(p, b01) for p in _split_bf16(x, terms))


def _dot_nt_exact_l(x, b01, terms=3):
    return sum(_dot_nt(p, b01) for p in _split_bf16(x, terms))


def _one_bf16(mask):
    return jnp.where(mask, 1.0, 0.0).astype(BF16)


def _pair_masks(C, rev):
    M = 2 * C
    ti = lax.broadcasted_iota(jnp.int32, (M, M), 0)
    si = lax.broadcasted_iota(jnp.int32, (M, M), 1)
    shift = C.bit_length() - 1
    assert (1 << shift) == C
    same = jnp.right_shift(ti, shift) == jnp.right_shift(si, shift)
    before = (si > ti) if rev else (si < ti)
    strict = same & before
    incl = same & (before | (ti == si))
    tc = lax.broadcasted_iota(jnp.int32, (C, C), 0)
    sc = lax.broadcasted_iota(jnp.int32, (C, C), 1)
    cum = _one_bf16((sc >= tc) if rev else (sc <= tc))
    return ti, si, shift, same, strict, incl, cum


def _tri_inv_n(a_list, ti, si, top_shift):
    def same(shift):
        return jnp.right_shift(ti, shift) == jnp.right_shift(si, shift)

    n_mat = range(len(a_list))
    m3 = same(3)
    ad = [jnp.where(m3, a, 0.0) for a in a_list]
    adb = [_b(x) for x in ad]
    b2 = [_dot(x, x) for x in adb]
    b2b = [_b(x) for x in b2]
    n1 = [b2[i] - ad[i] - _dot(adb[i], b2b[i]) for i in n_mat]
    b4 = [_dot(x, x) for x in b2b]
    n = [n1[i] + b4[i] + _dot(_b(n1[i]), _b(b4[i])) for i in n_mat]
    for shift in range(3, top_shift):
        moff = same(shift + 1) & jnp.logical_not(same(shift))
        aoff = [jnp.where(moff, a, 0.0) for a in a_list]
        aob = [_b(x) for x in aoff]
        nb = [_b(x) for x in n]
        x = [aoff[i] + _dot(nb[i], aob[i]) for i in n_mat]
        n = [n[i] - x[i] - _dot(_b(x[i]), nb[i]) for i in n_mat]
    return n


def _shift_rows(x, halo_prev_ref, halo_next_ref, first, last, direction):
    tm = x.shape[0]
    rows = lax.broadcasted_iota(jnp.int32, (tm, 1), 0)
    if direction == 0:
        edge = halo_prev_ref[SUB - 1:SUB, :] * (1.0 - first)
        return jnp.where(rows == 0, edge, pltpu.roll(x, 1, 0))
    edge = halo_next_ref[0:1, :] * (1.0 - last)
    return jnp.where(rows == tm - 1, edge, pltpu.roll(x, tm - 1, 0))


def _halo_specs(tm, width, col, n_rows):
    r = tm // SUB
    nb = n_rows // SUB
    prev = pl.BlockSpec((SUB, width), lambda i, f, l: (jnp.maximum(i * r - 1, 0), col))
    nxt = pl.BlockSpec((SUB, width), lambda i, f, l: (jnp.minimum((i + 1) * r, nb - 1), col))
    return prev, nxt


def _gdn_pre_kernel(n_heads, first_ref, last_ref, x_ref, xp_ref, xn_ref, cw_ref, ab_ref, gp_ref,
                    qkv_ref, gates_ref):
    i = pl.program_id(0)
    first = first_ref[i].astype(F32)
    last = last_ref[i].astype(F32)
    x = x_ref[...]
    xm = _shift_rows(x, xp_ref, xn_ref, first, last, 0)
    xq = _shift_rows(x, xp_ref, xn_ref, first, last, 1)
    y = _silu(xm * cw_ref[0:1, :] + x * cw_ref[1:2, :] + xq * cw_ref[2:3, :])
    for hd in range(3 * n_heads):
        seg = y[:, hd * GDN_HEAD:(hd + 1) * GDN_HEAD]
        if hd < 2 * n_heads:
            seg = seg * lax.rsqrt(jnp.sum(seg * seg, axis=-1, keepdims=True) + L2_EPS)
            if hd < n_heads:
                seg = seg * (GDN_HEAD ** -0.5)
        qkv_ref[:, hd * GDN_HEAD:(hd + 1) * GDN_HEAD] = seg
    ab = ab_ref[...]
    lane = lax.broadcasted_iota(jnp.int32, ab.shape, 1)
    g = gp_ref[0:1, :] * _softplus(ab + gp_ref[1:2, :])
    gates_ref[...] = jnp.where(lane < 2 * n_heads, g, _sigmoid(ab))


def _gdn_pre(p_gdn, p_small, conv_w3, gate_par, tok, n_heads, tm):
    T = p_gdn.shape[0]
    W3 = 3 * n_heads * GDN_HEAD
    first, last = tok.seq_flags(tm)
    hp, hn = _halo_specs(tm, W3, 0, T)
    grid_spec = pltpu.PrefetchScalarGridSpec(
        num_scalar_prefetch=2, grid=(T // tm,),
        in_specs=[pl.BlockSpec((tm, W3), lambda i, f, l: (i, 0)), hp, hn,
                  pl.BlockSpec((3, W3), lambda i, f, l: (0, 0)),
                  pl.BlockSpec((tm, LANE), lambda i, f, l: (i, 0)),
                  pl.BlockSpec((SUB, LANE), lambda i, f, l: (0, 0))],
        out_specs=[pl.BlockSpec((tm, W3), lambda i, f, l: (i, 0)),
                   pl.BlockSpec((tm, LANE), lambda i, f, l: (i, 0))])
    return pl.pallas_call(
        functools.partial(_gdn_pre_kernel, n_heads), grid_spec=grid_spec,
        out_shape=[jax.ShapeDtypeStruct((T, W3), F32), jax.ShapeDtypeStruct((T, LANE), F32)],
        compiler_params=_cp("parallel"), name="gdn_pre",
    )(first, last, p_gdn, p_gdn, p_gdn, conv_w3, p_small, gate_par)


def _gdn_chunk_kernel(n_heads, n_grp, rev, has_s0, emit, *refs):
    refs = list(refs)
    G = n_grp
    in_refs = [refs[5 * g:5 * g + 5] for g in range(G)]
    pos = 5 * G
    s0_ref = None
    if has_s0:
        s0_ref = refs[pos]
        pos += 1
    o_ref = refs[pos]
    pos += 1
    so_ref = None
    if emit:
        so_ref = refs[pos]
        pos += 1
    s_scr = refs[pos]

    n = pl.program_id(1)
    n_last = pl.num_programs(1) - 1
    C = in_refs[0][0].shape[0]
    H = n_heads
    d = 1 if rev else 0

    @pl.when(n == 0)
    def _():
        if has_s0:
            s_scr[...] = s0_ref[...]
        else:
            s_scr[...] = jnp.zeros(s_scr.shape, F32)

    M = 2 * C
    HP = H // 2
    assert H % 2 == 0 and M == LANE
    ti, si, shift, same, strict, incl, cum = _pair_masks(C, rev)
    incl_b = _one_bf16(incl)
    ncol = in_refs[0][3].shape[1]
    er = lax.broadcasted_iota(jnp.int32, (ncol, 2 * H * LANE), 0)
    ec = lax.broadcasted_iota(jnp.int32, (ncol, 2 * H * LANE), 1) // LANE
    want = jnp.where(ec < H, d * H + ec, 2 * H + d * H + (ec - H))
    expand = _one_bf16(er == want)
    last_row = 0 if rev else C - 1

    def sl(h):
        return slice(h * GDN_HEAD, (h + 1) * GDN_HEAD)

    def pair_rows(x, p):
        return jnp.concatenate([x[:, sl(2 * p)], x[:, sl(2 * p + 1)]], axis=0)

    arr = []
    for g in range(G):
        q_ref, k_ref, v_ref, gcol_ref, grow_ref = in_refs[g]
        gb_all = _dot_exact_l(gcol_ref[...], expand)
        gcb = _dot_exact_r(cum, gb_all[:, :H * LANE])
        beta = gb_all[:, H * LANE:]
        gc_rows = _dot_nt_exact_l(grow_ref[d * HP:(d + 1) * HP, :], incl_b)
        q = q_ref[...]
        k = k_ref[...]
        egc = jnp.exp(gcb)
        gl = gcb[last_row:last_row + 1, :]
        kb = k * beta
        arr.append(dict(q=q, k=k, gcb=gcb, gc_rows=gc_rows, kb=kb, vbeta=v_ref[...] * beta, kbe=kb * egc,
                        qd=q * egc, kd=k * jnp.exp(gl - gcb), sdec=jnp.exp(gl)))

    pair_units = [(g, p) for g in range(G) for p in range(HP)]
    head_units = [(g, h) for g in range(G) for h in range(H)]
    pu = range(len(pair_units))

    def pidx(g, p):
        return g * HP + p

    k_st = [_b(pair_rows(arr[g]['k'], p)) for g, p in pair_units]
    lhs = [_b(jnp.concatenate([pair_rows(arr[g]['kb'], p), pair_rows(arr[g]['q'], p)], axis=0)) for g, p in pair_units]
    sc = [_dot_nt(lhs[u], k_st[u]) for u in pu]
    decay = [jnp.exp(jnp.where(incl, pair_rows(arr[g]['gcb'], p)
                               - jnp.broadcast_to(arr[g]['gc_rows'][p:p + 1, :], (M, M)), -jnp.inf))
             for g, p in pair_units]
    a_mat = [jnp.where(strict, sc[u][:M] * decay[u], 0.0) for u in pu]
    attn_b = [_b(sc[u][M:] * decay[u]) for u in pu]
    n_inv = _tri_inv_n(a_mat, ti, si, shift)
    rhs = [jnp.concatenate([pair_rows(arr[g]['vbeta'], p), pair_rows(arr[g]['kbe'], p)], axis=1)
           for g, p in pair_units]
    sol = [rhs[u] + _dot(_b(n_inv[u]), _b(rhs[u])) for u in pu]
    s_old = [s_scr[g, h] for g, h in head_units]
    s_b = [_b(s) for s in s_old]
    ws = [_dot(_b(jnp.concatenate([sol[pidx(g, h // 2)][(h % 2) * C:(h % 2 + 1) * C, GDN_HEAD:],
                                   arr[g]['qd'][:, sl(h)]], axis=0)), s_b[g * H + h])
          for g, h in head_units]
    v_new = [sol[u][:, :GDN_HEAD] - jnp.concatenate([ws[g * H + 2 * p][:C], ws[g * H + 2 * p + 1][:C]], axis=0)
             for u, (g, p) in enumerate(pair_units)]
    vn_b = [_b(x) for x in v_new]
    o = [jnp.concatenate([ws[g * H + 2 * p][C:], ws[g * H + 2 * p + 1][C:]], axis=0) + _dot(attn_b[u], vn_b[u])
         for u, (g, p) in enumerate(pair_units)]
    for u, (g, p) in enumerate(pair_units):
        o_ref[g, :, sl(2 * p)] = o[u][:C]
        o_ref[g, :, sl(2 * p + 1)] = o[u][C:]
    for g, h in head_units:
        upd = _dot_tn(_b(arr[g]['kd'][:, sl(h)]), vn_b[pidx(g, h // 2)][(h % 2) * C:(h % 2 + 1) * C])
        s_scr[g, h] = s_old[g * H + h] * arr[g]['sdec'][:, sl(h)] + upd

    if emit:
        @pl.when(n == n_last)
        def _():
            so_ref[...] = s_scr[...]


def _gdn_chunk(qkv, gcol, grow, row0, n_seq, seq_len, n_heads, rev, s0=None, s0_idx=None, emit=False):
    T = qkv.shape[0]
    C = CHUNK
    N = seq_len // C
    H = n_heads
    W = H * GDN_HEAD
    b0 = row0 // C

    G = min(GDN_SEQ_PER_STEP, n_seq)
    assert n_seq % G == 0

    def chunk(n):
        return (N - 1 - n) if rev else n

    def blk(s, n, g):
        return b0 + (s * G + g) * N + chunk(n)

    in_specs, args = [], []
    for g in range(G):
        in_specs += [pl.BlockSpec((C, W), lambda s, n, g=g: (blk(s, n, g), 0)),
                     pl.BlockSpec((C, W), lambda s, n, g=g: (blk(s, n, g), 1)),
                     pl.BlockSpec((C, W), lambda s, n, g=g: (blk(s, n, g), 2)),
                     pl.BlockSpec((None, C, gcol.shape[2]), lambda s, n, g=g: (blk(s, n, g), 0, 0)),
                     pl.BlockSpec((None, grow.shape[1], grow.shape[2]), lambda s, n, g=g: (blk(s, n, g), 0, 0))]
        args += [qkv, qkv, qkv, gcol, grow]
    if s0 is not None:
        l, d = s0_idx
        in_specs.append(pl.BlockSpec((G, None, None, H, GDN_HEAD, GDN_HEAD), lambda s, n: (s, l, d, 0, 0, 0)))
        args.append(s0)
    out_specs = [pl.BlockSpec((G, C, W), lambda s, n: (s, chunk(n), 0))]
    out_shape = [jax.ShapeDtypeStruct((n_seq, seq_len, W), F32)]
    if emit:
        out_specs.append(pl.BlockSpec((G, H, GDN_HEAD, GDN_HEAD), lambda s, n: (s, 0, 0, 0)))
        out_shape.append(jax.ShapeDtypeStruct((n_seq, H, GDN_HEAD, GDN_HEAD), F32))
    res = pl.pallas_call(
        functools.partial(_gdn_chunk_kernel, H, G, rev, s0 is not None, emit),
        grid=(n_seq // G, N), in_specs=in_specs, out_specs=out_specs, out_shape=out_shape,
        scratch_shapes=[pltpu.VMEM((G, H, GDN_HEAD, GDN_HEAD), F32)],
        compiler_params=_cp("parallel", "arbitrary"), name="gdn_chunk",
    )(*args)
    res = list(res)
    res[0] = res[0].reshape(n_seq * seq_len, W)
    return res


def _gdn_out_kernel(n_heads, of_ref, ob_ref, z_ref, nw_ref, o_ref):
    for h in range(n_heads):
        sl = slice(h * GDN_HEAD, (h + 1) * GDN_HEAD)
        o = of_ref[:, sl] + ob_ref[:, sl]
        y = o * lax.rsqrt(jnp.mean(o * o, axis=-1, keepdims=True) + EPS) * nw_ref[...]
        o_ref[:, sl] = (y * _silu(z_ref[:, sl])).astype(o_ref.dtype)


def _gdn_out(o_f, o_b, p_gdn, norm_w, n_heads, tm):
    T, W = o_f.shape
    return pl.pallas_call(
        functools.partial(_gdn_out_kernel, n_heads), grid=(T // tm,),
        in_specs=[pl.BlockSpec((tm, W), lambda i: (i, 0)), pl.BlockSpec((tm, W), lambda i: (i, 0)),
                  pl.BlockSpec((tm, W), lambda i: (i, 3)), pl.BlockSpec((1, GDN_HEAD), lambda i: (0, 0))],
        out_specs=pl.BlockSpec((tm, W), lambda i: (i, 0)),
        out_shape=jax.ShapeDtypeStruct((T, W), BF16),
        compiler_params=_cp("parallel"), name="gdn_out",
    )(o_f, o_b, p_gdn, norm_w.reshape(1, GDN_HEAD))


def _pair_ones():
    r = lax.broadcasted_iota(jnp.int32, (LANE, LANE), 0) // RWKV_HEAD
    c = lax.broadcasted_iota(jnp.int32, (LANE, LANE), 1) // RWKV_HEAD
    return r == c


def _rwkv_features(rev, at_edge, in_ref, par_refs, ones_b):
    r_ref, k_ref, v_ref, f_ref, rh_ref, kh_ref, vh_ref, fh_ref = in_ref
    mu_r_ref, mu_k_ref, mu_v_ref, mu_f_ref, w0_ref, wup_ref, a0_ref, aup_ref, kkw_ref, kaw_ref = par_refs
    C = r_ref.shape[0]
    rows = lax.broadcasted_iota(jnp.int32, (C, 1), 0)
    keep = 1.0 - at_edge

    def mix(x_ref, h_ref, mu_ref):
        x = x_ref[...]
        if rev:
            sh = jnp.where(rows == C - 1, h_ref[0:1, :] * keep, pltpu.roll(x, C - 1, 0))
        else:
            sh = jnp.where(rows == 0, h_ref[SUB - 1:SUB, :] * keep, pltpu.roll(x, 1, 0))
        return x + (sh - x) * mu_ref[...]

    r = mix(r_ref, rh_ref, mu_r_ref)
    k = mix(k_ref, kh_ref, mu_k_ref)
    v = mix(v_ref, vh_ref, mu_v_ref)
    f = mix(f_ref, fh_ref, mu_f_ref)
    w_lin = w0_ref[...] + _dot(_b(jnp.tanh(f)), _b(wup_ref[...]))
    lw = -jnp.exp(-_softplus(-w_lin) - 0.5)
    a = _sigmoid(a0_ref[...] + _dot(_b(f), _b(aup_ref[...])))
    kraw = k * kkw_ref[...]
    sq = kraw * kraw
    ss = jnp.concatenate([_dot_exact_l(sq[:, p * LANE:(p + 1) * LANE], ones_b, 2)
                          for p in range(kraw.shape[1] // LANE)], axis=1)
    kk = kraw * lax.rsqrt(ss + L2_EPS)
    return r, k * (1.0 + (a - 1.0) * kaw_ref[...]), v, kk, kk * a, lw


def _rwkv_chunk_kernel(n_pairs, n_grp, rev, has_s0, emit, *refs):
    refs = list(refs)
    G = n_grp
    in_refs = [refs[8 * g:8 * g + 8] for g in range(G)]
    pos = 8 * G
    par_refs = refs[pos:pos + 10]
    pos += 10
    lnw_ref, lnb_ref, rk_ref = refs[pos:pos + 3]
    pos += 3
    s0_ref = None
    if has_s0:
        s0_ref = refs[pos]
        pos += 1
    y_ref = refs[pos]
    pos += 1
    so_ref = None
    if emit:
        so_ref = refs[pos]
        pos += 1
    s_scr = refs[pos]

    n = pl.program_id(1)
    n_last = pl.num_programs(1) - 1
    C = in_refs[0][0].shape[0]

    @pl.when(n == 0)
    def _():
        if has_s0:
            s_scr[...] = s0_ref[...]
        else:
            s_scr[...] = jnp.zeros(s_scr.shape, F32)

    M = 2 * C
    assert M == LANE and RWKV_HEAD == C
    ti, si, shift, same, strict, incl, cum = _pair_masks(C, rev)
    ones_b = _one_bf16(same)
    last_row = 0 if rev else C - 1
    inv_n = 1.0 / RWKV_HEAD
    pairs = range(n_pairs)
    sls = [slice(p * LANE, (p + 1) * LANE) for p in pairs]

    def st2(x):
        return jnp.concatenate([x, x], axis=0)

    def fold(x):
        return x[:C] + x[C:]

    def seg_sum(x):
        return _dot_exact_l(x, ones_b, 2)

    at_edge = (n == 0).astype(F32)
    arr = []
    for g in range(G):
        r, k, v, kk, b, lw = _rwkv_features(rev, at_edge, in_refs[g], par_refs, ones_b)
        cl = _dot_exact_r(cum, lw)
        cll = cl[last_row:last_row + 1, :]
        e_neg = jnp.exp(-cl)
        e_end = jnp.exp(cll - cl)
        arr.append(dict(r=r, k=k, v=v, rtb=_b(r * jnp.exp(cl)), ktb=_b(k * e_neg), btb=_b(b * e_neg),
                        atb=_b(-kk * jnp.exp(cl - lw)), keb=_b(k * e_end), beb=_b(b * e_end),
                        vb=_b(v), s_dec=jnp.exp(cll)))

    units = [(g, p) for g in range(G) for p in pairs]
    un = range(len(units))

    def col(name, u):
        g, p = units[u]
        return arr[g][name][:, sls[p]]

    s_old = [s_scr[g, p] for g, p in units]
    s_b = [_b(s) for s in s_old]
    x0 = [_dot_nt(jnp.concatenate([col('atb', u), col('rtb', u)], axis=0), s_b[u]) for u in un]
    la = [jnp.concatenate([st2(col('atb', u)) * ones_b, st2(col('rtb', u)) * ones_b], axis=0) for u in un]
    sk = [_dot_nt(la[u], st2(col('ktb', u))) for u in un]
    sb = [_dot_nt(la[u], st2(col('btb', u))) for u in un]
    a_ak = [_b(jnp.where(strict, x[:M], 0.0)) for x in sk]
    r_k = [_b(jnp.where(incl, x[M:], 0.0)) for x in sk]
    a_neg = [jnp.where(strict, -x[:M], 0.0) for x in sb]
    r_b = [_b(jnp.where(incl, x[M:], 0.0)) for x in sb]
    n_inv = _tri_inv_n(a_neg, ti, si, shift)
    v_st = [st2(col('vb', u)) * ones_b for u in un]
    wm = [x0[u][:C] + fold(_dot(a_ak[u], v_st[u])) for u in un]
    uu = [wm[u] + fold(_dot(_b(n_inv[u]), st2(_b(wm[u])) * ones_b)) for u in un]
    u_b = [_b(x) for x in uu]
    u_st = [st2(x) * ones_b for x in u_b]
    y = [x0[u][C:] + fold(_dot(jnp.concatenate([r_k[u], r_b[u]], axis=1),
                               jnp.concatenate([v_st[u], u_st[u]], axis=0))) for u in un]
    upd = [_dot_tn(jnp.concatenate([col('vb', u), u_b[u]], axis=0),
                   jnp.concatenate([col('keb', u), col('beb', u)], axis=0)) for u in un]
    for u, (g, p) in enumerate(units):
        s_scr[g, p] = s_old[u] * col('s_dec', u) + jnp.where(same, upd[u], 0.0)
    for u, (g, p) in enumerate(units):
        sl = sls[p]
        mu = seg_sum(y[u]) * inv_n
        yc = y[u] - mu
        var = seg_sum(yc * yc) * inv_n
        yn = yc * lax.rsqrt(var + RWKV_GN_EPS) * lnw_ref[:, sl] + lnb_ref[:, sl]
        bonus = seg_sum(col('r', u) * col('k', u) * rk_ref[:, sl])
        y_ref[g, :, sl] = yn + bonus * col('v', u)

    if emit:
        @pl.when(n == n_last)
        def _():
            so_ref[...] = s_scr[...]


def _rwkv_chunk(p_rwkv, p_small, par, ln, row0, n_seq, seq_len, rev, s0=None, emit=False):
    T, W3 = p_rwkv.shape
    W = W3 // 3
    P = W // LANE
    C = CHUNK
    N = seq_len // C
    b0 = row0 // C
    cs = C // SUB
    n_sub = T // SUB

    G = min(RWKV_SEQ_PER_STEP, n_seq)
    assert n_seq % G == 0

    def chunk(n):
        return (N - 1 - n) if rev else n

    def blk(s, n, g):
        return b0 + (s * G + g) * N + chunk(n)

    def halo(s, n, g):
        if rev:
            return jnp.minimum((blk(s, n, g) + 1) * cs, n_sub - 1)
        return jnp.maximum(blk(s, n, g) * cs - 1, 0)

    in_specs, args = [], []
    for g in range(G):
        in_specs += [pl.BlockSpec((C, W), lambda s, n, g=g, c=c: (blk(s, n, g), c)) for c in range(3)]
        in_specs.append(pl.BlockSpec((C, LANE), lambda s, n, g=g: (blk(s, n, g), 1)))
        in_specs += [pl.BlockSpec((SUB, W), lambda s, n, g=g, c=c: (halo(s, n, g), c)) for c in range(3)]
        in_specs.append(pl.BlockSpec((SUB, LANE), lambda s, n, g=g: (halo(s, n, g), 1)))
        args += [p_rwkv, p_rwkv, p_rwkv, p_small, p_rwkv, p_rwkv, p_rwkv, p_small]
    row = lambda width: pl.BlockSpec((1, width), lambda s, n: (0, 0))
    full = lambda a, b: pl.BlockSpec((a, b), lambda s, n: (0, 0))
    in_specs += [row(W), row(W), row(W), row(LANE), row(W), full(LANE, W), row(W), full(LANE, W), row(W), row(W)]
    args += list(par)
    in_specs += [row(W) for _ in range(3)]
    args += list(ln)
    if s0 is not None:
        in_specs.append(pl.BlockSpec((G, P, LANE, LANE), lambda s, n: (s, 0, 0, 0)))
        args.append(s0)
    out_specs = [pl.BlockSpec((G, C, W), lambda s, n: (s, chunk(n), 0))]
    out_shape = [jax.ShapeDtypeStruct((n_seq, seq_len, W), F32)]
    if emit:
        out_specs.append(pl.BlockSpec((G, P, LANE, LANE), lambda s, n: (s, 0, 0, 0)))
        out_shape.append(jax.ShapeDtypeStruct((n_seq, P, LANE, LANE), F32))
    res = list(pl.pallas_call(
        functools.partial(_rwkv_chunk_kernel, P, G, rev, s0 is not None, emit),
        grid=(n_seq // G, N), in_specs=in_specs, out_specs=out_specs, out_shape=out_shape,
        scratch_shapes=[pltpu.VMEM((G, P, LANE, LANE), F32)],
        compiler_params=_cp("parallel", "arbitrary"), name="rwkv_chunk",
    )(*args))
    res[0] = res[0].reshape(n_seq * seq_len, W)
    return res


def _rwkv_out_kernel(yf_ref, yb_ref, fg_ref, gup_ref, o_ref):
    g = _dot(_b(_sigmoid(fg_ref[...])), _b(gup_ref[...]))
    o_ref[...] = ((yf_ref[...] + yb_ref[...]) * g).astype(o_ref.dtype)


def _rwkv_out(y_f, y_b, p_small, g_up, tm):
    T, W = y_f.shape
    return pl.pallas_call(
        _rwkv_out_kernel, grid=(T // tm,),
        in_specs=[pl.BlockSpec((tm, W), lambda i: (i, 0)), pl.BlockSpec((tm, W), lambda i: (i, 0)),
                  pl.BlockSpec((tm, LANE), lambda i: (i, 2)), pl.BlockSpec((LANE, W), lambda i: (0, 0))],
        out_specs=pl.BlockSpec((tm, W), lambda i: (i, 0)),
        out_shape=jax.ShapeDtypeStruct((T, W), BF16),
        compiler_params=_cp("parallel"), name="rwkv_out",
    )(y_f, y_b, p_small, g_up)


def _rope128(x, cos, sin):
    return x * cos + pltpu.roll(x, MLA_ROPE, 1) * sin


def _mla_pre_kernel(cq_ref, ckv_ref, kr_ref, cos_ref, sin_ref, qw_ref, kvw_ref, cqn_ref, ckvn_ref, krf_ref):
    cq = cq_ref[...]
    cqn_ref[...] = (cq * lax.rsqrt(jnp.mean(cq * cq, axis=-1, keepdims=True) + EPS) * qw_ref[...]).astype(BF16)
    ckv = ckv_ref[...]
    ckvn_ref[...] = ckv * lax.rsqrt(jnp.mean(ckv * ckv, axis=-1, keepdims=True) + EPS) * kvw_ref[...]
    krf_ref[...] = _rope128(kr_ref[...], cos_ref[...], sin_ref[...]).astype(BF16)


def _mla_pre(p_cq, p_ckv, p_small, cos, sin, qw, kvw, tm):
    T, QR = p_cq.shape
    KR = p_ckv.shape[1]
    return pl.pallas_call(
        _mla_pre_kernel, grid=(T // tm,),
        in_specs=[pl.BlockSpec((tm, QR), lambda i: (i, 0)), pl.BlockSpec((tm, KR), lambda i: (i, 0)),
                  pl.BlockSpec((tm, LANE), lambda i: (i, 3)),
                  pl.BlockSpec((tm, LANE), lambda i: (i, 0)), pl.BlockSpec((tm, LANE), lambda i: (i, 0)),
                  pl.BlockSpec((1, QR), lambda i: (0, 0)), pl.BlockSpec((1, KR), lambda i: (0, 0))],
        out_specs=[pl.BlockSpec((tm, QR), lambda i: (i, 0)), pl.BlockSpec((tm, KR), lambda i: (i, 0)),
                   pl.BlockSpec((tm, LANE), lambda i: (i, 0))],
        out_shape=[jax.ShapeDtypeStruct((T, QR), BF16), jax.ShapeDtypeStruct((T, KR), F32),
                   jax.ShapeDtypeStruct((T, LANE), BF16)],
        compiler_params=_cp("parallel"), name="mla_pre",
    )(p_cq, p_ckv, p_small, cos, sin, qw.reshape(1, QR), kvw.reshape(1, KR))


def _attn_kernel(scale, qn_ref, qr_ref, cos_ref, sin_ref, kn_ref, kr_ref, v_ref, o_ref, k_scr):
    @pl.when(pl.program_id(2) == 0)
    def _():
        k_scr[...] = jnp.concatenate([kn_ref[...], kr_ref[...]], axis=1)

    qr = _rope128(qr_ref[...], cos_ref[...], sin_ref[...])
    q = _b(jnp.concatenate([qn_ref[...], qr], axis=1) * scale)
    s = _dot_nt(q, k_scr[...])
    m = jnp.max(s, axis=-1, keepdims=True)
    p = jnp.exp(s - m)
    l = jnp.sum(p, axis=-1, keepdims=True)
    o_ref[...] = (_dot(_b(p), v_ref[...]) / l).astype(o_ref.dtype)


def _attention(q, cos, sin, kv, kr, q_row0, k_row0, n_seq, tq_len, tk_len, n_heads):
    H = n_heads
    tq = _pick((256, 128), tq_len)
    assert q_row0 % tq == 0 and k_row0 % tk_len == 0
    nq = tq_len // tq
    qb0 = q_row0 // tq
    kb0 = k_row0 // tk_len
    scale = MLA_QK ** -0.5
    return pl.pallas_call(
        functools.partial(_attn_kernel, scale),
        grid=(n_seq, H, nq),
        in_specs=[pl.BlockSpec((tq, LANE), lambda s, h, i: (qb0 + s * nq + i, h)),
                  pl.BlockSpec((tq, LANE), lambda s, h, i: (qb0 + s * nq + i, H + h)),
                  pl.BlockSpec((tq, LANE), lambda s, h, i: (qb0 + s * nq + i, 0)),
                  pl.BlockSpec((tq, LANE), lambda s, h, i: (qb0 + s * nq + i, 0)),
                  pl.BlockSpec((tk_len, LANE), lambda s, h, i: (kb0 + s, 2 * h)),
                  pl.BlockSpec((tk_len, LANE), lambda s, h, i: (kb0 + s, 0)),
                  pl.BlockSpec((tk_len, LANE), lambda s, h, i: (kb0 + s, 2 * h + 1))],
        out_specs=pl.BlockSpec((tq, LANE), lambda s, h, i: (s * nq + i, h)),
        out_shape=jax.ShapeDtypeStruct((n_seq * tq_len, H * MLA_V), BF16),
        scratch_shapes=[pltpu.VMEM((tk_len, 2 * LANE), BF16)],
        compiler_params=_cp("parallel", "parallel", "arbitrary"), name="mla_attention",
    )(q, q, cos, sin, kv, kr, kv)


def _row_copy(src_ref, row, dst_ref, slot, r, sem):
    return pltpu.make_async_copy(src_ref.at[pl.ds(row, 1), :], dst_ref.at[slot, pl.ds(r, 1), :], sem.at[slot])


def _expert_up_kernel(rows, tok_ref, be_ref, nvalid_ref, nused_ref, h_ref, wg_ref, wu_ref, hid_ref, x_buf, sem):
    i = pl.program_id(0)
    n_used = nused_ref[0]
    slot = lax.rem(i, 2)

    def row_groups(blk, fn):
        def group(g, c):
            @pl.when(g * DMA_UNROLL < nvalid_ref[blk])
            def _():
                for j in range(DMA_UNROLL):
                    fn(g * DMA_UNROLL + j)
            return c

        lax.fori_loop(0, rows // DMA_UNROLL, group, 0)

    def gather(blk, dst_slot):
        row_groups(blk, lambda r: _row_copy(h_ref, tok_ref[blk * rows + r], x_buf, dst_slot, r, sem).start())

    @pl.when(i == 0)
    def _():
        x_buf[...] = jnp.zeros(x_buf.shape, x_buf.dtype)

    @pl.when(jnp.logical_and(i == 0, n_used > 0))
    def _():
        gather(0, 0)

    @pl.when(i + 1 < n_used)
    def _():
        gather(i + 1, 1 - slot)

    @pl.when(i < n_used)
    def _():
        row_groups(i, lambda r: _row_copy(h_ref, 0, x_buf, slot, r, sem).wait())
        xa, xb = _unpack_bf16_halves(x_buf[slot])
        half = xa.shape[1]
        g = _dot(xa, _b(wg_ref[:half, :])) + _dot(xb, _b(wg_ref[half:, :]))
        u = _dot(xa, _b(wu_ref[:half, :])) + _dot(xb, _b(wu_ref[half:, :]))
        hid_ref[...] = (_silu(g) * u).astype(hid_ref.dtype)

    @pl.when(i >= n_used)
    def _():
        hid_ref[...] = jnp.zeros(hid_ref.shape, hid_ref.dtype)


def _expert_down_kernel(be_ref, nused_ref, hid_ref, wd_ref, o_ref):
    blk = pl.program_id(0)

    @pl.when(blk < nused_ref[0])
    def _():
        o_ref[...] = _pack_bf16_halves(_dot(hid_ref[...], _b(wd_ref[...])))

    @pl.when(blk >= nused_ref[0])
    def _():
        o_ref[...] = jnp.zeros(o_ref.shape, o_ref.dtype)


def _experts(h_packed, slot_tok, block_e, n_valid, n_used, w_gate, w_up, w_down, layer, tm):
    n_slots = slot_tok.shape[0]
    D, DE = w_gate.shape[-2:]
    assert tm % DMA_UNROLL == 0
    up_spec = pltpu.PrefetchScalarGridSpec(
        num_scalar_prefetch=4, grid=(n_slots // tm,),
        in_specs=[pl.BlockSpec(memory_space=pl.ANY),
                  pl.BlockSpec((None, None, D, DE), lambda i, tk, be, nv, nu: (layer, be[i], 0, 0)),
                  pl.BlockSpec((None, None, D, DE), lambda i, tk, be, nv, nu: (layer, be[i], 0, 0))],
        out_specs=pl.BlockSpec((tm, DE), lambda i, tk, be, nv, nu: (i, 0)),
        scratch_shapes=[pltpu.VMEM((2, tm, D // 2), jnp.uint32), pltpu.SemaphoreType.DMA((2,))])
    hid = pl.pallas_call(
        functools.partial(_expert_up_kernel, tm), grid_spec=up_spec,
        out_shape=jax.ShapeDtypeStruct((n_slots, DE), BF16),
        compiler_params=_cp("arbitrary", vmem=58 * 1024 * 1024), name="moe_up",
    )(slot_tok, block_e, n_valid, n_used, h_packed, w_gate, w_up)
    down_spec = pltpu.PrefetchScalarGridSpec(
        num_scalar_prefetch=2, grid=(n_slots // tm,),
        in_specs=[pl.BlockSpec((tm, DE), lambda i, be, nu: (i, 0)),
                  pl.BlockSpec((None, None, DE, D), lambda i, be, nu: (layer, be[i], 0, 0))],
        out_specs=pl.BlockSpec((tm, D // 2), lambda i, be, nu: (i, 0)))
    return pl.pallas_call(
        _expert_down_kernel, grid_spec=down_spec,
        out_shape=jax.ShapeDtypeStruct((n_slots, D // 2), jnp.uint32),
        compiler_params=_cp("arbitrary"), name="moe_down",
    )(block_e, n_used, hid, w_down)


def _combine_kernel(rows, p0_ref, p1_ref, x_ref, g_ref, ga_ref, gb_ref, ys_ref, o_ref, a_scr, b_scr, sem):
    i = pl.program_id(0)

    def start(r, c):
        pltpu.make_async_copy(ys_ref.at[pl.ds(p0_ref[i * rows + r], 1), :], a_scr.at[pl.ds(r, 1), :], sem.at[0]).start()
        pltpu.make_async_copy(ys_ref.at[pl.ds(p1_ref[i * rows + r], 1), :], b_scr.at[pl.ds(r, 1), :], sem.at[1]).start()
        return c

    lax.fori_loop(0, rows, start, 0, unroll=DMA_UNROLL)

    def wait(r, c):
        pltpu.make_async_copy(ys_ref.at[pl.ds(0, 1), :], a_scr.at[pl.ds(r, 1), :], sem.at[0]).wait()
        pltpu.make_async_copy(ys_ref.at[pl.ds(0, 1), :], b_scr.at[pl.ds(r, 1), :], sem.at[1]).wait()
        return c

    lax.fori_loop(0, rows, wait, 0, unroll=DMA_UNROLL)
    a_lo, a_hi = _unpack_bf16_halves(a_scr[...])
    b_lo, b_hi = _unpack_bf16_halves(b_scr[...])
    half = a_lo.shape[1]
    ga = ga_ref[...]
    gb = gb_ref[...]
    o_ref[:, :half] = x_ref[:, :half] + g_ref[:, :half] * (a_lo.astype(F32) * ga + b_lo.astype(F32) * gb)
    o_ref[:, half:] = x_ref[:, half:] + g_ref[:, half:] * (a_hi.astype(F32) * ga + b_hi.astype(F32) * gb)


def _combine(x, mod3, ys, p0, p1, gate, tok, rows):
    T, D = x.shape
    grp = tok.group_of_tile(rows)
    grid_spec = pltpu.PrefetchScalarGridSpec(
        num_scalar_prefetch=2, grid=(T // rows,),
        in_specs=[pl.BlockSpec((rows, D), lambda i, a, b: (i, 0)),
                  pl.BlockSpec((None, 1, D), lambda i, a, b: (grp(i), 0, 5)),
                  pl.BlockSpec((rows, 1), lambda i, a, b: (i, 0)),
                  pl.BlockSpec((rows, 1), lambda i, a, b: (i, 0)),
                  pl.BlockSpec(memory_space=pl.ANY)],
        out_specs=pl.BlockSpec((rows, D), lambda i, a, b: (i, 0)),
        scratch_shapes=[pltpu.VMEM((rows, D // 2), jnp.uint32), pltpu.VMEM((rows, D // 2), jnp.uint32),
                        pltpu.SemaphoreType.DMA((2,))])
    return pl.pallas_call(
        functools.partial(_combine_kernel, rows), grid_spec=grid_spec,
        out_shape=jax.ShapeDtypeStruct((T, D), F32),
        compiler_params=_cp("arbitrary"), name="moe_combine",
    )(p0, p1, x, mod3, gate[:, 0:1], gate[:, 1:2], ys)


def _route(logits, n_groups, n_experts, tm):
    n_tok = logits.shape[0]
    epg = n_experts // n_groups
    gl = logits[:, :n_groups]
    pg = jax.nn.softmax(gl, axis=-1)
    grp = jnp.argmax(gl, axis=-1)
    pg_sel = jnp.max(pg, axis=-1, keepdims=True)
    el = logits[:, n_groups:n_groups + n_experts].reshape(n_tok, n_groups, epg)
    el = jnp.take_along_axis(el, grp[:, None, None], axis=1)[:, 0]
    top_p, top_i = lax.top_k(jax.nn.softmax(el, axis=-1), TOP_K)
    gate = pg_sel * top_p / jnp.sum(top_p, axis=-1, keepdims=True)
    expert = grp[:, None].astype(jnp.int32) * epg + top_i.astype(jnp.int32)
    flat_e = expert.reshape(-1)
    n_assign = n_tok * TOP_K
    onehot = (flat_e[:, None] == jnp.arange(n_experts, dtype=jnp.int32)[None, :]).astype(jnp.int32)
    csum = jnp.cumsum(onehot, axis=0)
    rank = jnp.take_along_axis(csum, flat_e[:, None], axis=1)[:, 0] - 1
    counts = csum[-1]
    padded = (counts + tm - 1) // tm * tm
    pend = jnp.cumsum(padded)
    pstart = pend - padded
    dest = (pstart[flat_e] + rank).astype(jnp.int32)
    n_blocks = n_assign // tm + n_experts
    n_slots = n_blocks * tm
    slot_tok = jnp.zeros((n_slots,), jnp.int32).at[dest].set(jnp.arange(n_assign, dtype=jnp.int32) // TOP_K)
    block_e = jnp.minimum(jnp.searchsorted(pend, jnp.arange(n_blocks, dtype=jnp.int32) * tm, side='right'),
                          n_experts - 1).astype(jnp.int32)
    n_used = (pend[-1] // tm).astype(jnp.int32).reshape(1)
    blk_start = jnp.arange(n_blocks, dtype=jnp.int32) * tm
    n_valid = jnp.clip((pstart + counts)[block_e] - blk_start, 0, tm).astype(jnp.int32)
    dest2 = dest.reshape(n_tok, TOP_K)
    return slot_tok, gate, block_e, n_valid, n_used, dest2[:, 0], dest2[:, 1]


def _rope_rot_cols(w):
    q = MLA_ROPE // 4
    return jnp.concatenate([-w[..., q:2 * q], w[..., 0:q], -w[..., 3 * q:4 * q], w[..., 2 * q:3 * q]], axis=-1)


def _pad_cols(w, n):
    return jnp.pad(w, ((0, 0), (0, n - w.shape[1])))


def _rope_tables(tok):
    half = MLA_ROPE // 2
    nfreq = half // 2
    inv = ROPE_THETA ** (-jnp.arange(nfreq, dtype=F32) / nfreq)
    t = jnp.arange(tok.dec_seq, dtype=jnp.int32)
    ang_r = (t // GRID_W).astype(F32)[:, None] * inv
    ang_c = (t % GRID_W).astype(F32)[:, None] * inv
    z = jnp.zeros((tok.dec_seq, LANE - MLA_ROPE), F32)
    cos = jnp.concatenate([jnp.cos(ang_r), jnp.cos(ang_r), jnp.cos(ang_c), jnp.cos(ang_c), z], axis=1)
    sin = jnp.concatenate([jnp.sin(ang_r), jnp.sin(ang_r), jnp.sin(ang_c), jnp.sin(ang_c), z], axis=1)
    ctx_cos = jnp.concatenate([jnp.ones((tok.t_ctx, MLA_ROPE), F32), jnp.zeros((tok.t_ctx, LANE - MLA_ROPE), F32)], axis=1)
    cos = jnp.concatenate([ctx_cos] + [cos] * tok.n_lat_seq, axis=0)
    sin = jnp.concatenate([jnp.zeros((tok.t_ctx, LANE), F32)] + [sin] * tok.n_lat_seq, axis=0)
    return cos, sin


def kernel(x_prompt, x_sample, state_gdn, state_rwkv, cache_mla_ckv, cache_mla_krope, c, c_ctx, ada_w, ada_b, norm_mix_w, norm_ffn_w, w_in, gdn_conv_w, gdn_A_log, gdn_dt_bias, gdn_norm_w, rwkv_mu, rwkv_w0, rwkv_w_up, rwkv_a0, rwkv_a_up, rwkv_g_up, rwkv_k_k, rwkv_k_a, rwkv_r_k, rwkv_ln_w, rwkv_ln_b, mla_q_norm_w, mla_q_up, mla_kv_norm_w, mla_kv_up, w_out, router_group_w, router_group_b, router_expert_w, router_expert_b, moe_w_gate, moe_w_up, moe_w_down, final_norm_w):
    B, SEQ, D = x_prompt.shape
    DB, DSEQ, _ = x_sample.shape
    L = w_in.shape[0]
    PAST = cache_mla_ckv.shape[2]
    GH = gdn_A_log.shape[-1]
    GW = GH * GDN_HEAD
    RH = rwkv_r_k.shape[1]
    RW = RH * RWKV_HEAD
    RP = RW // LANE
    DL, AL, GL = rwkv_w_up.shape[2], rwkv_a_up.shape[2], rwkv_g_up.shape[1]
    QR = mla_q_norm_w.shape[1]
    KR = mla_kv_norm_w.shape[1]
    MH = mla_kv_up.shape[2] // (MLA_NOPE + MLA_V)
    NG = router_group_w.shape[2]
    NE = router_expert_w.shape[2]
    assert DL + AL == LANE and GL == LANE and 4 * GH <= LANE and NG + NE <= LANE and 1 + DB <= SUB
    tok = _Tok(B, SEQ, DB, DSEQ)
    T = tok.t
    tm = _pick((256, 128, 64), SEQ, DSEQ)
    tm_moe = 256

    gdn_cols = 4 * GW + 4 * GH
    rwkv_cols = 3 * RW + DL + AL + GL
    o_r = gdn_cols
    o_m = gdn_cols + rwkv_cols

    x = jnp.concatenate([x_prompt.reshape(B * SEQ, D), x_sample.reshape(DB * DSEQ, D)], axis=0)
    cvec = jnp.zeros((SUB, D), F32).at[0].set(c_ctx).at[1:1 + DB].set(c)
    mod = _ada_mod(cvec, ada_w, ada_b)
    cos, sin = _rope_tables(tok)

    new_gdn, new_rwkv, new_ckv, new_kr = [], [], [], []
    for l in range(L):
        mod3 = mod[l].reshape(SUB, 1, 6 * D)
        wl = w_in[l]
        w_gdn = _b(wl[:, :4 * GW])
        w_rwkv = _b(wl[:, o_r:o_r + 3 * RW])
        w_cq = _b(wl[:, o_m:o_m + QR])
        w_ckv = _b(wl[:, o_m + QR:o_m + QR + KR])
        w_kr = wl[:, o_m + QR + KR:o_m + QR + KR + MLA_ROPE]
        w_small = _b(jnp.concatenate([
            _pad_cols(wl[:, 4 * GW:4 * GW + 4 * GH], LANE),
            wl[:, o_r + 3 * RW:o_r + 3 * RW + DL + AL],
            wl[:, o_r + 3 * RW + DL + AL:o_r + rwkv_cols],
            w_kr, _rope_rot_cols(w_kr)], axis=1))

        h1 = _norm_mod(x, norm_mix_w[l], mod3, 0, tok, tm)
        p_gdn = _matmul(h1, w_gdn, name="proj_gdn")
        p_rwkv = _matmul(h1, w_rwkv, name="proj_rwkv")
        p_cq = _matmul(h1, w_cq, name="proj_cq")
        p_ckv = _matmul(h1, w_ckv, name="proj_ckv")
        p_small = _matmul(h1, w_small, name="proj_small")

        gate_par = jnp.zeros((SUB, LANE), F32)
        gate_par = gate_par.at[0, :2 * GH].set(-jnp.exp(gdn_A_log[l].reshape(-1)))
        gate_par = gate_par.at[1, :2 * GH].set(gdn_dt_bias[l].reshape(-1))
        qkv, gates = _gdn_pre(p_gdn, p_small, gdn_conv_w[l].T, gate_par, tok, GH, tm)
        gcol = gates[:, :4 * GH].reshape(T // CHUNK, CHUNK, 4 * GH)
        grow = jnp.swapaxes(gates[:, :2 * GH].reshape(T // CHUNK, CHUNK, 2 * GH), 1, 2).reshape(
            T // CHUNK, GH, 2 * CHUNK)
        oc_f, sc_f = _gdn_chunk(qkv, gcol, grow, 0, B, SEQ, GH, False, emit=True)
        oc_b, sc_b = _gdn_chunk(qkv, gcol, grow, 0, B, SEQ, GH, True, emit=True)
        ol_f, = _gdn_chunk(qkv, gcol, grow, tok.t_ctx, DB, DSEQ, GH, False, s0=state_gdn, s0_idx=(l, 0))
        ol_b, = _gdn_chunk(qkv, gcol, grow, tok.t_ctx, DB, DSEQ, GH, True, s0=state_gdn, s0_idx=(l, 1))
        o_gdn = _gdn_out(jnp.concatenate([oc_f, ol_f], axis=0), jnp.concatenate([oc_b, ol_b], axis=0),
                         p_gdn, gdn_norm_w[l], GH, tm)
        new_gdn.append(jnp.stack([sc_f, sc_b], axis=1))

        ln = (rwkv_ln_w[l].reshape(1, RW), rwkv_ln_b[l].reshape(1, RW), rwkv_r_k[l].reshape(1, RW))
        y_dir, s_dir = [], []
        for d in range(2):
            mu = rwkv_mu[l, d]
            wup = jnp.concatenate([rwkv_w_up[l, d], jnp.zeros((AL, RW), F32)], axis=0)
            aup = jnp.concatenate([jnp.zeros((DL, RW), F32), rwkv_a_up[l, d]], axis=0)
            par = (mu[:RW].reshape(1, RW), mu[RW:2 * RW].reshape(1, RW), mu[2 * RW:3 * RW].reshape(1, RW),
                   mu[3 * RW:].reshape(1, LANE), rwkv_w0[l, d].reshape(1, RW), wup,
                   rwkv_a0[l, d].reshape(1, RW), aup, rwkv_k_k[l].reshape(1, RW), rwkv_k_a[l].reshape(1, RW))
            s0 = state_rwkv[:, l, d].reshape(DB, RP, 2, RWKV_HEAD, RWKV_HEAD)
            s0 = jnp.einsum('bpaij,ac->bpaicj', s0, jnp.eye(2, dtype=F32)).reshape(DB, RP, LANE, LANE)
            yc, sc = _rwkv_chunk(p_rwkv, p_small, par, ln, 0, B, SEQ, d == 1, emit=True)
            yl, = _rwkv_chunk(p_rwkv, p_small, par, ln, tok.t_ctx, DB, DSEQ, d == 1, s0=s0)
            y_dir.append(jnp.concatenate([yc, yl], axis=0))
            sc = sc.reshape(B, RP, 2, RWKV_HEAD, 2, RWKV_HEAD)
            s_dir.append(jnp.einsum('bpaiaj->bpaij', sc).reshape(B, RH, RWKV_HEAD, RWKV_HEAD))
        o_rwkv = _rwkv_out(y_dir[0], y_dir[1], p_small, rwkv_g_up[l], tm)
        new_rwkv.append(jnp.stack(s_dir, axis=1))

        cqn, ckvn, krf = _mla_pre(p_cq, p_ckv, p_small, cos, sin, mla_q_norm_w[l], mla_kv_norm_w[l], tm)
        qu = mla_q_up[l].reshape(QR, MH, MLA_QK)
        q_rope = qu[:, :, MLA_NOPE:]
        w_q = _b(jnp.concatenate([qu[:, :, :MLA_NOPE].reshape(QR, MH * MLA_NOPE),
                                  jnp.concatenate([q_rope, _rope_rot_cols(q_rope)], axis=-1).reshape(QR, MH * LANE)],
                                 axis=1))
        q = _matmul(cqn, w_q, name="q_up")
        kr_cache = _b(jnp.pad(cache_mla_krope[:, l], ((0, 0), (0, 0), (0, LANE - MLA_ROPE))))
        ckv_parts, kr_parts = [], []
        for b in range(DB):
            r0 = tok.t_ctx + b * DSEQ
            ckv_parts += [_b(cache_mla_ckv[b, l]), _b(ckvn[r0:r0 + DSEQ])]
            kr_parts += [kr_cache[b], krf[r0:r0 + DSEQ]]
        ckv_all = jnp.concatenate(ckv_parts + [_b(ckvn[:tok.t_ctx])], axis=0)
        kr_all = jnp.concatenate(kr_parts + [krf[:tok.t_ctx]], axis=0)
        kv = _matmul(ckv_all, _b(mla_kv_up[l]), out_dtype=BF16, name="kv_up")
        o_c = _attention(q, cos, sin, kv, kr_all, 0, DB * (PAST + DSEQ), B, SEQ, SEQ, MH)
        o_l = _attention(q, cos, sin, kv, kr_all, tok.t_ctx, 0, DB, DSEQ, PAST + DSEQ, MH)
        o_mla = jnp.concatenate([o_c, o_l], axis=0)
        new_ckv.append(ckvn[:tok.t_ctx].reshape(B, SEQ, KR))
        new_kr.append(p_small[:tok.t_ctx, 3 * LANE:3 * LANE + MLA_ROPE].reshape(B, SEQ, MLA_ROPE))

        x = _wout_residual(x, mod3, o_gdn, o_rwkv, o_mla, _b(w_out[l]), tok)

        rw = _pad_cols(jnp.concatenate([router_group_w[l], router_expert_w[l]], axis=1), LANE)
        rb = _pad_cols(jnp.concatenate([router_group_b[l], router_expert_b[l]])[None, :], LANE)
        h2, logits = _norm_mod(x, norm_ffn_w[l], mod3, 3, tok, tm, router=(rw, rb))
        slot_tok, gate, block_e, n_valid, n_used, p0, p1 = _route(logits, NG, NE, tm_moe)
        ys = _experts(h2, slot_tok, block_e, n_valid, n_used, moe_w_gate, moe_w_up, moe_w_down, l, tm_moe)
        x = _combine(x, mod3, ys, p0, p1, gate, tok, tm)

    y = _final_norm(x, final_norm_w, tm)
    y_prompt = y[:tok.t_ctx].reshape(B, SEQ, D)
    y_sample = y[tok.t_ctx:].reshape(DB, DSEQ, D)
    return (y_prompt, y_sample, jnp.stack(new_gdn, axis=1), jnp.stack(new_rwkv, axis=1),
            jnp.stack(new_ckv, axis=1), jnp.stack(new_kr, axis=1))
```

```python
import functools

import numpy as np
import jax
import jax.numpy as jnp
from jax import lax
from jax.experimental import pallas as pl
from jax.experimental.pallas import tpu as pltpu

F32 = jnp.float32
BF16 = jnp.bfloat16
HI = lax.Precision.HIGHEST

EPS = 1e-6
L2_EPS = 1e-6
GRID_W = 64
ROPE_THETA = 10000.0
GDN_HEAD = 128
RWKV_HEAD = 64
RWKV_GN_EPS = RWKV_HEAD * 1e-5
MLA_V = 128
MLA_NOPE = 128
MLA_ROPE = 64
MLA_QK = MLA_NOPE + MLA_ROPE
TOP_K = 2
CHUNK = 64
LANE = 128
SUB = 8
DMA_UNROLL = 8
GDN_SEQ_PER_STEP = 4
RWKV_SEQ_PER_STEP = 2
VMEM_LIMIT = 52 * 1024 * 1024


def _cp(*sem, vmem=VMEM_LIMIT):
    return pltpu.CompilerParams(dimension_semantics=sem, vmem_limit_bytes=vmem)


def _pick(prefs, *ns):
    for p in prefs:
        if all(n % p == 0 for n in ns):
            return p
    raise ValueError(f"no tile in {prefs} divides {ns}")


def _dot(a, b, prec=None):
    return lax.dot_general(a, b, (((1,), (0,)), ((), ())), precision=prec, preferred_element_type=F32)


def _dot_nt(a, b, prec=None):
    return lax.dot_general(a, b, (((1,), (1,)), ((), ())), precision=prec, preferred_element_type=F32)


def _dot_tn(a, b, prec=None):
    return lax.dot_general(a, b, (((0,), (0,)), ((), ())), precision=prec, preferred_element_type=F32)


def _b(x):
    return x.astype(BF16)


def _sigmoid(x):
    return 1.0 / (1.0 + jnp.exp(-x))


def _silu(x):
    return x * _sigmoid(x)


def _softplus(x):
    return jnp.maximum(x, 0.0) + jnp.log(1.0 + jnp.exp(-jnp.abs(x)))


def _ada_kernel(c_ref, w_ref, b_ref, o_ref):
    s = _silu(c_ref[...])
    o_ref[...] = _dot(_b(s), _b(w_ref[...])) + b_ref[...]


def _ada_mod(cvec, ada_w, ada_b):
    L, D, N = ada_w.shape
    R = cvec.shape[0]
    tn = _pick((1024, 512, 256, 128), N)
    return pl.pallas_call(
        _ada_kernel,
        grid=(L, N // tn),
        in_specs=[pl.BlockSpec((R, D), lambda l, j: (0, 0)),
                  pl.BlockSpec((None, D, tn), lambda l, j: (l, 0, j)),
                  pl.BlockSpec((None, 1, tn), lambda l, j: (l, 0, j))],
        out_specs=pl.BlockSpec((None, R, tn), lambda l, j: (l, 0, j)),
        out_shape=jax.ShapeDtypeStruct((L, R, N), F32),
        compiler_params=_cp("parallel", "parallel"),
        name="ada_mod",
    )(cvec, ada_w, ada_b.reshape(L, 1, N))


def _norm_kernel(x_ref, w_ref, o_ref):
    x = x_ref[...]
    y = x * lax.rsqrt(jnp.mean(x * x, axis=-1, keepdims=True) + EPS) * w_ref[...]
    o_ref[...] = y.astype(o_ref.dtype)


def _norm_mod_kernel(x_ref, w_ref, sc_ref, sh_ref, o_ref):
    x = x_ref[...]
    y = x * lax.rsqrt(jnp.mean(x * x, axis=-1, keepdims=True) + EPS) * w_ref[...]
    o_ref[...] = (y * (1.0 + sc_ref[...]) + sh_ref[...]).astype(o_ref.dtype)


def _pack_bf16_halves(h):
    half = h.shape[1] // 2
    a = lax.bitcast_convert_type(h[:, :half].astype(BF16).astype(F32), jnp.uint32)
    b = lax.bitcast_convert_type(h[:, half:].astype(BF16).astype(F32), jnp.uint32)
    return a | (b >> 16)


def _unpack_bf16_halves(w):
    a = lax.bitcast_convert_type(w & jnp.uint32(0xFFFF0000), F32).astype(BF16)
    b = lax.bitcast_convert_type(w << 16, F32).astype(BF16)
    return a, b


def _norm_router_kernel(x_ref, w_ref, sc_ref, sh_ref, rw_ref, rb_ref, o_ref, lg_ref):
    x = x_ref[...]
    y = x * lax.rsqrt(jnp.mean(x * x, axis=-1, keepdims=True) + EPS) * w_ref[...]
    h = y * (1.0 + sc_ref[...]) + sh_ref[...]
    o_ref[...] = _pack_bf16_halves(h)
    lg_ref[...] = _dot(h, rw_ref[...], HI) + rb_ref[...]


class _Tok:
    def __init__(self, n_ctx_seq, seq, n_lat_seq, dec_seq):
        self.n_ctx_seq, self.seq, self.n_lat_seq, self.dec_seq = n_ctx_seq, seq, n_lat_seq, dec_seq
        self.t_ctx = n_ctx_seq * seq
        self.t_lat = n_lat_seq * dec_seq
        self.t = self.t_ctx + self.t_lat

    def group_of_tile(self, tm):
        assert self.t_ctx % tm == 0 and self.dec_seq % tm == 0
        nct, tpl = self.t_ctx // tm, self.dec_seq // tm
        return lambda i: jnp.where(i < nct, 0, 1 + (i - nct) // tpl)

    def seq_flags(self, tm):
        assert self.seq % tm == 0 and self.dec_seq % tm == 0
        starts = np.arange(0, self.t, tm)
        first = np.where(starts < self.t_ctx, starts % self.seq == 0, (starts - self.t_ctx) % self.dec_seq == 0)
        ends = starts + tm
        last = np.where(starts < self.t_ctx, ends % self.seq == 0, (ends - self.t_ctx) % self.dec_seq == 0)
        return jnp.asarray(first, jnp.int32), jnp.asarray(last, jnp.int32)


def _norm_mod(x, w, mod3, seg, tok, tm, router=None):
    T, D = x.shape
    grp = tok.group_of_tile(tm)
    in_specs = [pl.BlockSpec((tm, D), lambda i: (i, 0)),
                pl.BlockSpec((1, D), lambda i: (0, 0)),
                pl.BlockSpec((None, 1, D), lambda i: (grp(i), 0, seg + 1)),
                pl.BlockSpec((None, 1, D), lambda i: (grp(i), 0, seg))]
    if router is None:
        return pl.pallas_call(
            _norm_mod_kernel, grid=(T // tm,), in_specs=in_specs,
            out_specs=pl.BlockSpec((tm, D), lambda i: (i, 0)),
            out_shape=jax.ShapeDtypeStruct((T, D), BF16),
            compiler_params=_cp("parallel"), name="norm_mod",
        )(x, w.reshape(1, D), mod3, mod3)
    rw, rb = router
    NR = rw.shape[1]
    return pl.pallas_call(
        _norm_router_kernel, grid=(T // tm,),
        in_specs=in_specs + [pl.BlockSpec((D, NR), lambda i: (0, 0)), pl.BlockSpec((1, NR), lambda i: (0, 0))],
        out_specs=[pl.BlockSpec((tm, D // 2), lambda i: (i, 0)), pl.BlockSpec((tm, NR), lambda i: (i, 0))],
        out_shape=[jax.ShapeDtypeStruct((T, D // 2), jnp.uint32), jax.ShapeDtypeStruct((T, NR), F32)],
        compiler_params=_cp("parallel"), name="norm_router",
    )(x, w.reshape(1, D), mod3, mod3, rw, rb)


def _final_norm(x, w, tm):
    T, D = x.shape
    return pl.pallas_call(
        _norm_kernel, grid=(T // tm,),
        in_specs=[pl.BlockSpec((tm, D), lambda i: (i, 0)), pl.BlockSpec((1, D), lambda i: (0, 0))],
        out_specs=pl.BlockSpec((tm, D), lambda i: (i, 0)),
        out_shape=jax.ShapeDtypeStruct((T, D), F32),
        compiler_params=_cp("parallel"), name="final_norm",
    )(x, w.reshape(1, D))


def _mm_kernel(a_ref, b_ref, o_ref):
    o_ref[...] = _dot(a_ref[...], b_ref[...]).astype(o_ref.dtype)


def _matmul(a, b, out_dtype=F32, name="matmul"):
    M, K = a.shape
    N = b.shape[1]
    tm = _pick((1024, 512, 256, 128, 64, 8), M)
    tn = _pick((1024, 512, 256, 128), N)
    return pl.pallas_call(
        _mm_kernel, grid=(N // tn, M // tm),
        in_specs=[pl.BlockSpec((tm, K), lambda j, i: (i, 0)), pl.BlockSpec((K, tn), lambda j, i: (0, j))],
        out_specs=pl.BlockSpec((tm, tn), lambda j, i: (i, j)),
        out_shape=jax.ShapeDtypeStruct((M, N), out_dtype),
        compiler_params=_cp("parallel", "parallel"), name=name,
    )(a, b)


def _wout_kernel(x_ref, g_ref, a1_ref, a2_ref, a3_ref, w1_ref, w2_ref, w3_ref, o_ref):
    mix = _dot(a1_ref[...], w1_ref[...]) + _dot(a2_ref[...], w2_ref[...]) + _dot(a3_ref[...], w3_ref[...])
    o_ref[...] = x_ref[...] + g_ref[...] * mix


def _wout_residual(x, mod3, a1, a2, a3, w, tok):
    T, D = x.shape
    k1, k2, k3 = a1.shape[1], a2.shape[1], a3.shape[1]
    assert k1 % k2 == 0 and (k1 + k2) % k3 == 0
    tm = _pick((512, 256, 128), tok.t_ctx, tok.dec_seq)
    tn = _pick((1024, 512, 256, 128), D)
    grp = tok.group_of_tile(tm)
    nj = D // tn
    return pl.pallas_call(
        _wout_kernel, grid=(nj, T // tm),
        in_specs=[pl.BlockSpec((tm, tn), lambda j, i: (i, j)),
                  pl.BlockSpec((None, 1, tn), lambda j, i: (grp(i), 0, 2 * nj + j)),
                  pl.BlockSpec((tm, k1), lambda j, i: (i, 0)),
                  pl.BlockSpec((tm, k2), lambda j, i: (i, 0)),
                  pl.BlockSpec((tm, k3), lambda j, i: (i, 0)),
                  pl.BlockSpec((k1, tn), lambda j, i: (0, j)),
                  pl.BlockSpec((k2, tn), lambda j, i: (k1 // k2, j)),
                  pl.BlockSpec((k3, tn), lambda j, i: ((k1 + k2) // k3, j))],
        out_specs=pl.BlockSpec((tm, tn), lambda j, i: (i, j)),
        out_shape=jax.ShapeDtypeStruct((T, D), F32),
        compiler_params=_cp("parallel", "parallel"), name="wout_residual",
    )(x, mod3, a1, a2, a3, w, w, w)


def _split_bf16(x, terms):
    out = []
    for _ in range(terms):
        p = x.astype(BF16)
        out.append(p)
        x = x - p.astype(F32)
    return out


def _dot_exact_r(a01, x, terms=3):
    return sum(_dot(a01, p) for p in _split_bf16(x, terms))


def _dot_exact_l(x, b01, terms=3):
    return sum(_dot(p, b01) for p in _split_bf16(x, terms))


def _dot_nt_exact_l(x, b01, terms=3):
    return sum(_dot_nt(p, b01) for p in _split_bf16(x, terms))


def _one_bf16(mask):
    return jnp.where(mask, 1.0, 0.0).astype(BF16)


def _pair_masks(C, rev):
    M = 2 * C
    ti = lax.broadcasted_iota(jnp.int32, (M, M), 0)
    si = lax.broadcasted_iota(jnp.int32, (M, M), 1)
    shift = C.bit_length() - 1
    assert (1 << shift) == C
    same = jnp.right_shift(ti, shift) == jnp.right_shift(si, shift)
    before = (si > ti) if rev else (si < ti)
    strict = same & before
    incl = same & (before | (ti == si))
    tc = lax.broadcasted_iota(jnp.int32, (C, C), 0)
    sc = lax.broadcasted_iota(jnp.int32, (C, C), 1)
    cum = _one_bf16((sc >= tc) if rev else (sc <= tc))
    return ti, si, shift, same, strict, incl, cum


def _tri_inv_n(a_list, ti, si, top_shift):
    def same(shift):
        return jnp.right_shift(ti, shift) == jnp.right_shift(si, shift)

    n_mat = range(len(a_list))
    m3 = same(3)
    ad = [jnp.where(m3, a, 0.0) for a in a_list]
    adb = [_b(x) for x in ad]
    b2 = [_dot(x, x) for x in adb]
    b2b = [_b(x) for x in b2]
    n1 = [b2[i] - ad[i] - _dot(adb[i], b2b[i]) for i in n_mat]
    b4 = [_dot(x, x) for x in b2b]
    n = [n1[i] + b4[i] + _dot(_b(n1[i]), _b(b4[i])) for i in n_mat]
    for shift in range(3, top_shift):
        moff = same(shift + 1) & jnp.logical_not(same(shift))
        aoff = [jnp.where(moff, a, 0.0) for a in a_list]
        aob = [_b(x) for x in aoff]
        nb = [_b(x) for x in n]
        x = [aoff[i] + _dot(nb[i], aob[i]) for i in n_mat]
        n = [n[i] - x[i] - _dot(_b(x[i]), nb[i]) for i in n_mat]
    return n


def _shift_rows(x, halo_prev_ref, halo_next_ref, first, last, direction):
    tm = x.shape[0]
    rows = lax.broadcasted_iota(jnp.int32, (tm, 1), 0)
    if direction == 0:
        edge = halo_prev_ref[SUB - 1:SUB, :] * (1.0 - first)
        return jnp.where(rows == 0, edge, pltpu.roll(x, 1, 0))
    edge = halo_next_ref[0:1, :] * (1.0 - last)
    return jnp.where(rows == tm - 1, edge, pltpu.roll(x, tm - 1, 0))


def _halo_specs(tm, width, col, n_rows):
    r = tm // SUB
    nb = n_rows // SUB
    prev = pl.BlockSpec((SUB, width), lambda i, f, l: (jnp.maximum(i * r - 1, 0), col))
    nxt = pl.BlockSpec((SUB, width), lambda i, f, l: (jnp.minimum((i + 1) * r, nb - 1), col))
    return prev, nxt


def _gdn_pre_kernel(n_heads, first_ref, last_ref, x_ref, xp_ref, xn_ref, cw_ref, ab_ref, gp_ref,
                    qkv_ref, gates_ref):
    i = pl.program_id(0)
    first = first_ref[i].astype(F32)
    last = last_ref[i].astype(F32)
    x = x_ref[...]
    xm = _shift_rows(x, xp_ref, xn_ref, first, last, 0)
    xq = _shift_rows(x, xp_ref, xn_ref, first, last, 1)
    y = _silu(xm * cw_ref[0:1, :] + x * cw_ref[1:2, :] + xq * cw_ref[2:3, :])
    for hd in range(3 * n_heads):
        seg = y[:, hd * GDN_HEAD:(hd + 1) * GDN_HEAD]
        if hd < 2 * n_heads:
            seg = seg * lax.rsqrt(jnp.sum(seg * seg, axis=-1, keepdims=True) + L2_EPS)
            if hd < n_heads:
                seg = seg * (GDN_HEAD ** -0.5)
        qkv_ref[:, hd * GDN_HEAD:(hd + 1) * GDN_HEAD] = seg
    ab = ab_ref[...]
    lane = lax.broadcasted_iota(jnp.int32, ab.shape, 1)
    g = gp_ref[0:1, :] * _softplus(ab + gp_ref[1:2, :])
    gates_ref[...] = jnp.where(lane < 2 * n_heads, g, _sigmoid(ab))


def _gdn_pre(p_gdn, p_small, conv_w3, gate_par, tok, n_heads, tm):
    T = p_gdn.shape[0]
    W3 = 3 * n_heads * GDN_HEAD
    first, last = tok.seq_flags(tm)
    hp, hn = _halo_specs(tm, W3, 0, T)
    grid_spec = pltpu.PrefetchScalarGridSpec(
        num_scalar_prefetch=2, grid=(T // tm,),
        in_specs=[pl.BlockSpec((tm, W3), lambda i, f, l: (i, 0)), hp, hn,
                  pl.BlockSpec((3, W3), lambda i, f, l: (0, 0)),
                  pl.BlockSpec((tm, LANE), lambda i, f, l: (i, 0)),
                  pl.BlockSpec((SUB, LANE), lambda i, f, l: (0, 0))],
        out_specs=[pl.BlockSpec((tm, W3), lambda i, f, l: (i, 0)),
                   pl.BlockSpec((tm, LANE), lambda i, f, l: (i, 0))])
    return pl.pallas_call(
        functools.partial(_gdn_pre_kernel, n_heads), grid_spec=grid_spec,
        out_shape=[jax.ShapeDtypeStruct((T, W3), F32), jax.ShapeDtypeStruct((T, LANE), F32)],
        compiler_params=_cp("parallel"), name="gdn_pre",
    )(first, last, p_gdn, p_gdn, p_gdn, conv_w3, p_small, gate_par)


def _gdn_chunk_kernel(n_heads, n_grp, rev, has_s0, emit, *refs):
    refs = list(refs)
    G = n_grp
    in_refs = [refs[5 * g:5 * g + 5] for g in range(G)]
    pos = 5 * G
    s0_ref = None
    if has_s0:
        s0_ref = refs[pos]
        pos += 1
    o_ref = refs[pos]
    pos += 1
    so_ref = None
    if emit:
        so_ref = refs[pos]
        pos += 1
    s_scr = refs[pos]

    n = pl.program_id(1)
    n_last = pl.num_programs(1) - 1
    C = in_refs[0][0].shape[0]
    H = n_heads
    d = 1 if rev else 0

    @pl.when(n == 0)
    def _():
        if has_s0:
            s_scr[...] = s0_ref[...]
        else:
            s_scr[...] = jnp.zeros(s_scr.shape, F32)

    M = 2 * C
    HP = H // 2
    assert H % 2 == 0 and M == LANE
    ti, si, shift, same, strict, incl, cum = _pair_masks(C, rev)
    incl_b = _one_bf16(incl)
    ncol = in_refs[0][3].shape[1]
    er = lax.broadcasted_iota(jnp.int32, (ncol, 2 * H * LANE), 0)
    ec = lax.broadcasted_iota(jnp.int32, (ncol, 2 * H * LANE), 1) // LANE
    want = jnp.where(ec < H, d * H + ec, 2 * H + d * H + (ec - H))
    expand = _one_bf16(er == want)
    last_row = 0 if rev else C - 1

    def sl(h):
        return slice(h * GDN_HEAD, (h + 1) * GDN_HEAD)

    def pair_rows(x, p):
        return jnp.concatenate([x[:, sl(2 * p)], x[:, sl(2 * p + 1)]], axis=0)

    arr = []
    for g in range(G):
        q_ref, k_ref, v_ref, gcol_ref, grow_ref = in_refs[g]
        gb_all = _dot_exact_l(gcol_ref[...], expand)
        gcb = _dot_exact_r(cum, gb_all[:, :H * LANE])
        beta = gb_all[:, H * LANE:]
        gc_rows = _dot_nt_exact_l(grow_ref[d * HP:(d + 1) * HP, :], incl_b)
        q = q_ref[...]
        k = k_ref[...]
        egc = jnp.exp(gcb)
        gl = gcb[last_row:last_row + 1, :]
        kb = k * beta
        arr.append(dict(q=q, k=k, gcb=gcb, gc_rows=gc_rows, kb=kb, vbeta=v_ref[...] * beta, kbe=kb * egc,
                        qd=q * egc, kd=k * jnp.exp(gl - gcb), sdec=jnp.exp(gl)))

    pair_units = [(g, p) for g in range(G) for p in range(HP)]
    head_units = [(g, h) for g in range(G) for h in range(H)]
    pu = range(len(pair_units))

    def pidx(g, p):
        return g * HP + p

    k_st = [_b(pair_rows(arr[g]['k'], p)) for g, p in pair_units]
    lhs = [_b(jnp.concatenate([pair_rows(arr[g]['kb'], p), pair_rows(arr[g]['q'], p)], axis=0)) for g, p in pair_units]
    sc = [_dot_nt(lhs[u], k_st[u]) for u in pu]
    decay = [jnp.exp(jnp.where(incl, pair_rows(arr[g]['gcb'], p)
                               - jnp.broadcast_to(arr[g]['gc_rows'][p:p + 1, :], (M, M)), -jnp.inf))
             for g, p in pair_units]
    a_mat = [jnp.where(strict, sc[u][:M] * decay[u], 0.0) for u in pu]
    attn_b = [_b(sc[u][M:] * decay[u]) for u in pu]
    n_inv = _tri_inv_n(a_mat, ti, si, shift)
    rhs = [jnp.concatenate([pair_rows(arr[g]['vbeta'], p), pair_rows(arr[g]['kbe'], p)], axis=1)
           for g, p in pair_units]
    sol = [rhs[u] + _dot(_b(n_inv[u]), _b(rhs[u])) for u in pu]
    s_old = [s_scr[g, h] for g, h in head_units]
    s_b = [_b(s) for s in s_old]
    ws = [_dot(_b(jnp.concatenate([sol[pidx(g, h // 2)][(h % 2) * C:(h % 2 + 1) * C, GDN_HEAD:],
                                   arr[g]['qd'][:, sl(h)]], axis=0)), s_b[g * H + h])
          for g, h in head_units]
    v_new = [sol[u][:, :GDN_HEAD] - jnp.concatenate([ws[g * H + 2 * p][:C], ws[g * H + 2 * p + 1][:C]], axis=0)
             for u, (g, p) in enumerate(pair_units)]
    vn_b = [_b(x) for x in v_new]
    o = [jnp.concatenate([ws[g * H + 2 * p][C:], ws[g * H + 2 * p + 1][C:]], axis=0) + _dot(attn_b[u], vn_b[u])
         for u, (g, p) in enumerate(pair_units)]
    for u, (g, p) in enumerate(pair_units):
        o_ref[g, :, sl(2 * p)] = o[u][:C]
        o_ref[g, :, sl(2 * p + 1)] = o[u][C:]
    for g, h in head_units:
        upd = _dot_tn(_b(arr[g]['kd'][:, sl(h)]), vn_b[pidx(g, h // 2)][(h % 2) * C:(h % 2 + 1) * C])
        s_scr[g, h] = s_old[g * H + h] * arr[g]['sdec'][:, sl(h)] + upd

    if emit:
        @pl.when(n == n_last)
        def _():
            so_ref[...] = s_scr[...]


def _gdn_chunk(qkv, gcol, grow, row0, n_seq, seq_len, n_heads, rev, s0=None, s0_idx=None, emit=False):
    T = qkv.shape[0]
    C = CHUNK
    N = seq_len // C
    H = n_heads
    W = H * GDN_HEAD
    b0 = row0 // C

    G = min(GDN_SEQ_PER_STEP, n_seq)
    assert n_seq % G == 0

    def chunk(n):
        return (N - 1 - n) if rev else n

    def blk(s, n, g):
        return b0 + (s * G + g) * N + chunk(n)

    in_specs, args = [], []
    for g in range(G):
        in_specs += [pl.BlockSpec((C, W), lambda s, n, g=g: (blk(s, n, g), 0)),
                     pl.BlockSpec((C, W), lambda s, n, g=g: (blk(s, n, g), 1)),
                     pl.BlockSpec((C, W), lambda s, n, g=g: (blk(s, n, g), 2)),
                     pl.BlockSpec((None, C, gcol.shape[2]), lambda s, n, g=g: (blk(s, n, g), 0, 0)),
                     pl.BlockSpec((None, grow.shape[1], grow.shape[2]), lambda s, n, g=g: (blk(s, n, g), 0, 0))]
        args += [qkv, qkv, qkv, gcol, grow]
    if s0 is not None:
        l, d = s0_idx
        in_specs.append(pl.BlockSpec((G, None, None, H, GDN_HEAD, GDN_HEAD), lambda s, n: (s, l, d, 0, 0, 0)))
        args.append(s0)
    out_specs = [pl.BlockSpec((G, C, W), lambda s, n: (s, chunk(n), 0))]
    out_shape = [jax.ShapeDtypeStruct((n_seq, seq_len, W), F32)]
    if emit:
        out_specs.append(pl.BlockSpec((G, H, GDN_HEAD, GDN_HEAD), lambda s, n: (s, 0, 0, 0)))
        out_shape.append(jax.ShapeDtypeStruct((n_seq, H, GDN_HEAD, GDN_HEAD), F32))
    res = pl.pallas_call(
        functools.partial(_gdn_chunk_kernel, H, G, rev, s0 is not None, emit),
        grid=(n_seq // G, N), in_specs=in_specs, out_specs=out_specs, out_shape=out_shape,
        scratch_shapes=[pltpu.VMEM((G, H, GDN_HEAD, GDN_HEAD), F32)],
        compiler_params=_cp("parallel", "arbitrary"), name="gdn_chunk",
    )(*args)
    res = list(res)
    res[0] = res[0].reshape(n_seq * seq_len, W)
    return res


def _gdn_out_kernel(n_heads, of_ref, ob_ref, z_ref, nw_ref, o_ref):
    for h in range(n_heads):
        sl = slice(h * GDN_HEAD, (h + 1) * GDN_HEAD)
        o = of_ref[:, sl] + ob_ref[:, sl]
        y = o * lax.rsqrt(jnp.mean(o * o, axis=-1, keepdims=True) + EPS) * nw_ref[...]
        o_ref[:, sl] = (y * _silu(z_ref[:, sl])).astype(o_ref.dtype)


def _gdn_out(o_f, o_b, p_gdn, norm_w, n_heads, tm):
    T, W = o_f.shape
    return pl.pallas_call(
        functools.partial(_gdn_out_kernel, n_heads), grid=(T // tm,),
        in_specs=[pl.BlockSpec((tm, W), lambda i: (i, 0)), pl.BlockSpec((tm, W), lambda i: (i, 0)),
                  pl.BlockSpec((tm, W), lambda i: (i, 3)), pl.BlockSpec((1, GDN_HEAD), lambda i: (0, 0))],
        out_specs=pl.BlockSpec((tm, W), lambda i: (i, 0)),
        out_shape=jax.ShapeDtypeStruct((T, W), BF16),
        compiler_params=_cp("parallel"), name="gdn_out",
    )(o_f, o_b, p_gdn, norm_w.reshape(1, GDN_HEAD))


def _pair_ones():
    r = lax.broadcasted_iota(jnp.int32, (LANE, LANE), 0) // RWKV_HEAD
    c = lax.broadcasted_iota(jnp.int32, (LANE, LANE), 1) // RWKV_HEAD
    return r == c


def _rwkv_features(rev, at_edge, in_ref, par_refs, ones_b):
    r_ref, k_ref, v_ref, f_ref, rh_ref, kh_ref, vh_ref, fh_ref = in_ref
    mu_r_ref, mu_k_ref, mu_v_ref, mu_f_ref, w0_ref, wup_ref, a0_ref, aup_ref, kkw_ref, kaw_ref = par_refs
    C = r_ref.shape[0]
    rows = lax.broadcasted_iota(jnp.int32, (C, 1), 0)
    keep = 1.0 - at_edge

    def mix(x_ref, h_ref, mu_ref):
        x = x_ref[...]
        if rev:
            sh = jnp.where(rows == C - 1, h_ref[0:1, :] * keep, pltpu.roll(x, C - 1, 0))
        else:
            sh = jnp.where(rows == 0, h_ref[SUB - 1:SUB, :] * keep, pltpu.roll(x, 1, 0))
        return x + (sh - x) * mu_ref[...]

    r = mix(r_ref, rh_ref, mu_r_ref)
    k = mix(k_ref, kh_ref, mu_k_ref)
    v = mix(v_ref, vh_ref, mu_v_ref)
    f = mix(f_ref, fh_ref, mu_f_ref)
    w_lin = w0_ref[...] + _dot(_b(jnp.tanh(f)), _b(wup_ref[...]))
    lw = -jnp.exp(-_softplus(-w_lin) - 0.5)
    a = _sigmoid(a0_ref[...] + _dot(_b(f), _b(aup_ref[...])))
    kraw = k * kkw_ref[...]
    sq = kraw * kraw
    ss = jnp.concatenate([_dot_exact_l(sq[:, p * LANE:(p + 1) * LANE], ones_b, 2)
                          for p in range(kraw.shape[1] // LANE)], axis=1)
    kk = kraw * lax.rsqrt(ss + L2_EPS)
    return r, k * (1.0 + (a - 1.0) * kaw_ref[...]), v, kk, kk * a, lw


def _rwkv_chunk_kernel(n_pairs, n_grp, rev, has_s0, emit, *refs):
    refs = list(refs)
    G = n_grp
    in_refs = [refs[8 * g:8 * g + 8] for g in range(G)]
    pos = 8 * G
    par_refs = refs[pos:pos + 10]
    pos += 10
    lnw_ref, lnb_ref, rk_ref = refs[pos:pos + 3]
    pos += 3
    s0_ref = None
    if has_s0:
        s0_ref = refs[pos]
        pos += 1
    y_ref = refs[pos]
    pos += 1
    so_ref = None
    if emit:
        so_ref = refs[pos]
        pos += 1
    s_scr = refs[pos]

    n = pl.program_id(1)
    n_last = pl.num_programs(1) - 1
    C = in_refs[0][0].shape[0]

    @pl.when(n == 0)
    def _():
        if has_s0:
            s_scr[...] = s0_ref[...]
        else:
            s_scr[...] = jnp.zeros(s_scr.shape, F32)

    M = 2 * C
    assert M == LANE and RWKV_HEAD == C
    ti, si, shift, same, strict, incl, cum = _pair_masks(C, rev)
    ones_b = _one_bf16(same)
    last_row = 0 if rev else C - 1
    inv_n = 1.0 / RWKV_HEAD
    pairs = range(n_pairs)
    sls = [slice(p * LANE, (p + 1) * LANE) for p in pairs]

    def st2(x):
        return jnp.concatenate([x, x], axis=0)

    def fold(x):
        return x[:C] + x[C:]

    def seg_sum(x):
        return _dot_exact_l(x, ones_b, 2)

    at_edge = (n == 0).astype(F32)
    arr = []
    for g in range(G):
        r, k, v, kk, b, lw = _rwkv_features(rev, at_edge, in_refs[g], par_refs, ones_b)
        cl = _dot_exact_r(cum, lw)
        cll = cl[last_row:last_row + 1, :]
        e_neg = jnp.exp(-cl)
        e_end = jnp.exp(cll - cl)
        arr.append(dict(r=r, k=k, v=v, rtb=_b(r * jnp.exp(cl)), ktb=_b(k * e_neg), btb=_b(b * e_neg),
                        atb=_b(-kk * jnp.exp(cl - lw)), keb=_b(k * e_end), beb=_b(b * e_end),
                        vb=_b(v), s_dec=jnp.exp(cll)))

    units = [(g, p) for g in range(G) for p in pairs]
    un = range(len(units))

    def col(name, u):
        g, p = units[u]
        return arr[g][name][:, sls[p]]

    s_old = [s_scr[g, p] for g, p in units]
    s_b = [_b(s) for s in s_old]
    x0 = [_dot_nt(jnp.concatenate([col('atb', u), col('rtb', u)], axis=0), s_b[u]) for u in un]
    la = [jnp.concatenate([st2(col('atb', u)) * ones_b, st2(col('rtb', u)) * ones_b], axis=0) for u in un]
    sk = [_dot_nt(la[u], st2(col('ktb', u))) for u in un]
    sb = [_dot_nt(la[u], st2(col('btb', u))) for u in un]
    a_ak = [_b(jnp.where(strict, x[:M], 0.0)) for x in sk]
    r_k = [_b(jnp.where(incl, x[M:], 0.0)) for x in sk]
    a_neg = [jnp.where(strict, -x[:M], 0.0) for x in sb]
    r_b = [_b(jnp.where(incl, x[M:], 0.0)) for x in sb]
    n_inv = _tri_inv_n(a_neg, ti, si, shift)
    v_st = [st2(col('vb', u)) * ones_b for u in un]
    wm = [x0[u][:C] + fold(_dot(a_ak[u], v_st[u])) for u in un]
    uu = [wm[u] + fold(_dot(_b(n_inv[u]), st2(_b(wm[u])) * ones_b)) for u in un]
    u_b = [_b(x) for x in uu]
    u_st = [st2(x) * ones_b for x in u_b]
    y = [x0[u][C:] + fold(_dot(jnp.concatenate([r_k[u], r_b[u]], axis=1),
                               jnp.concatenate([v_st[u], u_st[u]], axis=0))) for u in un]
    upd = [_dot_tn(jnp.concatenate([col('vb', u), u_b[u]], axis=0),
                   jnp.concatenate([col('keb', u), col('beb', u)], axis=0)) for u in un]
    for u, (g, p) in enumerate(units):
        s_scr[g, p] = s_old[u] * col('s_dec', u) + jnp.where(same, upd[u], 0.0)
    for u, (g, p) in enumerate(units):
        sl = sls[p]
        mu = seg_sum(y[u]) * inv_n
        yc = y[u] - mu
        var = seg_sum(yc * yc) * inv_n
        yn = yc * lax.rsqrt(var + RWKV_GN_EPS) * lnw_ref[:, sl] + lnb_ref[:, sl]
        bonus = seg_sum(col('r', u) * col('k', u) * rk_ref[:, sl])
        y_ref[g, :, sl] = yn + bonus * col('v', u)

    if emit:
        @pl.when(n == n_last)
        def _():
            so_ref[...] = s_scr[...]


def _rwkv_chunk(p_rwkv, p_small, par, ln, row0, n_seq, seq_len, rev, s0=None, emit=False):
    T, W3 = p_rwkv.shape
    W = W3 // 3
    P = W // LANE
    C = CHUNK
    N = seq_len // C
    b0 = row0 // C
    cs = C // SUB
    n_sub = T // SUB

    G = min(RWKV_SEQ_PER_STEP, n_seq)
    assert n_seq % G == 0

    def chunk(n):
        return (N - 1 - n) if rev else n

    def blk(s, n, g):
        return b0 + (s * G + g) * N + chunk(n)

    def halo(s, n, g):
        if rev:
            return jnp.minimum((blk(s, n, g) + 1) * cs, n_sub - 1)
        return jnp.maximum(blk(s, n, g) * cs - 1, 0)

    in_specs, args = [], []
    for g in range(G):
        in_specs += [pl.BlockSpec((C, W), lambda s, n, g=g, c=c: (blk(s, n, g), c)) for c in range(3)]
        in_specs.append(pl.BlockSpec((C, LANE), lambda s, n, g=g: (blk(s, n, g), 1)))
        in_specs += [pl.BlockSpec((SUB, W), lambda s, n, g=g, c=c: (halo(s, n, g), c)) for c in range(3)]
        in_specs.append(pl.BlockSpec((SUB, LANE), lambda s, n, g=g: (halo(s, n, g), 1)))
        args += [p_rwkv, p_rwkv, p_rwkv, p_small, p_rwkv, p_rwkv, p_rwkv, p_small]
    row = lambda width: pl.BlockSpec((1, width), lambda s, n: (0, 0))
    full = lambda a, b: pl.BlockSpec((a, b), lambda s, n: (0, 0))
    in_specs += [row(W), row(W), row(W), row(LANE), row(W), full(LANE, W), row(W), full(LANE, W), row(W), row(W)]
    args += list(par)
    in_specs += [row(W) for _ in range(3)]
    args += list(ln)
    if s0 is not None:
        in_specs.append(pl.BlockSpec((G, P, LANE, LANE), lambda s, n: (s, 0, 0, 0)))
        args.append(s0)
    out_specs = [pl.BlockSpec((G, C, W), lambda s, n: (s, chunk(n), 0))]
    out_shape = [jax.ShapeDtypeStruct((n_seq, seq_len, W), F32)]
    if emit:
        out_specs.append(pl.BlockSpec((G, P, LANE, LANE), lambda s, n: (s, 0, 0, 0)))
        out_shape.append(jax.ShapeDtypeStruct((n_seq, P, LANE, LANE), F32))
    res = list(pl.pallas_call(
        functools.partial(_rwkv_chunk_kernel, P, G, rev, s0 is not None, emit),
        grid=(n_seq // G, N), in_specs=in_specs, out_specs=out_specs, out_shape=out_shape,
        scratch_shapes=[pltpu.VMEM((G, P, LANE, LANE), F32)],
        compiler_params=_cp("parallel", "arbitrary"), name="rwkv_chunk",
    )(*args))
    res[0] = res[0].reshape(n_seq * seq_len, W)
    return res


def _rwkv_out_kernel(yf_ref, yb_ref, fg_ref, gup_ref, o_ref):
    g = _dot(_b(_sigmoid(fg_ref[...])), _b(gup_ref[...]))
    o_ref[...] = ((yf_ref[...] + yb_ref[...]) * g).astype(o_ref.dtype)


def _rwkv_out(y_f, y_b, p_small, g_up, tm):
    T, W = y_f.shape
    return pl.pallas_call(
        _rwkv_out_kernel, grid=(T // tm,),
        in_specs=[pl.BlockSpec((tm, W), lambda i: (i, 0)), pl.BlockSpec((tm, W), lambda i: (i, 0)),
                  pl.BlockSpec((tm, LANE), lambda i: (i, 2)), pl.BlockSpec((LANE, W), lambda i: (0, 0))],
        out_specs=pl.BlockSpec((tm, W), lambda i: (i, 0)),
        out_shape=jax.ShapeDtypeStruct((T, W), BF16),
        compiler_params=_cp("parallel"), name="rwkv_out",
    )(y_f, y_b, p_small, g_up)


def _rope128(x, cos, sin):
    return x * cos + pltpu.roll(x, MLA_ROPE, 1) * sin


def _mla_pre_kernel(cq_ref, ckv_ref, kr_ref, cos_ref, sin_ref, qw_ref, kvw_ref, cqn_ref, ckvn_ref, krf_ref):
    cq = cq_ref[...]
    cqn_ref[...] = (cq * lax.rsqrt(jnp.mean(cq * cq, axis=-1, keepdims=True) + EPS) * qw_ref[...]).astype(BF16)
    ckv = ckv_ref[...]
    ckvn_ref[...] = ckv * lax.rsqrt(jnp.mean(ckv * ckv, axis=-1, keepdims=True) + EPS) * kvw_ref[...]
    krf_ref[...] = _rope128(kr_ref[...], cos_ref[...], sin_ref[...]).astype(BF16)


def _mla_pre(p_cq, p_ckv, p_small, cos, sin, qw, kvw, tm):
    T, QR = p_cq.shape
    KR = p_ckv.shape[1]
    return pl.pallas_call(
        _mla_pre_kernel, grid=(T // tm,),
        in_specs=[pl.BlockSpec((tm, QR), lambda i: (i, 0)), pl.BlockSpec((tm, KR), lambda i: (i, 0)),
                  pl.BlockSpec((tm, LANE), lambda i: (i, 3)),
                  pl.BlockSpec((tm, LANE), lambda i: (i, 0)), pl.BlockSpec((tm, LANE), lambda i: (i, 0)),
                  pl.BlockSpec((1, QR), lambda i: (0, 0)), pl.BlockSpec((1, KR), lambda i: (0, 0))],
        out_specs=[pl.BlockSpec((tm, QR), lambda i: (i, 0)), pl.BlockSpec((tm, KR), lambda i: (i, 0)),
                   pl.BlockSpec((tm, LANE), lambda i: (i, 0))],
        out_shape=[jax.ShapeDtypeStruct((T, QR), BF16), jax.ShapeDtypeStruct((T, KR), F32),
                   jax.ShapeDtypeStruct((T, LANE), BF16)],
        compiler_params=_cp("parallel"), name="mla_pre",
    )(p_cq, p_ckv, p_small, cos, sin, qw.reshape(1, QR), kvw.reshape(1, KR))


def _attn_kernel(scale, qn_ref, qr_ref, cos_ref, sin_ref, kv_ref, kr_ref, o_ref, k_scr):
    @pl.when(pl.program_id(2) == 0)
    def _():
        for j in range(2):
            k_scr[j] = jnp.concatenate([kv_ref[:, (2 * j) * LANE:(2 * j + 1) * LANE], kr_ref[...]], axis=1)

    two = range(2)
    q = []
    for j in two:
        qr = _rope128(qr_ref[:, j * LANE:(j + 1) * LANE], cos_ref[...], sin_ref[...])
        q.append(_b(jnp.concatenate([qn_ref[:, j * LANE:(j + 1) * LANE], qr], axis=1) * scale))
    s = [_dot_nt(q[j], k_scr[j]) for j in two]
    m = [jnp.max(s[j], axis=-1, keepdims=True) for j in two]
    p = [jnp.exp(s[j] - m[j]) for j in two]
    l = [jnp.sum(p[j], axis=-1, keepdims=True) for j in two]
    for j in two:
        v = kv_ref[:, (2 * j + 1) * LANE:(2 * j + 2) * LANE]
        o_ref[:, j * LANE:(j + 1) * LANE] = (_dot(_b(p[j]), v) / l[j]).astype(o_ref.dtype)


def _attention(q, cos, sin, kv, kr, q_row0, k_row0, n_seq, tq_len, tk_len, n_heads):
    H = n_heads
    tq = _pick((256, 128), tq_len)
    assert q_row0 % tq == 0 and k_row0 % tk_len == 0
    nq = tq_len // tq
    qb0 = q_row0 // tq
    kb0 = k_row0 // tk_len
    scale = MLA_QK ** -0.5
    assert H % 2 == 0
    W2 = 2 * LANE
    return pl.pallas_call(
        functools.partial(_attn_kernel, scale),
        grid=(n_seq, H // 2, nq),
        in_specs=[pl.BlockSpec((tq, W2), lambda s, h, i: (qb0 + s * nq + i, h)),
                  pl.BlockSpec((tq, W2), lambda s, h, i: (qb0 + s * nq + i, H // 2 + h)),
                  pl.BlockSpec((tq, LANE), lambda s, h, i: (qb0 + s * nq + i, 0)),
                  pl.BlockSpec((tq, LANE), lambda s, h, i: (qb0 + s * nq + i, 0)),
                  pl.BlockSpec((tk_len, 2 * W2), lambda s, h, i: (kb0 + s, h)),
                  pl.BlockSpec((tk_len, LANE), lambda s, h, i: (kb0 + s, 0))],
        out_specs=pl.BlockSpec((tq, W2), lambda s, h, i: (s * nq + i, h)),
        out_shape=jax.ShapeDtypeStruct((n_seq * tq_len, H * MLA_V), BF16),
        scratch_shapes=[pltpu.VMEM((2, tk_len, W2), BF16)],
        compiler_params=_cp("parallel", "parallel", "arbitrary"), name="mla_attention",
    )(q, q, cos, sin, kv, kr)


def _row_copy(src_ref, row, dst_ref, slot, r, sem):
    return pltpu.make_async_copy(src_ref.at[pl.ds(row, 1), :], dst_ref.at[slot, pl.ds(r, 1), :], sem.at[slot])


def _expert_up_kernel(rows, tok_ref, be_ref, nvalid_ref, nused_ref, h_ref, wg_ref, wu_ref, hid_ref, x_buf, sem):
    i = pl.program_id(0)
    n_used = nused_ref[0]
    slot = lax.rem(i, 2)

    def row_groups(blk, fn):
        def group(g, c):
            @pl.when(g * DMA_UNROLL < nvalid_ref[blk])
            def _():
                for j in range(DMA_UNROLL):
                    fn(g * DMA_UNROLL + j)
            return c

        lax.fori_loop(0, rows // DMA_UNROLL, group, 0)

    def gather(blk, dst_slot):
        row_groups(blk, lambda r: _row_copy(h_ref, tok_ref[blk * rows + r], x_buf, dst_slot, r, sem).start())

    @pl.when(i == 0)
    def _():
        x_buf[...] = jnp.zeros(x_buf.shape, x_buf.dtype)

    @pl.when(jnp.logical_and(i == 0, n_used > 0))
    def _():
        gather(0, 0)

    @pl.when(i + 1 < n_used)
    def _():
        gather(i + 1, 1 - slot)

    @pl.when(i < n_used)
    def _():
        row_groups(i, lambda r: _row_copy(h_ref, 0, x_buf, slot, r, sem).wait())
        xa, xb = _unpack_bf16_halves(x_buf[slot])
        half = xa.shape[1]
        g = _dot(xa, _b(wg_ref[:half, :])) + _dot(xb, _b(wg_ref[half:, :]))
        u = _dot(xa, _b(wu_ref[:half, :])) + _dot(xb, _b(wu_ref[half:, :]))
        hid_ref[...] = (_silu(g) * u).astype(hid_ref.dtype)

    @pl.when(i >= n_used)
    def _():
        hid_ref[...] = jnp.zeros(hid_ref.shape, hid_ref.dtype)


def _expert_down_kernel(be_ref, nused_ref, hid_ref, wd_ref, o_ref):
    blk = pl.program_id(0)

    @pl.when(blk < nused_ref[0])
    def _():
        o_ref[...] = _pack_bf16_halves(_dot(hid_ref[...], _b(wd_ref[...])))

    @pl.when(blk >= nused_ref[0])
    def _():
        o_ref[...] = jnp.zeros(o_ref.shape, o_ref.dtype)


def _experts(h_packed, slot_tok, block_e, n_valid, n_used, w_gate, w_up, w_down, layer, tm):
    n_slots = slot_tok.shape[0]
    D, DE = w_gate.shape[-2:]
    assert tm % DMA_UNROLL == 0
    up_spec = pltpu.PrefetchScalarGridSpec(
        num_scalar_prefetch=4, grid=(n_slots // tm,),
        in_specs=[pl.BlockSpec(memory_space=pl.ANY),
                  pl.BlockSpec((None, None, D, DE), lambda i, tk, be, nv, nu: (layer, be[i], 0, 0)),
                  pl.BlockSpec((None, None, D, DE), lambda i, tk, be, nv, nu: (layer, be[i], 0, 0))],
        out_specs=pl.BlockSpec((tm, DE), lambda i, tk, be, nv, nu: (i, 0)),
        scratch_shapes=[pltpu.VMEM((2, tm, D // 2), jnp.uint32), pltpu.SemaphoreType.DMA((2,))])
    hid = pl.pallas_call(
        functools.partial(_expert_up_kernel, tm), grid_spec=up_spec,
        out_shape=jax.ShapeDtypeStruct((n_slots, DE), BF16),
        compiler_params=_cp("arbitrary", vmem=58 * 1024 * 1024), name="moe_up",
    )(slot_tok, block_e, n_valid, n_used, h_packed, w_gate, w_up)
    down_spec = pltpu.PrefetchScalarGridSpec(
        num_scalar_prefetch=2, grid=(n_slots // tm,),
        in_specs=[pl.BlockSpec((tm, DE), lambda i, be, nu: (i, 0)),
                  pl.BlockSpec((None, None, DE, D), lambda i, be, nu: (layer, be[i], 0, 0))],
        out_specs=pl.BlockSpec((tm, D // 2), lambda i, be, nu: (i, 0)))
    return pl.pallas_call(
        _expert_down_kernel, grid_spec=down_spec,
        out_shape=jax.ShapeDtypeStruct((n_slots, D // 2), jnp.uint32),
        compiler_params=_cp("arbitrary"), name="moe_down",
    )(block_e, n_used, hid, w_down)


def _combine_kernel(rows, p0_ref, p1_ref, x_ref, g_ref, ga_ref, gb_ref, ys_ref, o_ref, a_scr, b_scr, sem):
    i = pl.program_id(0)

    def start(r, c):
        pltpu.make_async_copy(ys_ref.at[pl.ds(p0_ref[i * rows + r], 1), :], a_scr.at[pl.ds(r, 1), :], sem.at[0]).start()
        pltpu.make_async_copy(ys_ref.at[pl.ds(p1_ref[i * rows + r], 1), :], b_scr.at[pl.ds(r, 1), :], sem.at[1]).start()
        return c

    lax.fori_loop(0, rows, start, 0, unroll=DMA_UNROLL)

    def wait(r, c):
        pltpu.make_async_copy(ys_ref.at[pl.ds(0, 1), :], a_scr.at[pl.ds(r, 1), :], sem.at[0]).wait()
        pltpu.make_async_copy(ys_ref.at[pl.ds(0, 1), :], b_scr.at[pl.ds(r, 1), :], sem.at[1]).wait()
        return c

    lax.fori_loop(0, rows, wait, 0, unroll=DMA_UNROLL)
    a_lo, a_hi = _unpack_bf16_halves(a_scr[...])
    b_lo, b_hi = _unpack_bf16_halves(b_scr[...])
    half = a_lo.shape[1]
    ga = ga_ref[...]
    gb = gb_ref[...]
    o_ref[:, :half] = x_ref[:, :half] + g_ref[:, :half] * (a_lo.astype(F32) * ga + b_lo.astype(F32) * gb)
    o_ref[:, half:] = x_ref[:, half:] + g_ref[:, half:] * (a_hi.astype(F32) * ga + b_hi.astype(F32) * gb)


def _combine(x, mod3, ys, p0, p1, gate, tok, rows):
    T, D = x.shape
    grp = tok.group_of_tile(rows)
    grid_spec = pltpu.PrefetchScalarGridSpec(
        num_scalar_prefetch=2, grid=(T // rows,),
        in_specs=[pl.BlockSpec((rows, D), lambda i, a, b: (i, 0)),
                  pl.BlockSpec((None, 1, D), lambda i, a, b: (grp(i), 0, 5)),
                  pl.BlockSpec((rows, 1), lambda i, a, b: (i, 0)),
                  pl.BlockSpec((rows, 1), lambda i, a, b: (i, 0)),
                  pl.BlockSpec(memory_space=pl.ANY)],
        out_specs=pl.BlockSpec((rows, D), lambda i, a, b: (i, 0)),
        scratch_shapes=[pltpu.VMEM((rows, D // 2), jnp.uint32), pltpu.VMEM((rows, D // 2), jnp.uint32),
                        pltpu.SemaphoreType.DMA((2,))])
    return pl.pallas_call(
        functools.partial(_combine_kernel, rows), grid_spec=grid_spec,
        out_shape=jax.ShapeDtypeStruct((T, D), F32),
        compiler_params=_cp("arbitrary"), name="moe_combine",
    )(p0, p1, x, mod3, gate[:, 0:1], gate[:, 1:2], ys)


def _route(logits, n_groups, n_experts, tm):
    n_tok = logits.shape[0]
    epg = n_experts // n_groups
    gl = logits[:, :n_groups]
    pg = jax.nn.softmax(gl, axis=-1)
    grp = jnp.argmax(gl, axis=-1)
    pg_sel = jnp.max(pg, axis=-1, keepdims=True)
    el = logits[:, n_groups:n_groups + n_experts].reshape(n_tok, n_groups, epg)
    el = jnp.take_along_axis(el, grp[:, None, None], axis=1)[:, 0]
    top_p, top_i = lax.top_k(jax.nn.softmax(el, axis=-1), TOP_K)
    gate = pg_sel * top_p / jnp.sum(top_p, axis=-1, keepdims=True)
    expert = grp[:, None].astype(jnp.int32) * epg + top_i.astype(jnp.int32)
    flat_e = expert.reshape(-1)
    n_assign = n_tok * TOP_K
    onehot = (flat_e[:, None] == jnp.arange(n_experts, dtype=jnp.int32)[None, :]).astype(jnp.int32)
    csum = jnp.cumsum(onehot, axis=0)
    rank = jnp.take_along_axis(csum, flat_e[:, None], axis=1)[:, 0] - 1
    counts = csum[-1]
    padded = (counts + tm - 1) // tm * tm
    pend = jnp.cumsum(padded)
    pstart = pend - padded
    dest = (pstart[flat_e] + rank).astype(jnp.int32)
    n_blocks = n_assign // tm + n_experts
    n_slots = n_blocks * tm
    slot_tok = jnp.zeros((n_slots,), jnp.int32).at[dest].set(jnp.arange(n_assign, dtype=jnp.int32) // TOP_K)
    block_e = jnp.minimum(jnp.searchsorted(pend, jnp.arange(n_blocks, dtype=jnp.int32) * tm, side='right'),
                          n_experts - 1).astype(jnp.int32)
    n_used = (pend[-1] // tm).astype(jnp.int32).reshape(1)
    blk_start = jnp.arange(n_blocks, dtype=jnp.int32) * tm
    n_valid = jnp.clip((pstart + counts)[block_e] - blk_start, 0, tm).astype(jnp.int32)
    dest2 = dest.reshape(n_tok, TOP_K)
    return slot_tok, gate, block_e, n_valid, n_used, dest2[:, 0], dest2[:, 1]


def _rope_rot_cols(w):
    q = MLA_ROPE // 4
    return jnp.concatenate([-w[..., q:2 * q], w[..., 0:q], -w[..., 3 * q:4 * q], w[..., 2 * q:3 * q]], axis=-1)


def _pad_cols(w, n):
    return jnp.pad(w, ((0, 0), (0, n - w.shape[1])))


def _rope_tables(tok):
    half = MLA_ROPE // 2
    nfreq = half // 2
    inv = ROPE_THETA ** (-jnp.arange(nfreq, dtype=F32) / nfreq)
    t = jnp.arange(tok.dec_seq, dtype=jnp.int32)
    ang_r = (t // GRID_W).astype(F32)[:, None] * inv
    ang_c = (t % GRID_W).astype(F32)[:, None] * inv
    z = jnp.zeros((tok.dec_seq, LANE - MLA_ROPE), F32)
    cos = jnp.concatenate([jnp.cos(ang_r), jnp.cos(ang_r), jnp.cos(ang_c), jnp.cos(ang_c), z], axis=1)
    sin = jnp.concatenate([jnp.sin(ang_r), jnp.sin(ang_r), jnp.sin(ang_c), jnp.sin(ang_c), z], axis=1)
    ctx_cos = jnp.concatenate([jnp.ones((tok.t_ctx, MLA_ROPE), F32), jnp.zeros((tok.t_ctx, LANE - MLA_ROPE), F32)], axis=1)
    cos = jnp.concatenate([ctx_cos] + [cos] * tok.n_lat_seq, axis=0)
    sin = jnp.concatenate([jnp.zeros((tok.t_ctx, LANE), F32)] + [sin] * tok.n_lat_seq, axis=0)
    return cos, sin


def kernel(x_prompt, x_sample, state_gdn, state_rwkv, cache_mla_ckv, cache_mla_krope, c, c_ctx, ada_w, ada_b, norm_mix_w, norm_ffn_w, w_in, gdn_conv_w, gdn_A_log, gdn_dt_bias, gdn_norm_w, rwkv_mu, rwkv_w0, rwkv_w_up, rwkv_a0, rwkv_a_up, rwkv_g_up, rwkv_k_k, rwkv_k_a, rwkv_r_k, rwkv_ln_w, rwkv_ln_b, mla_q_norm_w, mla_q_up, mla_kv_norm_w, mla_kv_up, w_out, router_group_w, router_group_b, router_expert_w, router_expert_b, moe_w_gate, moe_w_up, moe_w_down, final_norm_w):
    B, SEQ, D = x_prompt.shape
    DB, DSEQ, _ = x_sample.shape
    L = w_in.shape[0]
    PAST = cache_mla_ckv.shape[2]
    GH = gdn_A_log.shape[-1]
    GW = GH * GDN_HEAD
    RH = rwkv_r_k.shape[1]
    RW = RH * RWKV_HEAD
    RP = RW // LANE
    DL, AL, GL = rwkv_w_up.shape[2], rwkv_a_up.shape[2], rwkv_g_up.shape[1]
    QR = mla_q_norm_w.shape[1]
    KR = mla_kv_norm_w.shape[1]
    MH = mla_kv_up.shape[2] // (MLA_NOPE + MLA_V)
    NG = router_group_w.shape[2]
    NE = router_expert_w.shape[2]
    assert DL + AL == LANE and GL == LANE and 4 * GH <= LANE and NG + NE <= LANE and 1 + DB <= SUB
    tok = _Tok(B, SEQ, DB, DSEQ)
    T = tok.t
    tm = _pick((256, 128, 64), SEQ, DSEQ)
    tm_moe = 256

    gdn_cols = 4 * GW + 4 * GH
    rwkv_cols = 3 * RW + DL + AL + GL
    o_r = gdn_cols
    o_m = gdn_cols + rwkv_cols

    x = jnp.concatenate([x_prompt.reshape(B * SEQ, D), x_sample.reshape(DB * DSEQ, D)], axis=0)
    cvec = jnp.zeros((SUB, D), F32).at[0].set(c_ctx).at[1:1 + DB].set(c)
    mod = _ada_mod(cvec, ada_w, ada_b)
    cos, sin = _rope_tables(tok)

    new_gdn, new_rwkv, new_ckv, new_kr = [], [], [], []
    for l in range(L):
        mod3 = mod[l].reshape(SUB, 1, 6 * D)
        wl = w_in[l]
        w_gdn = _b(wl[:, :4 * GW])
        w_rwkv = _b(wl[:, o_r:o_r + 3 * RW])
        w_cq = _b(wl[:, o_m:o_m + QR])
        w_ckv = _b(wl[:, o_m + QR:o_m + QR + KR])
        w_kr = wl[:, o_m + QR + KR:o_m + QR + KR + MLA_ROPE]
        w_small = _b(jnp.concatenate([
            _pad_cols(wl[:, 4 * GW:4 * GW + 4 * GH], LANE),
            wl[:, o_r + 3 * RW:o_r + 3 * RW + DL + AL],
            wl[:, o_r + 3 * RW + DL + AL:o_r + rwkv_cols],
            w_kr, _rope_rot_cols(w_kr)], axis=1))

        h1 = _norm_mod(x, norm_mix_w[l], mod3, 0, tok, tm)
        p_gdn = _matmul(h1, w_gdn, name="proj_gdn")
        p_rwkv = _matmul(h1, w_rwkv, name="proj_rwkv")
        p_cq = _matmul(h1, w_cq, name="proj_cq")
        p_ckv = _matmul(h1, w_ckv, name="proj_ckv")
        p_small = _matmul(h1, w_small, name="proj_small")

        gate_par = jnp.zeros((SUB, LANE), F32)
        gate_par = gate_par.at[0, :2 * GH].set(-jnp.exp(gdn_A_log[l].reshape(-1)))
        gate_par = gate_par.at[1, :2 * GH].set(gdn_dt_bias[l].reshape(-1))
        qkv, gates = _gdn_pre(p_gdn, p_small, gdn_conv_w[l].T, gate_par, tok, GH, tm)
        gcol = gates[:, :4 * GH].reshape(T // CHUNK, CHUNK, 4 * GH)
        grow = jnp.swapaxes(gates[:, :2 * GH].reshape(T // CHUNK, CHUNK, 2 * GH), 1, 2).reshape(
            T // CHUNK, GH, 2 * CHUNK)
        oc_f, sc_f = _gdn_chunk(qkv, gcol, grow, 0, B, SEQ, GH, False, emit=True)
        oc_b, sc_b = _gdn_chunk(qkv, gcol, grow, 0, B, SEQ, GH, True, emit=True)
        ol_f, = _gdn_chunk(qkv, gcol, grow, tok.t_ctx, DB, DSEQ, GH, False, s0=state_gdn, s0_idx=(l, 0))
        ol_b, = _gdn_chunk(qkv, gcol, grow, tok.t_ctx, DB, DSEQ, GH, True, s0=state_gdn, s0_idx=(l, 1))
        o_gdn = _gdn_out(jnp.concatenate([oc_f, ol_f], axis=0), jnp.concatenate([oc_b, ol_b], axis=0),
                         p_gdn, gdn_norm_w[l], GH, tm)
        new_gdn.append(jnp.stack([sc_f, sc_b], axis=1))

        ln = (rwkv_ln_w[l].reshape(1, RW), rwkv_ln_b[l].reshape(1, RW), rwkv_r_k[l].reshape(1, RW))
        y_dir, s_dir = [], []
        for d in range(2):
            mu = rwkv_mu[l, d]
            wup = jnp.concatenate([rwkv_w_up[l, d], jnp.zeros((AL, RW), F32)], axis=0)
            aup = jnp.concatenate([jnp.zeros((DL, RW), F32), rwkv_a_up[l, d]], axis=0)
            par = (mu[:RW].reshape(1, RW), mu[RW:2 * RW].reshape(1, RW), mu[2 * RW:3 * RW].reshape(1, RW),
                   mu[3 * RW:].reshape(1, LANE), rwkv_w0[l, d].reshape(1, RW), wup,
                   rwkv_a0[l, d].reshape(1, RW), aup, rwkv_k_k[l].reshape(1, RW), rwkv_k_a[l].reshape(1, RW))
            s0 = state_rwkv[:, l, d].reshape(DB, RP, 2, RWKV_HEAD, RWKV_HEAD)
            s0 = jnp.einsum('bpaij,ac->bpaicj', s0, jnp.eye(2, dtype=F32)).reshape(DB, RP, LANE, LANE)
            yc, sc = _rwkv_chunk(p_rwkv, p_small, par, ln, 0, B, SEQ, d == 1, emit=True)
            yl, = _rwkv_chunk(p_rwkv, p_small, par, ln, tok.t_ctx, DB, DSEQ, d == 1, s0=s0)
            y_dir.append(jnp.concatenate([yc, yl], axis=0))
            sc = sc.reshape(B, RP, 2, RWKV_HEAD, 2, RWKV_HEAD)
            s_dir.append(jnp.einsum('bpaiaj->bpaij', sc).reshape(B, RH, RWKV_HEAD, RWKV_HEAD))
        o_rwkv = _rwkv_out(y_dir[0], y_dir[1], p_small, rwkv_g_up[l], tm)
        new_rwkv.append(jnp.stack(s_dir, axis=1))

        cqn, ckvn, krf = _mla_pre(p_cq, p_ckv, p_small, cos, sin, mla_q_norm_w[l], mla_kv_norm_w[l], tm)
        qu = mla_q_up[l].reshape(QR, MH, MLA_QK)
        q_rope = qu[:, :, MLA_NOPE:]
        w_q = _b(jnp.concatenate([qu[:, :, :MLA_NOPE].reshape(QR, MH * MLA_NOPE),
                                  jnp.concatenate([q_rope, _rope_rot_cols(q_rope)], axis=-1).reshape(QR, MH * LANE)],
                                 axis=1))
        q = _matmul(cqn, w_q, name="q_up")
        kr_cache = _b(jnp.pad(cache_mla_krope[:, l], ((0, 0), (0, 0), (0, LANE - MLA_ROPE))))
        ckv_parts, kr_parts = [], []
        for b in range(DB):
            r0 = tok.t_ctx + b * DSEQ
            ckv_parts += [_b(cache_mla_ckv[b, l]), _b(ckvn[r0:r0 + DSEQ])]
            kr_parts += [kr_cache[b], krf[r0:r0 + DSEQ]]
        ckv_all = jnp.concatenate(ckv_parts + [_b(ckvn[:tok.t_ctx])], axis=0)
        kr_all = jnp.concatenate(kr_parts + [krf[:tok.t_ctx]], axis=0)
        kv = _matmul(ckv_all, _b(mla_kv_up[l]), out_dtype=BF16, name="kv_up")
        o_c = _attention(q, cos, sin, kv, kr_all, 0, DB * (PAST + DSEQ), B, SEQ, SEQ, MH)
        o_l = _attention(q, cos, sin, kv, kr_all, tok.t_ctx, 0, DB, DSEQ, PAST + DSEQ, MH)
        o_mla = jnp.concatenate([o_c, o_l], axis=0)
        new_ckv.append(ckvn[:tok.t_ctx].reshape(B, SEQ, KR))
        new_kr.append(p_small[:tok.t_ctx, 3 * LANE:3 * LANE + MLA_ROPE].reshape(B, SEQ, MLA_ROPE))

        x = _wout_residual(x, mod3, o_gdn, o_rwkv, o_mla, _b(w_out[l]), tok)

        rw = _pad_cols(jnp.concatenate([router_group_w[l], router_expert_w[l]], axis=1), LANE)
        rb = _pad_cols(jnp.concatenate([router_group_b[l], router_expert_b[l]])[None, :], LANE)
        h2, logits = _norm_mod(x, norm_ffn_w[l], mod3, 3, tok, tm, router=(rw, rb))
        slot_tok, gate, block_e, n_valid, n_used, p0, p1 = _route(logits, NG, NE, tm_moe)
        ys = _experts(h2, slot_tok, block_e, n_valid, n_used, moe_w_gate, moe_w_up, moe_w_down, l, tm_moe)
        x = _combine(x, mod3, ys, p0, p1, gate, tok, tm)

    y = _final_norm(x, final_norm_w, tm)
    y_prompt = y[:tok.t_ctx].reshape(B, SEQ, D)
    y_sample = y[tok.t_ctx:].reshape(DB, DSEQ, D)
    return (y_prompt, y_sample, jnp.stack(new_gdn, axis=1), jnp.stack(new_rwkv, axis=1),
            jnp.stack(new_ckv, axis=1), jnp.stack(new_kr, axis=1))
```
